```python
import math
import jax
import jax.numpy as jnp
from jax import lax
import numpy as np

D_MODEL = 1024
BATCH = 8
SEQ = 4096
DEPTH = 4

N_MIXERS = 4
N_CONV = (DEPTH + 3) // N_MIXERS
N_SGU = (DEPTH + 2) // N_MIXERS
N_GDN = (DEPTH + 1) // N_MIXERS
N_MLA = DEPTH // N_MIXERS
N_DENSE = (DEPTH + 1) // 2
N_MOE = DEPTH // 2

DEEPNORM_ALPHA = (2 * DEPTH) ** 0.25
DEEPNORM_BETA = (8 * DEPTH) ** -0.25

CONV_WIDTH = 31
SGU_CHUNK = 128
SGU_GROUPS = 8
SGU_HALF = 2 * D_MODEL
GDN_HEADS = 8
GDN_DK = 128
GDN_DV = 128
GDN_CONV = 4
GDN_CHUNK = 64
GDN_QKV = 2 * GDN_HEADS * GDN_DK + GDN_HEADS * GDN_DV
GDN_PROJ = GDN_QKV + GDN_HEADS * GDN_DV + 2 * GDN_HEADS
MLA_HEADS = 8
MLA_Q_RANK = 512
MLA_KV_RANK = 256
MLA_NOPE = 128
MLA_ROPE = 64
MLA_V = 128
ROPE_THETA = 10000.0
Q_BLOCK = 128
FFN_HIDDEN = 7 * D_MODEL // 2
N_EXPERTS = 8
TOP_K = 2
MOE_BLOCK = 256

kernel_name = "hybrid_conv_sgu_gdn_mla_moe_deepnorm_adaln"


def _layernorm(x, g, b, eps=1e-5):
    xf = x.astype(jnp.float32)
    mu = jnp.mean(xf, -1, keepdims=True)
    var = jnp.mean(jnp.square(xf - mu), -1, keepdims=True)
    return ((xf - mu) * lax.rsqrt(var + eps)).astype(x.dtype) * g + b


def _rmsnorm(x, g, eps=1e-6):
    xf = x.astype(jnp.float32)
    return (xf * lax.rsqrt(jnp.mean(xf * xf, -1, keepdims=True) + eps)).astype(x.dtype) * g


def _l2norm(x, eps=1e-6):
    xf = x.astype(jnp.float32)
    return xf * lax.rsqrt(jnp.sum(xf * xf, -1, keepdims=True) + eps)


def _causal_dwconv(x, w):
    k, ch = w.shape
    return lax.conv_general_dilated(
        x, w[:, None, :].astype(x.dtype), window_strides=(1,), padding=((k - 1, 0),),
        dimension_numbers=("NWC", "WIO", "NWC"), feature_group_count=ch)


def _rope(x, positions):
    half = x.shape[-1] // 2
    inv_freq = ROPE_THETA ** (-jnp.arange(half, dtype=jnp.float32) / half)
    ang = positions.astype(jnp.float32)[..., None, None] * inv_freq
    cos, sin = jnp.cos(ang).astype(x.dtype), jnp.sin(ang).astype(x.dtype)
    x1, x2 = x[..., :half], x[..., half:]
    return jnp.concatenate([x1 * cos - x2 * sin, x2 * cos + x1 * sin], -1)


def conformer_conv(h, w_in, dw, dw_b, ln_g, ln_b, w_out):
    a, g = jnp.split(h @ w_in, 2, -1)
    y = a * jax.nn.sigmoid(g)
    y = _causal_dwconv(y, dw) + dw_b
    y = jax.nn.silu(_layernorm(y, ln_g, ln_b))
    return y @ w_out


def chunked_spatial_gating(h, w_in, b_in, ln_g, ln_b, w_s, b_s, w_out):
    bsz, s, _ = h.shape
    z = jax.nn.gelu(h @ w_in + b_in, approximate=False)
    u, v = jnp.split(z, 2, -1)
    v = _layernorm(v, ln_g, ln_b)
    v = v.reshape(bsz, s // SGU_CHUNK, SGU_CHUNK, SGU_GROUPS, SGU_HALF // SGU_GROUPS)
    causal = jnp.tril(jnp.ones((SGU_CHUNK, SGU_CHUNK), dtype=bool))
    ws = jnp.where(causal, w_s, jnp.zeros((), w_s.dtype))
    sv = jnp.einsum("gts,bnsgd->bntgd", ws, v) + b_s.T[None, None, :, :, None]
    return (u * sv.reshape(bsz, s, SGU_HALF)) @ w_out


def _chunk_gated_delta(q, k, v, g, beta):
    bsz, s, nh, dk = q.shape
    dv = v.shape[-1]
    c = GDN_CHUNK
    n = s // c
    f32 = jnp.float32

    def chunks(t):
        t = t.astype(f32).reshape(bsz, n, c, nh, *t.shape[3:])
        return jnp.moveaxis(t, 3, 1)

    q, k, v, g, beta = (chunks(t) for t in (q, k, v, g, beta))
    gc = jnp.cumsum(g, -1)
    causal = jnp.tril(jnp.ones((c, c), dtype=bool))
    strict = jnp.tril(jnp.ones((c, c), dtype=bool), -1)
    diff = gc[..., :, None] - gc[..., None, :]
    decay = jnp.where(causal, jnp.exp(jnp.where(causal, diff, 0.0)), 0.0)
    kk = jnp.einsum("bhncd,bhnsd->bhncs", k, k)
    a = jnp.where(strict, beta[..., None] * kk * decay, 0.0)
    rhs = jnp.concatenate([v * beta[..., None], k * (beta * jnp.exp(gc))[..., None]], -1)
    sol = lax.linalg.triangular_solve(a + jnp.eye(c, dtype=f32), rhs, left_side=True,
                                      lower=True, unit_diagonal=True)
    u, w = jnp.split(sol, [dv], -1)
    qk = jnp.where(causal, jnp.einsum("bhncd,bhnsd->bhncs", q, k) * decay, 0.0)
    q_dec = q * jnp.exp(gc)[..., None]
    k_dec = k * jnp.exp(gc[..., -1:] - gc)[..., None]
    last = jnp.exp(gc[..., -1])

    def step(state, inp):
        q_i, k_i, u_i, w_i, qk_i, last_i = inp
        v_new = u_i - jnp.einsum("bhcd,bhde->bhce", w_i, state)
        o_i = jnp.einsum("bhcd,bhde->bhce", q_i, state) + jnp.einsum("bhcs,bhse->bhce", qk_i, v_new)
        state = state * last_i[..., None, None] + jnp.einsum("bhcd,bhce->bhde", k_i, v_new)
        return state, o_i

    xs = tuple(jnp.moveaxis(t, 2, 0) for t in (q_dec, k_dec, u, w, qk, last))
    _, o = lax.scan(step, jnp.zeros((bsz, nh, dk, dv), f32), xs)
    return jnp.transpose(o, (1, 0, 3, 2, 4)).reshape(bsz, s, nh, dv)


def gated_deltanet(h, w_in, conv_w, a_log, dt_bias, norm_g, w_out):
    bsz, s, _ = h.shape
    nh = GDN_HEADS
    qkv, z, b_raw, a_raw = jnp.split(
        h @ w_in, [GDN_QKV, GDN_QKV + nh * GDN_DV, GDN_QKV + nh * GDN_DV + nh], -1)
    qkv = jax.nn.silu(_causal_dwconv(qkv, conv_w))
    q, k, v = jnp.split(qkv, [nh * GDN_DK, 2 * nh * GDN_DK], -1)
    q = _l2norm(q.reshape(bsz, s, nh, GDN_DK)) * (GDN_DK ** -0.5)
    k = _l2norm(k.reshape(bsz, s, nh, GDN_DK))
    v = v.reshape(bsz, s, nh, GDN_DV)
    beta = jax.nn.sigmoid(b_raw.astype(jnp.float32))
    g = -jnp.exp(a_log.astype(jnp.float32)) * jax.nn.softplus(
        a_raw.astype(jnp.float32) + dt_bias.astype(jnp.float32))
    o = _chunk_gated_delta(q, k, v, g, beta).astype(h.dtype)
    o = _rmsnorm(o, norm_g) * jax.nn.silu(z.reshape(bsz, s, nh, GDN_DV))
    return o.reshape(bsz, s, nh * GDN_DV) @ w_out


def latent_attention(h, positions, w_in, q_norm_g, kv_norm_g, w_uq, w_ukv, w_out):
    bsz, s, _ = h.shape
    nh = MLA_HEADS
    c_q, c_kv, k_rope = jnp.split(h @ w_in, [MLA_Q_RANK, MLA_Q_RANK + MLA_KV_RANK], -1)
    q = (_rmsnorm(c_q, q_norm_g) @ w_uq).reshape(bsz, s, nh, MLA_NOPE + MLA_ROPE)
    kv = (_rmsnorm(c_kv, kv_norm_g) @ w_ukv).reshape(bsz, s, nh, MLA_NOPE + MLA_V)
    q_nope, q_rope = jnp.split(q, [MLA_NOPE], -1)
    k_nope, v = jnp.split(kv, [MLA_NOPE], -1)
    q_rope = _rope(q_rope, positions)
    k_rope = _rope(k_rope[:, :, None, :], positions)[:, :, 0]
    scale = (MLA_NOPE + MLA_ROPE) ** -0.5
    nb = s // Q_BLOCK
    qn_b = jnp.moveaxis(q_nope.reshape(bsz, nb, Q_BLOCK, nh, MLA_NOPE), 1, 0)
    qr_b = jnp.moveaxis(q_rope.reshape(bsz, nb, Q_BLOCK, nh, MLA_ROPE), 1, 0)
    key_idx = jnp.arange(s)

    def block(args):
        i, qn, qr = args
        sc = (jnp.einsum("bqhd,bkhd->bhqk", qn, k_nope)
              + jnp.einsum("bqhd,bkd->bhqk", qr, k_rope)).astype(jnp.float32) * scale
        q_idx = i * Q_BLOCK + jnp.arange(Q_BLOCK)
        sc = jnp.where(key_idx[None, :] <= q_idx[:, None], sc, -jnp.inf)
        p = jax.nn.softmax(sc, -1).astype(v.dtype)
        return jnp.einsum("bhqk,bkhd->bqhd", p, v)

    o = lax.map(block, (jnp.arange(nb), qn_b, qr_b))
    return jnp.moveaxis(o, 0, 1).reshape(bsz, s, nh * MLA_V) @ w_out


def swiglu(h, w_in, w_out):
    a, b = jnp.split(h @ w_in, 2, -1)
    return (jax.nn.silu(a) * b) @ w_out


def moe_swiglu(h, router, w_in, w_out):
    bsz, s, d = h.shape
    t = h.reshape(-1, d)
    n = t.shape[0]
    nk = n * TOP_K
    logits = (t @ router).astype(jnp.float32)
    top_logit, top_e = lax.top_k(logits, TOP_K)
    gate = jax.nn.softmax(top_logit, -1)
    flat_e = top_e.reshape(-1)
    flat_tok = jnp.repeat(jnp.arange(n, dtype=jnp.int32), TOP_K)
    flat_gate = gate.reshape(-1)
    order = jnp.argsort(flat_e)
    se, stok, sgate = flat_e[order], flat_tok[order], flat_gate[order]
    counts = jnp.bincount(flat_e, length=N_EXPERTS)
    start = jnp.cumsum(counts) - counts
    padded = (counts + MOE_BLOCK - 1) // MOE_BLOCK * MOE_BLOCK
    pend = jnp.cumsum(padded)
    pstart = pend - padded
    dest = pstart[se] + (jnp.arange(nk, dtype=jnp.int32) - start[se])
    n_blocks = (nk + N_EXPERTS * (MOE_BLOCK - 1) + MOE_BLOCK - 1) // MOE_BLOCK
    cap = n_blocks * MOE_BLOCK
    buf_tok = jnp.zeros((cap,), jnp.int32).at[dest].set(stok)
    buf_gate = jnp.zeros((cap,), jnp.float32).at[dest].set(sgate)
    block_e = jnp.clip(jnp.searchsorted(pend, jnp.arange(n_blocks) * MOE_BLOCK, side="right"),
                       0, N_EXPERTS - 1)
    xb = t[buf_tok].reshape(n_blocks, MOE_BLOCK, d)

    def expert_block(args):
        xe, e = args
        a, b = jnp.split(xe @ w_in[e], 2, -1)
        return (jax.nn.silu(a) * b) @ w_out[e]

    yb = lax.map(expert_block, (xb, block_e)).reshape(cap, d)
    y = jnp.zeros_like(t).at[buf_tok].add(yb * buf_gate[:, None].astype(yb.dtype))
    return y.reshape(bsz, s, d)


def _normal(key, shape, fan_in, scale=1.0):
    return jax.random.normal(key, shape, jnp.float32) * (scale * fan_in ** -0.5)


def _gain(key, shape):
    return 1.0 + 0.02 * jax.random.normal(key, shape, jnp.float32)


def _bias(key, shape, scale=0.02):
    return scale * jax.random.normal(key, shape, jnp.float32)


def setup_inputs(seed: int = 0) -> dict:
    key = jax.random.key(seed)
    ks = iter(jax.random.split(key, 64))
    D = D_MODEL
    beta = DEEPNORM_BETA
    dt = jnp.exp(jax.random.uniform(next(ks), (N_GDN, GDN_HEADS), jnp.float32,
                                    math.log(1e-3), math.log(1e-1)))
    return {
        "x": jax.random.normal(next(ks), (BATCH, SEQ, D), jnp.float32),
        "c": jax.random.normal(next(ks), (BATCH, D), jnp.float32),
        "positions": jnp.arange(SEQ, dtype=jnp.int32)[None, :]
        + jax.random.randint(next(ks), (BATCH, 1), 0, 1024, dtype=jnp.int32),
        "ada_w": _normal(next(ks), (DEPTH, D, 6 * D), D, 0.1),
        "ada_b": _bias(next(ks), (DEPTH, 6 * D), 0.01),
        "ln_g": _gain(next(ks), (DEPTH, 2, D)),
        "ln_b": _bias(next(ks), (DEPTH, 2, D)),
        "conv_w_in": _normal(next(ks), (N_CONV, D, 2 * D), D),
        "conv_dw": _normal(next(ks), (N_CONV, CONV_WIDTH, D), CONV_WIDTH),
        "conv_dw_b": _bias(next(ks), (N_CONV, D)),
        "conv_ln_g": _gain(next(ks), (N_CONV, D)),
        "conv_ln_b": _bias(next(ks), (N_CONV, D)),
        "conv_w_out": _normal(next(ks), (N_CONV, D, D), D, beta),
        "sgu_w_in": _normal(next(ks), (N_SGU, D, 2 * SGU_HALF), D),
        "sgu_b_in": _bias(next(ks), (N_SGU, 2 * SGU_HALF)),
        "sgu_ln_g": _gain(next(ks), (N_SGU, SGU_HALF)),
        "sgu_ln_b": _bias(next(ks), (N_SGU, SGU_HALF)),
        "sgu_w_s": _normal(next(ks), (N_SGU, SGU_GROUPS, SGU_CHUNK, SGU_CHUNK), SGU_CHUNK),
        "sgu_b_s": _gain(next(ks), (N_SGU, SGU_GROUPS, SGU_CHUNK)),
        "sgu_w_out": _normal(next(ks), (N_SGU, SGU_HALF, D), SGU_HALF, beta),
        "gdn_w_in": _normal(next(ks), (N_GDN, D, GDN_PROJ), D),
        "gdn_conv": _normal(next(ks), (N_GDN, GDN_CONV, GDN_QKV), GDN_CONV),
        "gdn_a_log": jnp.log(jax.random.uniform(next(ks), (N_GDN, GDN_HEADS), jnp.float32, 1.0, 16.0)),
        "gdn_dt_bias": dt + jnp.log(-jnp.expm1(-dt)),
        "gdn_norm_g": _gain(next(ks), (N_GDN, GDN_DV)),
        "gdn_w_out": _normal(next(ks), (N_GDN, GDN_HEADS * GDN_DV, D), GDN_HEADS * GDN_DV, beta),
        "mla_w_in": _normal(next(ks), (N_MLA, D, MLA_Q_RANK + MLA_KV_RANK + MLA_ROPE), D),
        "mla_q_norm_g": _gain(next(ks), (N_MLA, MLA_Q_RANK)),
        "mla_kv_norm_g": _gain(next(ks), (N_MLA, MLA_KV_RANK)),
        "mla_w_uq": _normal(next(ks), (N_MLA, MLA_Q_RANK, MLA_HEADS * (MLA_NOPE + MLA_ROPE)), MLA_Q_RANK),
        "mla_w_ukv": _normal(next(ks), (N_MLA, MLA_KV_RANK, MLA_HEADS * (MLA_NOPE + MLA_V)), MLA_KV_RANK),
        "mla_w_out": _normal(next(ks), (N_MLA, MLA_HEADS * MLA_V, D), MLA_HEADS * MLA_V, beta),
        "ffn_w_in": _normal(next(ks), (N_DENSE, D, 2 * FFN_HIDDEN), D),
        "ffn_w_out": _normal(next(ks), (N_DENSE, FFN_HIDDEN, D), FFN_HIDDEN, beta),
        "moe_router": _normal(next(ks), (N_MOE, D, N_EXPERTS), D),
        "moe_w_in": _normal(next(ks), (N_MOE, N_EXPERTS, D, 2 * FFN_HIDDEN), D),
        "moe_w_out": _normal(next(ks), (N_MOE, N_EXPERTS, FFN_HIDDEN, D), FFN_HIDDEN, beta),
    }


def reference(x, c, positions, ada_w, ada_b, ln_g, ln_b,
              conv_w_in, conv_dw, conv_dw_b, conv_ln_g, conv_ln_b, conv_w_out,
              sgu_w_in, sgu_b_in, sgu_ln_g, sgu_ln_b, sgu_w_s, sgu_b_s, sgu_w_out,
              gdn_w_in, gdn_conv, gdn_a_log, gdn_dt_bias, gdn_norm_g, gdn_w_out,
              mla_w_in, mla_q_norm_g, mla_kv_norm_g, mla_w_uq, mla_w_ukv, mla_w_out,
              ffn_w_in, ffn_w_out, moe_router, moe_w_in, moe_w_out):
    cond = jax.nn.silu(c)
    for i in range(DEPTH):
        mod = (cond @ ada_w[i] + ada_b[i])[:, None, :]
        sh1, sc1, g1, sh2, sc2, g2 = jnp.split(mod, 6, -1)
        h = x * (1 + sc1) + sh1
        j = i // N_MIXERS
        mixer = i % N_MIXERS
        if mixer == 0:
            y = conformer_conv(h, conv_w_in[j], conv_dw[j], conv_dw_b[j], conv_ln_g[j],
                               conv_ln_b[j], conv_w_out[j])
        elif mixer == 1:
            y = chunked_spatial_gating(h, sgu_w_in[j], sgu_b_in[j], sgu_ln_g[j], sgu_ln_b[j],
                                       sgu_w_s[j], sgu_b_s[j], sgu_w_out[j])
        elif mixer == 2:
            y = gated_deltanet(h, gdn_w_in[j], gdn_conv[j], gdn_a_log[j], gdn_dt_bias[j],
                               gdn_norm_g[j], gdn_w_out[j])
        else:
            y = latent_attention(h, positions, mla_w_in[j], mla_q_norm_g[j], mla_kv_norm_g[j],
                                 mla_w_uq[j], mla_w_ukv[j], mla_w_out[j])
        x = _layernorm(DEEPNORM_ALPHA * x + (1 + g1) * y, ln_g[i, 0], ln_b[i, 0])
        h = x * (1 + sc2) + sh2
        if i % 2 == 0:
            y = swiglu(h, ffn_w_in[i // 2], ffn_w_out[i // 2])
        else:
            y = moe_swiglu(h, moe_router[i // 2], moe_w_in[i // 2], moe_w_out[i // 2])
        x = _layernorm(DEEPNORM_ALPHA * x + (1 + g2) * y, ln_g[i, 1], ln_b[i, 1])
    return x
```

```python
import functools
import math

import jax
import jax.numpy as jnp
from jax import lax
from jax.experimental import pallas as pl
from jax.experimental.pallas import tpu as pltpu

F32 = jnp.float32
BF16 = jnp.bfloat16
HI = lax.Precision.HIGHEST

D_MODEL = 1024
DEPTH = 4
ALPHA = (2 * DEPTH) ** 0.25

CONV_WIDTH = 31
CONV_HALO = 32
SGU_CHUNK = 128
SGU_GROUPS = 8
SGU_HALF = 2 * D_MODEL
GDN_HEADS = 8
GDN_DK = 128
GDN_DV = 128
GDN_CONV = 4
GDN_TILE = 256
GDN_TAIL = 8
MLA_HEADS = 8
MLA_Q_RANK = 512
MLA_KV_RANK = 256
MLA_NOPE = 128
MLA_ROPE = 64
MLA_V = 128
ROPE_THETA = 10000.0
FFN_HIDDEN = 7 * D_MODEL // 2
N_EXPERTS = 8
TOP_K = 2
LANES = 128
SUBLANES = 8
MOE_BLOCK = 512
VMEM_LIMIT = 56 * 1024 * 1024


def _cparams(sem):
    return pltpu.CompilerParams(dimension_semantics=sem, vmem_limit_bytes=VMEM_LIMIT)


def _sigmoid(x):
    return 1.0 / (1.0 + jnp.exp(-x))


def _silu(x):
    return x * _sigmoid(x)


def _ln(z, g, b, eps=1e-5):
    mu = jnp.mean(z, -1, keepdims=True)
    zc = z - mu
    var = jnp.mean(zc * zc, -1, keepdims=True)
    return zc * lax.rsqrt(var + eps) * g + b


def _res_ln(x, y, gate, g, b):
    return _ln(ALPHA * x + (1.0 + gate) * y, g, b)


def _dot(a, b):
    return jnp.dot(a, b, preferred_element_type=F32)


def _dot_nt(a, b):
    return lax.dot_general(a, b, (((1,), (1,)), ((), ())), preferred_element_type=F32)


def _dot_tn(a, b):
    return lax.dot_general(a, b, (((0,), (0,)), ((), ())), preferred_element_type=F32)


def _dot_hi(a, b):
    return jnp.dot(a, b, precision=HI, preferred_element_type=F32)


def _ada_kernel(c_ref, w_ref, b_ref, o_ref):
    cond = _silu(c_ref[...])
    o_ref[0] = _dot_hi(cond, w_ref[0]) + b_ref[0]


def _ada_mod(c, ada_w, ada_b):
    bsz, d = c.shape
    n_out = ada_w.shape[-1]
    tn = 1024
    return pl.pallas_call(
        _ada_kernel,
        grid=(DEPTH, n_out // tn),
        in_specs=[
            pl.BlockSpec((bsz, d), lambda i, j: (0, 0)),
            pl.BlockSpec((1, d, tn), lambda i, j: (i, 0, j)),
            pl.BlockSpec((1, 1, tn), lambda i, j: (i, 0, j)),
        ],
        out_specs=pl.BlockSpec((1, bsz, tn), lambda i, j: (i, 0, j)),
        out_shape=jax.ShapeDtypeStruct((DEPTH, bsz, n_out), F32),
        compiler_params=_cparams(("parallel", "parallel")),
        name="ada_mod",
    )(c, ada_w, ada_b.reshape(DEPTH, 1, n_out))


def _row_spec(tm, width):
    return pl.BlockSpec((1, tm, width), lambda b, i: (b, i, 0))


def _mod_spec(width=D_MODEL):
    return pl.BlockSpec((1, 1, width), lambda b, i: (b, 0, 0))


def _full_spec(shape):
    nd = len(shape)
    return pl.BlockSpec(shape, lambda b, i: (0,) * nd)


def _outproj_kernel(a_ref, w_ref, x_ref, gate_ref, g_ref, b_ref, o_ref):
    y = _dot(a_ref[0], w_ref[...])
    o_ref[0] = _res_ln(x_ref[0], y, gate_ref[0], g_ref[...], b_ref[...])


def _outproj_ln(a, w, x, gate, ln_g, ln_b, tm=512):
    bsz, s, k = a.shape
    d = x.shape[-1]
    return pl.pallas_call(
        _outproj_kernel,
        grid=(bsz, s // tm),
        in_specs=[_row_spec(tm, k), _full_spec((k, d)), _row_spec(tm, d), _mod_spec(d),
                  _full_spec((1, d)), _full_spec((1, d))],
        out_specs=_row_spec(tm, d),
        out_shape=jax.ShapeDtypeStruct(x.shape, F32),
        compiler_params=_cparams(("parallel", "parallel")),
        name="outproj_ln",
    )(a, w, x, gate, ln_g, ln_b)


def _conv_in_kernel(x_ref, sc_ref, sh_ref, w_ref, o_ref):
    h = (x_ref[0] * (1.0 + sc_ref[0]) + sh_ref[0]).astype(BF16)
    ag = _dot(h, w_ref[...])
    d = o_ref.shape[-1]
    o_ref[0] = ag[:, :d] * _sigmoid(ag[:, d:])


def _conv_mid_kernel(y_ref, halo_ref, dw_ref, dwb_ref, cg_ref, cb_ref, w_ref,
                     x_ref, gate_ref, g_ref, b_ref, o_ref, ybuf, cbuf):
    tm = y_ref.shape[1]
    d = y_ref.shape[2]
    first = pl.program_id(1) == 0
    ybuf[0:CONV_HALO, :] = jnp.where(first, 0.0, halo_ref[0])
    ybuf[CONV_HALO:, :] = y_ref[0]
    rc = 64
    off = CONV_HALO - (CONV_WIDTH - 1)
    for c0 in range(0, d, LANES):
        taps = [dw_ref[k:k + 1, c0:c0 + LANES] for k in range(CONV_WIDTH)]
        for r0 in range(0, tm, rc):
            acc = taps[0] * ybuf[r0 + off:r0 + off + rc, c0:c0 + LANES]
            for k in range(1, CONV_WIDTH):
                acc = acc + taps[k] * ybuf[r0 + off + k:r0 + off + k + rc, c0:c0 + LANES]
            cbuf[r0:r0 + rc, c0:c0 + LANES] = acc
    yc = cbuf[...] + dwb_ref[...]
    yn = _silu(_ln(yc, cg_ref[...], cb_ref[...]))
    yo = _dot(yn.astype(BF16), w_ref[...])
    o_ref[0] = _res_ln(x_ref[0], yo, gate_ref[0], g_ref[...], b_ref[...])


def _conformer_layer(x, sc, sh, gate, ln_g, ln_b, w_in, dw, dw_b, cg, cb, w_out, tm=512):
    bsz, s, d = x.shape
    y = pl.pallas_call(
        _conv_in_kernel,
        grid=(bsz, s // tm),
        in_specs=[_row_spec(tm, d), _mod_spec(), _mod_spec(), _full_spec((d, 2 * d))],
        out_specs=_row_spec(tm, d),
        out_shape=jax.ShapeDtypeStruct((bsz, s, d), F32),
        compiler_params=_cparams(("parallel", "parallel")),
        name="conv_in",
    )(x, sc, sh, w_in.astype(BF16))
    hb = tm // CONV_HALO
    halo_spec = pl.BlockSpec((1, CONV_HALO, d), lambda b, i: (b, jnp.maximum(i * hb - 1, 0), 0))
    dw_pad = jnp.concatenate([dw, jnp.zeros((1, d), F32)], 0)
    return pl.pallas_call(
        _conv_mid_kernel,
        grid=(bsz, s // tm),
        in_specs=[_row_spec(tm, d), halo_spec, _full_spec((CONV_WIDTH + 1, d)), _full_spec((1, d)),
                  _full_spec((1, d)), _full_spec((1, d)), _full_spec((d, d)),
                  _row_spec(tm, d), _mod_spec(), _full_spec((1, d)), _full_spec((1, d))],
        out_specs=_row_spec(tm, d),
        out_shape=jax.ShapeDtypeStruct((bsz, s, d), F32),
        scratch_shapes=[pltpu.VMEM((tm + CONV_HALO, d), F32), pltpu.VMEM((tm, d), F32)],
        compiler_params=_cparams(("parallel", "parallel")),
        name="conv_mid",
    )(y, y, dw_pad, dw_b.reshape(1, d), cg.reshape(1, d), cb.reshape(1, d), w_out.astype(BF16),
      x, gate, ln_g, ln_b)


def _swiglu_kernel(x_ref, sc_ref, sh_ref, wa_ref, wb_ref, wo_ref, gate_ref, g_ref, b_ref,
                   o_ref, hbuf, acc):
    j = pl.program_id(2)

    @pl.when(j == 0)
    def _():
        hbuf[...] = (x_ref[0] * (1.0 + sc_ref[0]) + sh_ref[0]).astype(BF16)
        acc[...] = jnp.zeros_like(acc)

    h = hbuf[...]
    a = _dot(h, wa_ref[...])
    b = _dot(h, wb_ref[...])
    act = (_silu(a) * b).astype(BF16)
    acc[...] += _dot(act, wo_ref[...])

    @pl.when(j == pl.num_programs(2) - 1)
    def _():
        o_ref[0] = _res_ln(x_ref[0], acc[...], gate_ref[0], g_ref[...], b_ref[...])


def _swiglu_layer(x, sc, sh, gate, ln_g, ln_b, w_in, w_out, tm=512, th=512):
    bsz, s, d = x.shape
    hid = w_out.shape[0]
    nj = hid // th
    row = pl.BlockSpec((1, tm, d), lambda b, i, j: (b, i, 0))
    mod = pl.BlockSpec((1, 1, d), lambda b, i, j: (b, 0, 0))
    vec = pl.BlockSpec((1, d), lambda b, i, j: (0, 0))
    w_in_b = w_in.astype(BF16)
    return pl.pallas_call(
        _swiglu_kernel,
        grid=(bsz, s // tm, nj),
        in_specs=[row, mod, mod,
                  pl.BlockSpec((d, th), lambda b, i, j: (0, j)),
                  pl.BlockSpec((d, th), lambda b, i, j: (0, j + nj)),
                  pl.BlockSpec((th, d), lambda b, i, j: (j, 0)),
                  mod, vec, vec],
        out_specs=row,
        out_shape=jax.ShapeDtypeStruct(x.shape, F32),
        scratch_shapes=[pltpu.VMEM((tm, d), BF16), pltpu.VMEM((tm, d), F32)],
        compiler_params=_cparams(("parallel", "parallel", "arbitrary")),
        name="swiglu",
    )(x, sc, sh, w_in_b, w_in_b, w_out.astype(BF16), gate, ln_g, ln_b)


def _sgu_kernel(x_ref, sc_ref, sh_ref, wi_ref, bi_ref, vg_ref, vb_ref, ws_ref, bs_ref, wo_ref,
                gate_ref, g_ref, b_ref, o_ref, gbuf):
    tm = x_ref.shape[1]
    h = (x_ref[0] * (1.0 + sc_ref[0]) + sh_ref[0]).astype(BF16)
    z = _dot(h, wi_ref[...]) + bi_ref[...]
    z = 0.5 * z * (1.0 + lax.erf(z * (2.0 ** -0.5)))
    u = z[:, :SGU_HALF]
    v = _ln(z[:, SGU_HALF:], vg_ref[...], vb_ref[...]).astype(BF16)
    gw = SGU_HALF // SGU_GROUPS
    ri = lax.broadcasted_iota(jnp.int32, (SGU_CHUNK, SGU_CHUNK), 0)
    ci = lax.broadcasted_iota(jnp.int32, (SGU_CHUNK, SGU_CHUNK), 1)
    causal = ci <= ri
    for g in range(SGU_GROUPS):
        wsg = jnp.where(causal, ws_ref[g], 0.0).astype(BF16)
        bias = bs_ref[:, g:g + 1]
        for c in range(tm // SGU_CHUNK):
            rows = slice(c * SGU_CHUNK, (c + 1) * SGU_CHUNK)
            cols = slice(g * gw, (g + 1) * gw)
            sv = _dot(wsg, v[rows, cols]) + bias
            gbuf[rows, cols] = (u[rows, cols] * sv).astype(BF16)
    yo = _dot(gbuf[...], wo_ref[...])
    o_ref[0] = _res_ln(x_ref[0], yo, gate_ref[0], g_ref[...], b_ref[...])


def _sgu_layer(x, sc, sh, gate, ln_g, ln_b, w_in, b_in, vg, vb, w_s, b_s, w_out, tm=256):
    bsz, s, d = x.shape
    return pl.pallas_call(
        _sgu_kernel,
        grid=(bsz, s // tm),
        in_specs=[_row_spec(tm, d), _mod_spec(), _mod_spec(),
                  _full_spec((d, 2 * SGU_HALF)), _full_spec((1, 2 * SGU_HALF)),
                  _full_spec((1, SGU_HALF)), _full_spec((1, SGU_HALF)),
                  _full_spec((SGU_GROUPS, SGU_CHUNK, SGU_CHUNK)), _full_spec((SGU_CHUNK, SGU_GROUPS)),
                  _full_spec((SGU_HALF, d)), _mod_spec(), _full_spec((1, d)), _full_spec((1, d))],
        out_specs=_row_spec(tm, d),
        out_shape=jax.ShapeDtypeStruct(x.shape, F32),
        scratch_shapes=[pltpu.VMEM((tm, SGU_HALF), BF16)],
        compiler_params=_cparams(("parallel", "parallel")),
        name="sgu",
    )(x, sc, sh, w_in.astype(BF16), b_in.reshape(1, -1), vg.reshape(1, -1), vb.reshape(1, -1),
      w_s, b_s.T, w_out.astype(BF16), gate, ln_g, ln_b)


def _gdn_in_kernel(x_ref, sc_ref, sh_ref, wq_ref, wz_ref, wba_ref, qkv_ref, z_ref, ba_ref):
    hf = x_ref[0] * (1.0 + sc_ref[0]) + sh_ref[0]
    h = hf.astype(BF16)
    qkv_ref[0] = _dot(h, wq_ref[...])
    z_ref[0] = _dot(h, wz_ref[...]).astype(BF16)
    ba_ref[0] = _dot_hi(hf, wba_ref[...])


def _gdn_chunk_kernel(q_ref, k_ref, v_ref, cwq_ref, cwk_ref, cwv_ref, ba_ref, alog_ref, dtb_ref,
                      z_ref, ng_ref, o_ref, tail, cbuf, state):
    c = q_ref.shape[1]
    hd = pl.program_id(1)

    @pl.when(pl.program_id(2) == 0)
    def _():
        tail[...] = jnp.zeros_like(tail)
        state[...] = jnp.zeros_like(state)

    off = GDN_TAIL - (GDN_CONV - 1)
    convd = []
    for idx, (ref, cw) in enumerate(((q_ref, cwq_ref), (k_ref, cwk_ref), (v_ref, cwv_ref))):
        cbuf[idx, 0:GDN_TAIL, :] = tail[idx]
        cbuf[idx, GDN_TAIL:, :] = ref[0]
        tail[idx] = ref[0, c - GDN_TAIL:c, :]
        acc = cw[0:1, :] * cbuf[idx, pl.ds(off, c), :]
        for t in range(1, GDN_CONV):
            acc = acc + cw[t:t + 1, :] * cbuf[idx, pl.ds(off + t, c), :]
        convd.append(_silu(acc))
    qc, kc, v = convd
    q = qc * lax.rsqrt(jnp.sum(qc * qc, -1, keepdims=True) + 1e-6) * (GDN_DK ** -0.5)
    k = kc * lax.rsqrt(jnp.sum(kc * kc, -1, keepdims=True) + 1e-6)

    lane = lax.broadcasted_iota(jnp.int32, (1, LANES), 1)
    sel = lane == hd
    ba = ba_ref[0]
    beta_all = _sigmoid(ba[:, :LANES])
    a_in = ba[:, LANES:] + dtb_ref[...]
    softplus = jnp.maximum(a_in, 0.0) + jnp.log1p(jnp.exp(-jnp.abs(a_in)))
    g_all = -jnp.exp(alog_ref[...]) * softplus
    beta = jnp.sum(jnp.where(sel, beta_all, 0.0), -1, keepdims=True)
    g = jnp.sum(jnp.where(sel, g_all, 0.0), -1, keepdims=True)

    ri = lax.broadcasted_iota(jnp.int32, (c, c), 0)
    ci = lax.broadcasted_iota(jnp.int32, (c, c), 1)
    causal = ci <= ri
    strict = ci < ri
    gc_b = _dot_hi(causal.astype(F32), jnp.broadcast_to(g, (c, LANES)))
    gc = gc_b[:, 0:1]
    gc_row = gc_b.T[0:1, :]
    dmat = jnp.where(causal, jnp.exp(jnp.where(causal, gc - gc_row, 0.0)), 0.0)

    kb = k.astype(BF16)
    a_mat = jnp.where(strict, beta * _dot_nt(kb, kb) * dmat, 0.0)
    xor = ri ^ ci
    eye = (ri == ci).astype(F32)
    inv = eye - jnp.where(xor == 1, a_mat, 0.0)
    lvl = 1
    while (1 << lvl) < c:
        m = jnp.where((xor >> lvl) == 1, a_mat, 0.0).astype(BF16)
        invb = inv.astype(BF16)
        inv = inv - _dot(_dot(invb, m).astype(BF16), invb)
        lvl += 1

    egc = jnp.exp(gc)
    rhs = jnp.concatenate([v * beta, k * (beta * egc)], -1).astype(BF16)
    sol = _dot(inv.astype(BF16), rhs)
    u = sol[:, :GDN_DV]
    w = sol[:, GDN_DV:]
    qb = q.astype(BF16)
    qk = jnp.where(causal, _dot_nt(qb, kb) * dmat, 0.0)
    s_prev = state[...]
    sb = s_prev.astype(BF16)
    v_new = u - _dot(w.astype(BF16), sb)
    vnb = v_new.astype(BF16)
    o = _dot((q * egc).astype(BF16), sb) + _dot(qk.astype(BF16), vnb)
    g_last = gc[c - 1:c, :]
    k_dec = (k * jnp.exp(g_last - gc)).astype(BF16)
    state[...] = s_prev * jnp.exp(g_last) + _dot_tn(k_dec, vnb)

    on = o * lax.rsqrt(jnp.mean(o * o, -1, keepdims=True) + 1e-6) * ng_ref[...]
    o_ref[0] = (on * _silu(z_ref[0].astype(F32))).astype(BF16)


def _gdn_layer(x, sc, sh, gate, ln_g, ln_b, w_in, conv_w, a_log, dt_bias, norm_g, w_out, tm=512):
    bsz, s, d = x.shape
    nh = GDN_HEADS
    nqkv = 3 * nh * GDN_DK
    w_qkv = w_in[:, :nqkv].astype(BF16)
    w_z = w_in[:, nqkv:nqkv + nh * GDN_DV].astype(BF16)
    w_b = w_in[:, nqkv + nh * GDN_DV:nqkv + nh * GDN_DV + nh]
    w_a = w_in[:, nqkv + nh * GDN_DV + nh:]
    pad = jnp.zeros((d, LANES - nh), F32)
    w_ba = jnp.concatenate([w_b, pad, w_a, pad], 1)
    qkv, z, ba = pl.pallas_call(
        _gdn_in_kernel,
        grid=(bsz, s // tm),
        in_specs=[_row_spec(tm, d), _mod_spec(), _mod_spec(), _full_spec((d, nqkv)),
                  _full_spec((d, nh * GDN_DV)), _full_spec((d, 2 * LANES))],
        out_specs=[_row_spec(tm, nqkv), _row_spec(tm, nh * GDN_DV), _row_spec(tm, 2 * LANES)],
        out_shape=[jax.ShapeDtypeStruct((bsz, s, nqkv), F32),
                   jax.ShapeDtypeStruct((bsz, s, nh * GDN_DV), BF16),
                   jax.ShapeDtypeStruct((bsz, s, 2 * LANES), F32)],
        compiler_params=_cparams(("parallel", "parallel")),
        name="gdn_in",
    )(x, sc, sh, w_qkv, w_z, w_ba)

    c = GDN_TILE
    lane_pad = jnp.zeros((LANES - nh,), F32)
    alog = jnp.concatenate([a_log, lane_pad]).reshape(1, LANES)
    dtb = jnp.concatenate([dt_bias, lane_pad]).reshape(1, LANES)

    def col(base):
        return pl.BlockSpec((1, c, LANES), lambda b, h, i: (b, i, base + h))

    def cw(base):
        return pl.BlockSpec((GDN_CONV, LANES), lambda b, h, i: (0, base + h))

    vec = pl.BlockSpec((1, LANES), lambda b, h, i: (0, 0))
    og = pl.pallas_call(
        _gdn_chunk_kernel,
        grid=(bsz, nh, s // c),
        in_specs=[col(0), col(nh), col(2 * nh), cw(0), cw(nh), cw(2 * nh),
                  pl.BlockSpec((1, c, 2 * LANES), lambda b, h, i: (b, i, 0)), vec, vec,
                  col(0), vec],
        out_specs=col(0),
        out_shape=jax.ShapeDtypeStruct((bsz, s, nh * GDN_DV), BF16),
        scratch_shapes=[pltpu.VMEM((3, GDN_TAIL, LANES), F32), pltpu.VMEM((3, c + GDN_TAIL, LANES), F32),
                        pltpu.VMEM((GDN_DK, GDN_DV), F32)],
        compiler_params=_cparams(("parallel", "parallel", "arbitrary")),
        name="gdn_chunk",
    )(qkv, qkv, qkv, conv_w, conv_w, conv_w, ba, alog, dtb, z, norm_g.reshape(1, LANES))
    return _outproj_ln(og, w_out.astype(BF16), x, gate, ln_g, ln_b)


def _mla_in_kernel(x_ref, sc_ref, sh_ref, pos_ref, wi_ref, qg_ref, kg_ref, wqn_ref, wqr_ref, wqt_ref,
                   wkv_ref, qn_ref, qr_ref, kn_ref, kr_ref, v_ref):
    h = (x_ref[0] * (1.0 + sc_ref[0]) + sh_ref[0]).astype(BF16)
    p = _dot(h, wi_ref[...])
    cq = p[:, :MLA_Q_RANK]
    ckv = p[:, MLA_Q_RANK:MLA_Q_RANK + MLA_KV_RANK]
    kr = p[:, MLA_Q_RANK + MLA_KV_RANK:MLA_Q_RANK + MLA_KV_RANK + LANES]
    krt = p[:, MLA_Q_RANK + MLA_KV_RANK + LANES:]
    cqn = (cq * lax.rsqrt(jnp.mean(cq * cq, -1, keepdims=True) + 1e-6) * qg_ref[...]).astype(BF16)
    ckn = (ckv * lax.rsqrt(jnp.mean(ckv * ckv, -1, keepdims=True) + 1e-6) * kg_ref[...]).astype(BF16)
    lane = lax.broadcasted_iota(jnp.int32, (1, LANES), 1)
    half = MLA_ROPE // 2
    fidx = (lane % half).astype(F32)
    inv_freq = jnp.exp(fidx * (-math.log(ROPE_THETA) / half))
    ang = pos_ref[0].astype(F32) * inv_freq
    live = lane < MLA_ROPE
    cos = jnp.where(live, jnp.cos(ang), 0.0)
    sin = jnp.where(live, jnp.sin(ang), 0.0)
    scale = (MLA_NOPE + MLA_ROPE) ** -0.5
    qn_ref[0] = (_dot(cqn, wqn_ref[...]) * scale).astype(BF16)
    qr = _dot(cqn, wqr_ref[...])
    qrt = _dot(cqn, wqt_ref[...])
    for hd in range(MLA_HEADS):
        cs = slice(hd * LANES, (hd + 1) * LANES)
        qr_ref[0, :, cs] = ((qr[:, cs] * cos + qrt[:, cs] * sin) * scale).astype(BF16)
    kr_ref[0] = (kr * cos + krt * sin).astype(BF16)
    kv = _dot(ckn, wkv_ref[...])
    nk = MLA_HEADS * MLA_NOPE
    kn_ref[0] = kv[:, :nk].astype(BF16)
    v_ref[0] = kv[:, nk:].astype(BF16)


def _mla_attn_kernel(qn_ref, qr_ref, kn_ref, kr_ref, v_ref, o_ref, m_sc, l_sc, acc_sc):
    tq = qn_ref.shape[1]
    tk = tq
    i = pl.program_id(2)
    q = jnp.concatenate([qn_ref[0], qr_ref[0]], -1)
    m_sc[...] = jnp.full_like(m_sc, -jnp.inf)
    l_sc[...] = jnp.zeros_like(l_sc)
    acc_sc[...] = jnp.zeros_like(acc_sc)

    def step(j, masked):
        r0 = pl.multiple_of(j * tk, tk)
        kt = jnp.concatenate([kn_ref[0, pl.ds(r0, tk), :], kr_ref[0, pl.ds(r0, tk), :]], -1)
        sc = _dot_nt(q, kt)
        if masked:
            ri = lax.broadcasted_iota(jnp.int32, (tq, tk), 0)
            ci = lax.broadcasted_iota(jnp.int32, (tq, tk), 1)
            sc = jnp.where(ci <= ri, sc, -jnp.inf)
        m_prev = m_sc[...]
        m_new = jnp.maximum(m_prev, jnp.max(sc, -1, keepdims=True))
        alpha = jnp.exp(m_prev - m_new)
        p = jnp.exp(sc - m_new)
        l_sc[...] = alpha * l_sc[...] + jnp.sum(p, -1, keepdims=True)
        acc_sc[...] = alpha * acc_sc[...] + _dot(p.astype(BF16), v_ref[0, pl.ds(r0, tk), :])
        m_sc[...] = m_new

    def body(j, carry):
        step(j, False)
        return carry

    lax.fori_loop(0, i, body, 0)
    step(i, True)
    o_ref[0] = (acc_sc[...] / l_sc[...]).astype(BF16)


def _mla_layer(x, positions, sc, sh, gate, ln_g, ln_b, w_in, q_norm_g, kv_norm_g, w_uq, w_ukv, w_out,
               tm=512, tq=512):
    bsz, s, d = x.shape
    nh = MLA_HEADS
    half = MLA_ROPE // 2
    rot = jnp.concatenate([jnp.arange(half, MLA_ROPE), jnp.arange(half)])
    sign = jnp.concatenate([-jnp.ones((half,), F32), jnp.ones((half,), F32)])
    zpad = jnp.zeros((d, LANES - MLA_ROPE), F32)
    w_kr = w_in[:, MLA_Q_RANK + MLA_KV_RANK:]
    w_in_ext = jnp.concatenate(
        [w_in[:, :MLA_Q_RANK + MLA_KV_RANK], w_kr, zpad, w_kr[:, rot] * sign, zpad], 1).astype(BF16)
    wq = w_uq.reshape(MLA_Q_RANK, nh, MLA_NOPE + MLA_ROPE)
    w_qn = wq[:, :, :MLA_NOPE].reshape(MLA_Q_RANK, nh * MLA_NOPE).astype(BF16)
    wq_r = wq[:, :, MLA_NOPE:]
    hpad = jnp.zeros((MLA_Q_RANK, nh, LANES - MLA_ROPE), F32)
    w_qr = jnp.concatenate([wq_r, hpad], -1).reshape(MLA_Q_RANK, nh * LANES).astype(BF16)
    w_qt = jnp.concatenate([wq_r[:, :, rot] * sign, hpad], -1).reshape(MLA_Q_RANK, nh * LANES).astype(BF16)
    wkv = w_ukv.reshape(MLA_KV_RANK, nh, MLA_NOPE + MLA_V)
    w_kv = jnp.concatenate([wkv[:, :, :MLA_NOPE].reshape(MLA_KV_RANK, nh * MLA_NOPE),
                            wkv[:, :, MLA_NOPE:].reshape(MLA_KV_RANK, nh * MLA_V)], 1).astype(BF16)
    n_in = w_in_ext.shape[1]
    wide = nh * LANES
    qn, qr, kn, kr, v = pl.pallas_call(
        _mla_in_kernel,
        grid=(bsz, s // tm),
        in_specs=[_row_spec(tm, d), _mod_spec(), _mod_spec(), _row_spec(tm, 1),
                  _full_spec((d, n_in)), _full_spec((1, MLA_Q_RANK)), _full_spec((1, MLA_KV_RANK)),
                  _full_spec((MLA_Q_RANK, wide)), _full_spec((MLA_Q_RANK, wide)),
                  _full_spec((MLA_Q_RANK, wide)), _full_spec((MLA_KV_RANK, 2 * wide))],
        out_specs=[_row_spec(tm, wide), _row_spec(tm, wide), _row_spec(tm, wide), _row_spec(tm, LANES),
                   _row_spec(tm, wide)],
        out_shape=[jax.ShapeDtypeStruct((bsz, s, wide), BF16), jax.ShapeDtypeStruct((bsz, s, wide), BF16),
                   jax.ShapeDtypeStruct((bsz, s, wide), BF16), jax.ShapeDtypeStruct((bsz, s, LANES), BF16),
                   jax.ShapeDtypeStruct((bsz, s, wide), BF16)],
        compiler_params=_cparams(("parallel", "parallel")),
        name="mla_in",
    )(x, sc, sh, positions.reshape(bsz, s, 1), w_in_ext, q_norm_g.reshape(1, -1), kv_norm_g.reshape(1, -1),
      w_qn, w_qr, w_qt, w_kv)

    qspec = pl.BlockSpec((1, tq, LANES), lambda b, h, i: (b, i, h))
    kspec = pl.BlockSpec((1, s, LANES), lambda b, h, i: (b, 0, h))
    o = pl.pallas_call(
        _mla_attn_kernel,
        grid=(bsz, nh, s // tq),
        in_specs=[qspec, qspec, kspec, pl.BlockSpec((1, s, LANES), lambda b, h, i: (b, 0, 0)), kspec],
        out_specs=qspec,
        out_shape=jax.ShapeDtypeStruct((bsz, s, wide), BF16),
        scratch_shapes=[pltpu.VMEM((tq, 1), F32), pltpu.VMEM((tq, 1), F32), pltpu.VMEM((tq, MLA_V), F32)],
        compiler_params=_cparams(("parallel", "parallel", "arbitrary")),
        name="mla_attn",
    )(qn, qr, kn, kr, v)
    return _outproj_ln(o, w_out.astype(BF16), x, gate, ln_g, ln_b)


def _router_kernel(x_ref, sc_ref, sh_ref, wr_ref, meta_ref, cnt_ref, run):
    tm = x_ref.shape[1]

    @pl.when(pl.program_id(0) == 0)
    def _():
        run[...] = jnp.zeros_like(run)

    h = x_ref[0] * (1.0 + sc_ref[0]) + sh_ref[0]
    lane = lax.broadcasted_iota(jnp.int32, (tm, LANES), 1)
    lane_f = lane.astype(F32)
    logits = jnp.where(lane < N_EXPERTS, _dot_hi(h, wr_ref[...]), -jnp.inf)
    m1 = jnp.max(logits, -1, keepdims=True)
    i1 = jnp.min(jnp.where(logits == m1, lane_f, float(LANES)), -1, keepdims=True)
    oh1 = lane_f == i1
    rest = jnp.where(oh1, -jnp.inf, logits)
    m2 = jnp.max(rest, -1, keepdims=True)
    i2 = jnp.min(jnp.where(rest == m2, lane_f, float(LANES)), -1, keepdims=True)
    oh2 = lane_f == i2
    e21 = jnp.exp(m2 - m1)
    g1 = 1.0 / (1.0 + e21)
    g2 = e21 / (1.0 + e21)
    cnt = oh1.astype(F32) + oh2.astype(F32)
    ri = lax.broadcasted_iota(jnp.int32, (tm, tm), 0)
    ci = lax.broadcasted_iota(jnp.int32, (tm, tm), 1)
    before = _dot((ci < ri).astype(BF16), cnt.astype(BF16)) + run[...]
    r1 = jnp.sum(jnp.where(oh1, before, 0.0), -1, keepdims=True)
    r2 = jnp.sum(jnp.where(oh2, before, 0.0), -1, keepdims=True)
    meta = jnp.zeros((tm, LANES), F32)
    for k, col in enumerate((i1, i2, g1, g2, r1, r2)):
        meta = jnp.where(lane == k, col, meta)
    meta_ref[...] = meta
    run[...] = run[...] + jnp.sum(cnt, 0, keepdims=True)
    cnt_ref[...] = run[...]


def _dispatch_kernel(tbl_sm, x_ref, sc_ref, sh_ref, dest_hbm, xb_hbm, hbuf, zbuf, idx_sm, sem, isem, *,
                     first_tail_block):
    tm = x_ref.shape[1]
    i = pl.program_id(0)
    icp = pltpu.make_async_copy(dest_hbm.at[i], idx_sm, isem)
    icp.start()
    hbuf[...] = x_ref[0] * (1.0 + sc_ref[0]) + sh_ref[0]
    icp.wait()

    def issue(r, carry):
        pltpu.make_async_copy(hbuf.at[pl.ds(r, 1)], xb_hbm.at[pl.ds(idx_sm[2 * r], 1)], sem).start()
        pltpu.make_async_copy(hbuf.at[pl.ds(r, 1)], xb_hbm.at[pl.ds(idx_sm[2 * r + 1], 1)], sem).start()
        return carry

    lax.fori_loop(0, tm, issue, 0)
    for _ in range(TOP_K):
        pltpu.make_async_copy(hbuf, xb_hbm.at[pl.ds(0, tm)], sem).wait()

    @pl.when(i == pl.num_programs(0) - 1)
    def _():
        zbuf[...] = jnp.zeros_like(zbuf)
        n_blocks = xb_hbm.shape[0] // MOE_BLOCK

        def pad_copies(e):
            start = tbl_sm[e]
            end = tbl_sm[N_EXPERTS + e]
            n1 = (-start) & (SUBLANES - 1)
            a0 = start + n1
            l8 = end - a0
            out = []
            for r in range(SUBLANES - 1):
                out.append((r < n1, pltpu.make_async_copy(zbuf.at[pl.ds(r, 1)], xb_hbm.at[pl.ds(start + r, 1)], sem)))
            sz = MOE_BLOCK // 2
            while sz >= SUBLANES:
                off = pl.multiple_of(a0 + (l8 & ~(2 * sz - 1)), SUBLANES)
                out.append(((l8 & sz) != 0,
                            pltpu.make_async_copy(zbuf.at[pl.ds(0, sz)], xb_hbm.at[pl.ds(off, sz)], sem)))
                sz //= 2
            return out

        def tail_copies():
            nvalid = tbl_sm[2 * N_EXPERTS]
            return [(bi >= nvalid,
                     pltpu.make_async_copy(zbuf, xb_hbm.at[pl.ds(bi * MOE_BLOCK, MOE_BLOCK)], sem))
                    for bi in range(first_tail_block, n_blocks)]

        def start_all(e, carry):
            for pred, cp in pad_copies(e):
                pl.when(pred)(cp.start)
            return carry

        def wait_all(e, carry):
            for pred, cp in pad_copies(e):
                pl.when(pred)(cp.wait)
            return carry

        lax.fori_loop(0, N_EXPERTS, start_all, 0)
        for pred, cp in tail_copies():
            pl.when(pred)(cp.start)
        lax.fori_loop(0, N_EXPERTS, wait_all, 0)
        for pred, cp in tail_copies():
            pl.when(pred)(cp.wait)


def _expert_kernel(be_sm, nv_sm, x_ref, wa_ref, wb_ref, wo_ref, o_ref, hbuf, acc):
    i = pl.program_id(0)
    j = pl.program_id(1)

    @pl.when((i >= nv_sm[0]) & (j == pl.num_programs(1) - 1))
    def _():
        o_ref[...] = jnp.zeros_like(o_ref)

    @pl.when(i < nv_sm[0])
    def _():
        @pl.when(j == 0)
        def _():
            hbuf[...] = x_ref[...].astype(BF16)
            acc[...] = jnp.zeros_like(acc)

        h = hbuf[...]
        a = _dot(h, wa_ref[0])
        b = _dot(h, wb_ref[0])
        acc[...] += _dot((_silu(a) * b).astype(BF16), wo_ref[0])

        @pl.when(j == pl.num_programs(1) - 1)
        def _():
            o_ref[...] = acc[...]


def _combine_kernel(x_ref, gate_ref, meta_ref, g_ref, b_ref, dest_hbm, yb_hbm, o_ref,
                    y0, y1, idx_sm, sem, isem):
    tm = x_ref.shape[1]
    i = pl.program_id(0)
    icp = pltpu.make_async_copy(dest_hbm.at[i], idx_sm, isem)
    icp.start()
    icp.wait()

    def issue(r, carry):
        pltpu.make_async_copy(yb_hbm.at[pl.ds(idx_sm[2 * r], 1)], y0.at[pl.ds(r, 1)], sem).start()
        pltpu.make_async_copy(yb_hbm.at[pl.ds(idx_sm[2 * r + 1], 1)], y1.at[pl.ds(r, 1)], sem).start()
        return carry

    lax.fori_loop(0, tm, issue, 0)
    for buf in (y0, y1):
        pltpu.make_async_copy(yb_hbm.at[pl.ds(0, tm)], buf, sem).wait()
    meta = meta_ref[...]
    y = meta[:, 2:3] * y0[...] + meta[:, 3:4] * y1[...]
    o_ref[0] = _res_ln(x_ref[0], y, gate_ref[0], g_ref[...], b_ref[...])


def _moe_layer(x, sc, sh, gate, ln_g, ln_b, router, w_in, w_out, tm=256, th=512):
    bsz, s, d = x.shape
    n = bsz * s
    nt = n // tm
    spt = s // tm
    hid = w_out.shape[1]
    nj = hid // th
    min_blocks = n * TOP_K // MOE_BLOCK
    nb = min_blocks + N_EXPERTS

    row1 = pl.BlockSpec((1, tm, d), lambda i: (i // spt, i % spt, 0))
    mod1 = pl.BlockSpec((1, 1, d), lambda i: (i // spt, 0, 0))
    slab1 = pl.BlockSpec((tm, LANES), lambda i: (i, 0))
    w_r = jnp.concatenate([router, jnp.zeros((d, LANES - N_EXPERTS), F32)], 1)
    meta, counts = pl.pallas_call(
        _router_kernel,
        grid=(nt,),
        in_specs=[row1, mod1, mod1, pl.BlockSpec((d, LANES), lambda i: (0, 0))],
        out_specs=[slab1, pl.BlockSpec((1, LANES), lambda i: (0, 0))],
        out_shape=[jax.ShapeDtypeStruct((n, LANES), F32), jax.ShapeDtypeStruct((1, LANES), F32)],
        scratch_shapes=[pltpu.VMEM((1, LANES), F32)],
        compiler_params=_cparams(("arbitrary",)),
        name="moe_router",
    )(x, sc, sh, w_r)

    cnt = counts[0, :N_EXPERTS].astype(jnp.int32)
    nblk = (cnt + MOE_BLOCK - 1) // MOE_BLOCK
    ends = jnp.cumsum(nblk)
    first_row = (ends - nblk) * MOE_BLOCK
    nvalid = ends[-1]
    eid = meta[:, :TOP_K].astype(jnp.int32)
    rank = meta[:, 4:4 + TOP_K].astype(jnp.int32)
    base = jnp.sum(jnp.where(eid[:, :, None] == jnp.arange(N_EXPERTS), first_row, 0), -1)
    dest = (base + rank).reshape(nt, TOP_K * tm)
    tbl = jnp.concatenate([first_row + cnt, ends * MOE_BLOCK, nvalid.reshape(1)]).astype(jnp.int32)
    bi = jnp.minimum(jnp.arange(nb, dtype=jnp.int32), nvalid - 1)
    blk_e = jnp.sum(bi[:, None] >= ends[None, :], -1).astype(jnp.int32)

    row1p = pl.BlockSpec((1, tm, d), lambda i, t: (i // spt, i % spt, 0))
    mod1p = pl.BlockSpec((1, 1, d), lambda i, t: (i // spt, 0, 0))
    xb = pl.pallas_call(
        functools.partial(_dispatch_kernel, first_tail_block=min_blocks),
        grid_spec=pltpu.PrefetchScalarGridSpec(
            num_scalar_prefetch=1,
            grid=(nt,),
            in_specs=[row1p, mod1p, mod1p, pl.BlockSpec(memory_space=pl.ANY)],
            out_specs=pl.BlockSpec(memory_space=pl.ANY),
            scratch_shapes=[pltpu.VMEM((tm, d), F32), pltpu.VMEM((MOE_BLOCK, d), F32),
                            pltpu.SMEM((TOP_K * tm,), jnp.int32),
                            pltpu.SemaphoreType.DMA, pltpu.SemaphoreType.DMA]),
        out_shape=jax.ShapeDtypeStruct((nb * MOE_BLOCK, d), F32),
        compiler_params=_cparams(("arbitrary",)),
        name="moe_dispatch",
    )(tbl, x, sc, sh, dest)

    last = nj - 1

    def jsel(i, j, nv):
        return jnp.where(i < nv[0], j, last)

    w_in_b = w_in.astype(BF16)
    yb = pl.pallas_call(
        _expert_kernel,
        grid_spec=pltpu.PrefetchScalarGridSpec(
            num_scalar_prefetch=2,
            grid=(nb, nj),
            in_specs=[pl.BlockSpec((MOE_BLOCK, d), lambda i, j, be, nv: (i, 0)),
                      pl.BlockSpec((1, d, th), lambda i, j, be, nv: (be[i], 0, jsel(i, j, nv))),
                      pl.BlockSpec((1, d, th), lambda i, j, be, nv: (be[i], 0, jsel(i, j, nv) + nj)),
                      pl.BlockSpec((1, th, d), lambda i, j, be, nv: (be[i], jsel(i, j, nv), 0))],
            out_specs=pl.BlockSpec((MOE_BLOCK, d), lambda i, j, be, nv: (i, 0)),
            scratch_shapes=[pltpu.VMEM((MOE_BLOCK, d), BF16), pltpu.VMEM((MOE_BLOCK, d), F32)]),
        out_shape=jax.ShapeDtypeStruct((nb * MOE_BLOCK, d), F32),
        compiler_params=_cparams(("arbitrary", "arbitrary")),
        name="moe_experts",
    )(blk_e, nvalid.reshape(1).astype(jnp.int32), xb, w_in_b, w_in_b, w_out.astype(BF16))

    return pl.pallas_call(
        _combine_kernel,
        grid=(nt,),
        in_specs=[row1, mod1, slab1, pl.BlockSpec((1, d), lambda i: (0, 0)), pl.BlockSpec((1, d), lambda i: (0, 0)),
                  pl.BlockSpec(memory_space=pl.ANY), pl.BlockSpec(memory_space=pl.ANY)],
        out_specs=row1,
        out_shape=jax.ShapeDtypeStruct(x.shape, F32),
        scratch_shapes=[pltpu.VMEM((tm, d), F32), pltpu.VMEM((tm, d), F32),
                        pltpu.SMEM((TOP_K * tm,), jnp.int32),
                        pltpu.SemaphoreType.DMA, pltpu.SemaphoreType.DMA],
        compiler_params=_cparams(("arbitrary",)),
        name="moe_combine",
    )(x, gate, meta, ln_g, ln_b, dest, yb)


def kernel(x, c, positions, ada_w, ada_b, ln_g, ln_b, conv_w_in, conv_dw, conv_dw_b, conv_ln_g, conv_ln_b, conv_w_out, sgu_w_in, sgu_b_in, sgu_ln_g, sgu_ln_b, sgu_w_s, sgu_b_s, sgu_w_out, gdn_w_in, gdn_conv, gdn_a_log, gdn_dt_bias, gdn_norm_g, gdn_w_out, mla_w_in, mla_q_norm_g, mla_kv_norm_g, mla_w_uq, mla_w_ukv, mla_w_out, ffn_w_in, ffn_w_out, moe_router, moe_w_in, moe_w_out):
    bsz, s, d = x.shape
    mod = _ada_mod(c, ada_w, ada_b).reshape(DEPTH, bsz, 6, 1, d)
    for i in range(DEPTH):
        sh1, sc1, g1, sh2, sc2, g2 = (mod[i, :, t] for t in range(6))
        lg1, lb1 = ln_g[i, 0].reshape(1, d), ln_b[i, 0].reshape(1, d)
        lg2, lb2 = ln_g[i, 1].reshape(1, d), ln_b[i, 1].reshape(1, d)
        j = i // 4
        mixer = i % 4
        if mixer == 0:
            x = _conformer_layer(x, sc1, sh1, g1, lg1, lb1, conv_w_in[j], conv_dw[j], conv_dw_b[j],
                                 conv_ln_g[j], conv_ln_b[j], conv_w_out[j])
        elif mixer == 1:
            x = _sgu_layer(x, sc1, sh1, g1, lg1, lb1, sgu_w_in[j], sgu_b_in[j], sgu_ln_g[j], sgu_ln_b[j],
                           sgu_w_s[j], sgu_b_s[j], sgu_w_out[j])
        elif mixer == 2:
            x = _gdn_layer(x, sc1, sh1, g1, lg1, lb1, gdn_w_in[j], gdn_conv[j], gdn_a_log[j], gdn_dt_bias[j],
                           gdn_norm_g[j], gdn_w_out[j])
        else:
            x = _mla_layer(x, positions, sc1, sh1, g1, lg1, lb1, mla_w_in[j], mla_q_norm_g[j],
                           mla_kv_norm_g[j], mla_w_uq[j], mla_w_ukv[j], mla_w_out[j])
        if i % 2 == 0:
            x = _swiglu_layer(x, sc2, sh2, g2, lg2, lb2, ffn_w_in[i // 2], ffn_w_out[i // 2])
        else:
            x = _moe_layer(x, sc2, sh2, g2, lg2, lb2, moe_router[i // 2], moe_w_in[i // 2], moe_w_out[i // 2])
    return x
```

```python
import functools
import math

import jax
import jax.numpy as jnp
from jax import lax
from jax.experimental import pallas as pl
from jax.experimental.pallas import tpu as pltpu

F32 = jnp.float32
BF16 = jnp.bfloat16
HI = lax.Precision.HIGHEST

D_MODEL = 1024
DEPTH = 4
ALPHA = (2 * DEPTH) ** 0.25

CONV_WIDTH = 31
CONV_HALO = 32
CONV_ROWS = 128
SGU_CHUNK = 128
SGU_GROUPS = 8
SGU_HALF = 2 * D_MODEL
GDN_HEADS = 8
GDN_DK = 128
GDN_DV = 128
GDN_CONV = 4
GDN_TILE = 256
GDN_TAIL = 8
MLA_HEADS = 8
MLA_Q_RANK = 512
MLA_KV_RANK = 256
MLA_NOPE = 128
MLA_ROPE = 64
MLA_V = 128
ROPE_THETA = 10000.0
FFN_HIDDEN = 7 * D_MODEL // 2
N_EXPERTS = 8
TOP_K = 2
LANES = 128
SUBLANES = 8
MOE_BLOCK = 1024
VMEM_LIMIT = 56 * 1024 * 1024


def _cparams(sem):
    return pltpu.CompilerParams(dimension_semantics=sem, vmem_limit_bytes=VMEM_LIMIT)


def _sigmoid(x):
    return 1.0 / (1.0 + jnp.exp(-x))


def _silu(x):
    return x * _sigmoid(x)


def _ln(z, g, b, eps=1e-5):
    mu = jnp.mean(z, -1, keepdims=True)
    zc = z - mu
    var = jnp.mean(zc * zc, -1, keepdims=True)
    return zc * lax.rsqrt(var + eps) * g + b


def _res_ln(x, y, gate, g, b):
    return _ln(ALPHA * x + (1.0 + gate) * y, g, b)


def _dot(a, b):
    return jnp.dot(a, b, preferred_element_type=F32)


def _dot_nt(a, b):
    return lax.dot_general(a, b, (((1,), (1,)), ((), ())), preferred_element_type=F32)


def _dot_tn(a, b):
    return lax.dot_general(a, b, (((0,), (0,)), ((), ())), preferred_element_type=F32)


def _dot_hi(a, b):
    return jnp.dot(a, b, precision=HI, preferred_element_type=F32)


def _ada_kernel(c_ref, w_ref, b_ref, o_ref):
    cond = _silu(c_ref[...])
    o_ref[0] = _dot_hi(cond, w_ref[0]) + b_ref[0]


def _ada_mod(c, ada_w, ada_b):
    bsz, d = c.shape
    n_out = ada_w.shape[-1]
    tn = 1024
    return pl.pallas_call(
        _ada_kernel,
        grid=(DEPTH, n_out // tn),
        in_specs=[
            pl.BlockSpec((bsz, d), lambda i, j: (0, 0)),
            pl.BlockSpec((1, d, tn), lambda i, j: (i, 0, j)),
            pl.BlockSpec((1, 1, tn), lambda i, j: (i, 0, j)),
        ],
        out_specs=pl.BlockSpec((1, bsz, tn), lambda i, j: (i, 0, j)),
        out_shape=jax.ShapeDtypeStruct((DEPTH, bsz, n_out), F32),
        compiler_params=_cparams(("parallel", "parallel")),
        name="ada_mod",
    )(c, ada_w, ada_b.reshape(DEPTH, 1, n_out))


def _row_spec(tm, width):
    return pl.BlockSpec((1, tm, width), lambda b, i: (b, i, 0))


def _mod_spec(width=D_MODEL):
    return pl.BlockSpec((1, 1, width), lambda b, i: (b, 0, 0))


def _full_spec(shape):
    nd = len(shape)
    return pl.BlockSpec(shape, lambda b, i: (0,) * nd)


def _outproj_kernel(a_ref, w_ref, x_ref, gate_ref, g_ref, b_ref, o_ref):
    y = _dot(a_ref[0], w_ref[...])
    o_ref[0] = _res_ln(x_ref[0], y, gate_ref[0], g_ref[...], b_ref[...])


def _outproj_ln(a, w, x, gate, ln_g, ln_b, tm=512):
    bsz, s, k = a.shape
    d = x.shape[-1]
    return pl.pallas_call(
        _outproj_kernel,
        grid=(bsz, s // tm),
        in_specs=[_row_spec(tm, k), _full_spec((k, d)), _row_spec(tm, d), _mod_spec(d),
                  _full_spec((1, d)), _full_spec((1, d))],
        out_specs=_row_spec(tm, d),
        out_shape=jax.ShapeDtypeStruct(x.shape, F32),
        compiler_params=_cparams(("parallel", "parallel")),
        name="outproj_ln",
    )(a, w, x, gate, ln_g, ln_b)


def _conv_in_kernel(x_ref, sc_ref, sh_ref, w_ref, o_ref):
    h = (x_ref[0] * (1.0 + sc_ref[0]) + sh_ref[0]).astype(BF16)
    ag = _dot(h, w_ref[...])
    d = o_ref.shape[-1]
    o_ref[0] = ag[:, :d] * _sigmoid(ag[:, d:])


def _conv_mid_kernel(y_ref, halo_ref, dw_ref, dwb_ref, cg_ref, cb_ref, w_ref,
                     x_ref, gate_ref, g_ref, b_ref, o_ref, ybuf, cbuf, shbuf):
    tm = y_ref.shape[1]
    d = y_ref.shape[2]
    first = pl.program_id(1) == 0
    ybuf[0:CONV_HALO, :] = jnp.where(first, 0.0, halo_ref[0])
    ybuf[CONV_HALO:, :] = y_ref[0]
    rc = CONV_ROWS
    off = CONV_HALO - (CONV_WIDTH - 1)
    for c0 in range(0, d, LANES):
        taps = [dw_ref[k:k + 1, c0:c0 + LANES] for k in range(CONV_WIDTH)]
        for r0 in range(0, tm, rc):
            acc = None
            for r in range(SUBLANES):
                ks = [k for k in range(CONV_WIDTH) if (off + k) % SUBLANES == r]
                span = max((off + k) // SUBLANES for k in ks) * SUBLANES + rc
                shbuf[r, 0:span, :] = ybuf[r0 + r:r0 + r + span, c0:c0 + LANES]
                for k in ks:
                    q8 = (off + k) // SUBLANES * SUBLANES
                    term = taps[k] * shbuf[r, q8:q8 + rc, :]
                    acc = term if acc is None else acc + term
            cbuf[r0:r0 + rc, c0:c0 + LANES] = acc
    yc = cbuf[...] + dwb_ref[...]
    yn = _silu(_ln(yc, cg_ref[...], cb_ref[...]))
    yo = _dot(yn.astype(BF16), w_ref[...])
    o_ref[0] = _res_ln(x_ref[0], yo, gate_ref[0], g_ref[...], b_ref[...])


def _conformer_layer(x, sc, sh, gate, ln_g, ln_b, w_in, dw, dw_b, cg, cb, w_out, tm=512):
    bsz, s, d = x.shape
    y = pl.pallas_call(
        _conv_in_kernel,
        grid=(bsz, s // tm),
        in_specs=[_row_spec(tm, d), _mod_spec(), _mod_spec(), _full_spec((d, 2 * d))],
        out_specs=_row_spec(tm, d),
        out_shape=jax.ShapeDtypeStruct((bsz, s, d), F32),
        compiler_params=_cparams(("parallel", "parallel")),
        name="conv_in",
    )(x, sc, sh, w_in.astype(BF16))
    hb = tm // CONV_HALO
    halo_spec = pl.BlockSpec((1, CONV_HALO, d), lambda b, i: (b, jnp.maximum(i * hb - 1, 0), 0))
    dw_pad = jnp.concatenate([dw, jnp.zeros((1, d), F32)], 0)
    return pl.pallas_call(
        _conv_mid_kernel,
        grid=(bsz, s // tm),
        in_specs=[_row_spec(tm, d), halo_spec, _full_spec((CONV_WIDTH + 1, d)), _full_spec((1, d)),
                  _full_spec((1, d)), _full_spec((1, d)), _full_spec((d, d)),
                  _row_spec(tm, d), _mod_spec(), _full_spec((1, d)), _full_spec((1, d))],
        out_specs=_row_spec(tm, d),
        out_shape=jax.ShapeDtypeStruct((bsz, s, d), F32),
        scratch_shapes=[pltpu.VMEM((tm + CONV_HALO, d), F32), pltpu.VMEM((tm, d), F32),
                        pltpu.VMEM((SUBLANES, CONV_ROWS + CONV_HALO, LANES), F32)],
        compiler_params=_cparams(("parallel", "parallel")),
        name="conv_mid",
    )(y, y, dw_pad, dw_b.reshape(1, d), cg.reshape(1, d), cb.reshape(1, d), w_out.astype(BF16),
      x, gate, ln_g, ln_b)


def _swiglu_kernel(x_ref, sc_ref, sh_ref, wa_ref, wb_ref, wo_ref, gate_ref, g_ref, b_ref,
                   o_ref, hbuf, acc):
    j = pl.program_id(2)

    @pl.when(j == 0)
    def _():
        hbuf[...] = (x_ref[0] * (1.0 + sc_ref[0]) + sh_ref[0]).astype(BF16)
        acc[...] = jnp.zeros_like(acc)

    h = hbuf[...]
    a = _dot(h, wa_ref[...])
    b = _dot(h, wb_ref[...])
    act = (_silu(a) * b).astype(BF16)
    acc[...] += _dot(act, wo_ref[...])

    @pl.when(j == pl.num_programs(2) - 1)
    def _():
        o_ref[0] = _res_ln(x_ref[0], acc[...], gate_ref[0], g_ref[...], b_ref[...])


def _swiglu_layer(x, sc, sh, gate, ln_g, ln_b, w_in, w_out, tm=1024, th=512):
    bsz, s, d = x.shape
    hid = w_out.shape[0]
    nj = hid // th
    row = pl.BlockSpec((1, tm, d), lambda b, i, j: (b, i, 0))
    mod = pl.BlockSpec((1, 1, d), lambda b, i, j: (b, 0, 0))
    vec = pl.BlockSpec((1, d), lambda b, i, j: (0, 0))
    w_in_b = w_in.astype(BF16)
    return pl.pallas_call(
        _swiglu_kernel,
        grid=(bsz, s // tm, nj),
        in_specs=[row, mod, mod,
                  pl.BlockSpec((d, th), lambda b, i, j: (0, j)),
                  pl.BlockSpec((d, th), lambda b, i, j: (0, j + nj)),
                  pl.BlockSpec((th, d), lambda b, i, j: (j, 0)),
                  mod, vec, vec],
        out_specs=row,
        out_shape=jax.ShapeDtypeStruct(x.shape, F32),
        scratch_shapes=[pltpu.VMEM((tm, d), BF16), pltpu.VMEM((tm, d), F32)],
        compiler_params=_cparams(("parallel", "parallel", "arbitrary")),
        name="swiglu",
    )(x, sc, sh, w_in_b, w_in_b, w_out.astype(BF16), gate, ln_g, ln_b)


def _sgu_kernel(x_ref, sc_ref, sh_ref, wi_ref, bi_ref, vg_ref, vb_ref, ws_ref, bs_ref, wo_ref,
                gate_ref, g_ref, b_ref, o_ref, gbuf):
    tm = x_ref.shape[1]
    h = (x_ref[0] * (1.0 + sc_ref[0]) + sh_ref[0]).astype(BF16)
    z = _dot(h, wi_ref[...]) + bi_ref[...]
    z = 0.5 * z * (1.0 + lax.erf(z * (2.0 ** -0.5)))
    u = z[:, :SGU_HALF]
    v = _ln(z[:, SGU_HALF:], vg_ref[...], vb_ref[...]).astype(BF16)
    gw = SGU_HALF // SGU_GROUPS
    ri = lax.broadcasted_iota(jnp.int32, (SGU_CHUNK, SGU_CHUNK), 0)
    ci = lax.broadcasted_iota(jnp.int32, (SGU_CHUNK, SGU_CHUNK), 1)
    causal = ci <= ri
    for g in range(SGU_GROUPS):
        wsg = jnp.where(causal, ws_ref[g], 0.0).astype(BF16)
        bias = bs_ref[:, g:g + 1]
        for c in range(tm // SGU_CHUNK):
            rows = slice(c * SGU_CHUNK, (c + 1) * SGU_CHUNK)
            cols = slice(g * gw, (g + 1) * gw)
            sv = _dot(wsg, v[rows, cols]) + bias
            gbuf[rows, cols] = (u[rows, cols] * sv).astype(BF16)
    yo = _dot(gbuf[...], wo_ref[...])
    o_ref[0] = _res_ln(x_ref[0], yo, gate_ref[0], g_ref[...], b_ref[...])


def _sgu_layer(x, sc, sh, gate, ln_g, ln_b, w_in, b_in, vg, vb, w_s, b_s, w_out, tm=256):
    bsz, s, d = x.shape
    return pl.pallas_call(
        _sgu_kernel,
        grid=(bsz, s // tm),
        in_specs=[_row_spec(tm, d), _mod_spec(), _mod_spec(),
                  _full_spec((d, 2 * SGU_HALF)), _full_spec((1, 2 * SGU_HALF)),
                  _full_spec((1, SGU_HALF)), _full_spec((1, SGU_HALF)),
                  _full_spec((SGU_GROUPS, SGU_CHUNK, SGU_CHUNK)), _full_spec((SGU_CHUNK, SGU_GROUPS)),
                  _full_spec((SGU_HALF, d)), _mod_spec(), _full_spec((1, d)), _full_spec((1, d))],
        out_specs=_row_spec(tm, d),
        out_shape=jax.ShapeDtypeStruct(x.shape, F32),
        scratch_shapes=[pltpu.VMEM((tm, SGU_HALF), BF16)],
        compiler_params=_cparams(("parallel", "parallel")),
        name="sgu",
    )(x, sc, sh, w_in.astype(BF16), b_in.reshape(1, -1), vg.reshape(1, -1), vb.reshape(1, -1),
      w_s, b_s.T, w_out.astype(BF16), gate, ln_g, ln_b)


def _gdn_in_kernel(x_ref, sc_ref, sh_ref, wq_ref, wz_ref, wba_ref, alog_ref, dtb_ref,
                   qkv_ref, z_ref, bg_ref, gct_ref):
    tm = x_ref.shape[1]
    c = GDN_TILE
    hf = x_ref[0] * (1.0 + sc_ref[0]) + sh_ref[0]
    h = hf.astype(BF16)
    qkv_ref[0] = _dot(h, wq_ref[...])
    z_ref[0] = _dot(h, wz_ref[...]).astype(BF16)
    ba = _dot_hi(hf, wba_ref[...])
    bg_ref[0, :, :LANES] = _sigmoid(ba[:, :LANES])
    a_in = ba[:, LANES:] + dtb_ref[...]
    softplus = jnp.maximum(a_in, 0.0) + jnp.log1p(jnp.exp(-jnp.abs(a_in)))
    g_all = -jnp.exp(alog_ref[...]) * softplus
    ri = lax.broadcasted_iota(jnp.int32, (c, c), 0)
    ci = lax.broadcasted_iota(jnp.int32, (c, c), 1)
    tril = (ci <= ri).astype(F32)
    for r0 in range(0, tm, c):
        gc = _dot_hi(tril, g_all[r0:r0 + c])
        bg_ref[0, r0:r0 + c, LANES:] = gc
        gct_ref[0, :, r0:r0 + c] = gc.T[:GDN_HEADS]


def _gdn_chunk_kernel(q_ref, k_ref, v_ref, cwq_ref, cwk_ref, cwv_ref, bg_ref, gct_ref,
                      z_ref, ng_ref, o_ref, tail, cbuf, state, *, hp):
    c = q_ref.shape[1]

    @pl.when(pl.program_id(2) == 0)
    def _():
        tail[...] = jnp.zeros_like(tail)
        state[...] = jnp.zeros_like(state)

    off = GDN_TAIL - (GDN_CONV - 1)
    convd = []
    for idx, (ref, cw) in enumerate(((q_ref, cwq_ref), (k_ref, cwk_ref), (v_ref, cwv_ref))):
        cbuf[idx, 0:GDN_TAIL, :] = tail[idx]
        cbuf[idx, GDN_TAIL:, :] = ref[0]
        tail[idx] = ref[0, c - GDN_TAIL:c, :]
        acc = cw[0:1, :] * cbuf[idx, pl.ds(off, c), :]
        for t in range(1, GDN_CONV):
            acc = acc + cw[t:t + 1, :] * cbuf[idx, pl.ds(off + t, c), :]
        convd.append(_silu(acc))
    qc_all, kc_all, v_all = convd

    lane = lax.broadcasted_iota(jnp.int32, (1, LANES), 1)
    sub = lax.broadcasted_iota(jnp.int32, (GDN_HEADS, 1), 0)
    ri = lax.broadcasted_iota(jnp.int32, (c, c), 0)
    ci = lax.broadcasted_iota(jnp.int32, (c, c), 1)
    causal = ci <= ri
    strict = ci < ri
    xor = ri ^ ci
    eye = (ri == ci).astype(F32)
    bg = bg_ref[0]
    gct = gct_ref[0]
    z_all = z_ref[0]
    heads = range(hp)
    q, k, v, kb, beta, gc, dmat, a_mat = ([None] * hp for _ in range(8))
    for t in heads:
        hd = pl.program_id(1) * hp + t
        cols = slice(t * LANES, (t + 1) * LANES)
        qc, kc, v[t] = qc_all[:, cols], kc_all[:, cols], v_all[:, cols]
        q[t] = qc * lax.rsqrt(jnp.sum(qc * qc, -1, keepdims=True) + 1e-6) * (GDN_DK ** -0.5)
        k[t] = kc * lax.rsqrt(jnp.sum(kc * kc, -1, keepdims=True) + 1e-6)
        sel = lane == hd
        beta[t] = jnp.sum(jnp.where(sel, bg[:, :LANES], 0.0), -1, keepdims=True)
        gc[t] = jnp.sum(jnp.where(sel, bg[:, LANES:], 0.0), -1, keepdims=True)
        gc_row = jnp.sum(jnp.where(sub == hd, gct, 0.0), 0, keepdims=True)
        dmat[t] = jnp.where(causal, jnp.exp(jnp.where(causal, gc[t] - gc_row, 0.0)), 0.0)
        kb[t] = k[t].astype(BF16)
    for t in heads:
        a_mat[t] = jnp.where(strict, beta[t] * _dot_nt(kb[t], kb[t]) * dmat[t], 0.0)
    inv = [eye - jnp.where(xor == 1, a_mat[t], 0.0) for t in heads]
    lvl = 1
    while (1 << lvl) < c:
        invb = [inv[t].astype(BF16) for t in heads]
        joins = (xor >> lvl) == 1
        dm = [_dot(invb[t], jnp.where(joins, a_mat[t], 0.0).astype(BF16)) for t in heads]
        inv = [inv[t] - _dot(dm[t].astype(BF16), invb[t]) for t in heads]
        lvl += 1

    egc = [jnp.exp(gc[t]) for t in heads]
    sol = [_dot(inv[t].astype(BF16),
                jnp.concatenate([v[t] * beta[t], k[t] * (beta[t] * egc[t])], -1).astype(BF16)) for t in heads]
    qk = [jnp.where(causal, _dot_nt(q[t].astype(BF16), kb[t]) * dmat[t], 0.0).astype(BF16) for t in heads]
    s_prev = [state[t] for t in heads]
    sb = [s_prev[t].astype(BF16) for t in heads]
    vnb = [(sol[t][:, :GDN_DV] - _dot(sol[t][:, GDN_DV:].astype(BF16), sb[t])).astype(BF16) for t in heads]
    o = [_dot((q[t] * egc[t]).astype(BF16), sb[t]) + _dot(qk[t], vnb[t]) for t in heads]
    for t in heads:
        g_last = gc[t][c - 1:c, :]
        k_dec = (k[t] * jnp.exp(g_last - gc[t])).astype(BF16)
        state[t] = s_prev[t] * jnp.exp(g_last) + _dot_tn(k_dec, vnb[t])
    for t in heads:
        cols = slice(t * LANES, (t + 1) * LANES)
        on = o[t] * lax.rsqrt(jnp.mean(o[t] * o[t], -1, keepdims=True) + 1e-6) * ng_ref[...]
        o_ref[0, :, cols] = (on * _silu(z_all[:, cols].astype(F32))).astype(BF16)


def _gdn_layer(x, sc, sh, gate, ln_g, ln_b, w_in, conv_w, a_log, dt_bias, norm_g, w_out, tm=512, hp=4):
    bsz, s, d = x.shape
    nh = GDN_HEADS
    nqkv = 3 * nh * GDN_DK
    w_qkv = w_in[:, :nqkv].astype(BF16)
    w_z = w_in[:, nqkv:nqkv + nh * GDN_DV].astype(BF16)
    w_b = w_in[:, nqkv + nh * GDN_DV:nqkv + nh * GDN_DV + nh]
    w_a = w_in[:, nqkv + nh * GDN_DV + nh:]
    pad = jnp.zeros((d, LANES - nh), F32)
    w_ba = jnp.concatenate([w_b, pad, w_a, pad], 1)
    lane_pad = jnp.zeros((LANES - nh,), F32)
    alog = jnp.concatenate([a_log, lane_pad]).reshape(1, LANES)
    dtb = jnp.concatenate([dt_bias, lane_pad]).reshape(1, LANES)
    qkv, z, bg, gct = pl.pallas_call(
        _gdn_in_kernel,
        grid=(bsz, s // tm),
        in_specs=[_row_spec(tm, d), _mod_spec(), _mod_spec(), _full_spec((d, nqkv)),
                  _full_spec((d, nh * GDN_DV)), _full_spec((d, 2 * LANES)),
                  _full_spec((1, LANES)), _full_spec((1, LANES))],
        out_specs=[_row_spec(tm, nqkv), _row_spec(tm, nh * GDN_DV), _row_spec(tm, 2 * LANES),
                   pl.BlockSpec((1, nh, tm), lambda b, i: (b, 0, i))],
        out_shape=[jax.ShapeDtypeStruct((bsz, s, nqkv), F32),
                   jax.ShapeDtypeStruct((bsz, s, nh * GDN_DV), BF16),
                   jax.ShapeDtypeStruct((bsz, s, 2 * LANES), F32),
                   jax.ShapeDtypeStruct((bsz, nh, s), F32)],
        compiler_params=_cparams(("parallel", "parallel")),
        name="gdn_in",
    )(x, sc, sh, w_qkv, w_z, w_ba, alog, dtb)

    c = GDN_TILE
    wide = hp * LANES
    ng = nh // hp

    def col(base):
        return pl.BlockSpec((1, c, wide), lambda b, h, i: (b, i, base + h))

    def cw(base):
        return pl.BlockSpec((GDN_CONV, wide), lambda b, h, i: (0, base + h))

    og = pl.pallas_call(
        functools.partial(_gdn_chunk_kernel, hp=hp),
        grid=(bsz, ng, s // c),
        in_specs=[col(0), col(ng), col(2 * ng), cw(0), cw(ng), cw(2 * ng),
                  pl.BlockSpec((1, c, 2 * LANES), lambda b, h, i: (b, i, 0)),
                  pl.BlockSpec((1, nh, c), lambda b, h, i: (b, 0, i)),
                  col(0), pl.BlockSpec((1, LANES), lambda b, h, i: (0, 0))],
        out_specs=col(0),
        out_shape=jax.ShapeDtypeStruct((bsz, s, nh * GDN_DV), BF16),
        scratch_shapes=[pltpu.VMEM((3, GDN_TAIL, wide), F32), pltpu.VMEM((3, c + GDN_TAIL, wide), F32),
                        pltpu.VMEM((hp, GDN_DK, GDN_DV), F32)],
        compiler_params=_cparams(("parallel", "parallel", "arbitrary")),
        name="gdn_chunk",
    )(qkv, qkv, qkv, conv_w, conv_w, conv_w, bg, gct, z, norm_g.reshape(1, LANES))
    return _outproj_ln(og, w_out.astype(BF16), x, gate, ln_g, ln_b)


def _mla_in_kernel(x_ref, sc_ref, sh_ref, pos_ref, wi_ref, qg_ref, kg_ref, wqn_ref, wqr_ref, wqt_ref,
                   wkv_ref, qn_ref, qr_ref, kn_ref, kr_ref, v_ref):
    h = (x_ref[0] * (1.0 + sc_ref[0]) + sh_ref[0]).astype(BF16)
    p = _dot(h, wi_ref[...])
    cq = p[:, :MLA_Q_RANK]
    ckv = p[:, MLA_Q_RANK:MLA_Q_RANK + MLA_KV_RANK]
    kr = p[:, MLA_Q_RANK + MLA_KV_RANK:MLA_Q_RANK + MLA_KV_RANK + LANES]
    krt = p[:, MLA_Q_RANK + MLA_KV_RANK + LANES:]
    cqn = (cq * lax.rsqrt(jnp.mean(cq * cq, -1, keepdims=True) + 1e-6) * qg_ref[...]).astype(BF16)
    ckn = (ckv * lax.rsqrt(jnp.mean(ckv * ckv, -1, keepdims=True) + 1e-6) * kg_ref[...]).astype(BF16)
    lane = lax.broadcasted_iota(jnp.int32, (1, LANES), 1)
    half = MLA_ROPE // 2
    fidx = (lane % half).astype(F32)
    inv_freq = jnp.exp(fidx * (-math.log(ROPE_THETA) / half))
    ang = pos_ref[0].astype(F32) * inv_freq
    live = lane < MLA_ROPE
    cos = jnp.where(live, jnp.cos(ang), 0.0)
    sin = jnp.where(live, jnp.sin(ang), 0.0)
    scale = (MLA_NOPE + MLA_ROPE) ** -0.5 * math.log2(math.e)
    qn_ref[0] = (_dot(cqn, wqn_ref[...]) * scale).astype(BF16)
    qr = _dot(cqn, wqr_ref[...])
    qrt = _dot(cqn, wqt_ref[...])
    for hd in range(MLA_HEADS):
        cs = slice(hd * LANES, (hd + 1) * LANES)
        qr_ref[0, :, cs] = ((qr[:, cs] * cos + qrt[:, cs] * sin) * scale).astype(BF16)
    kr_ref[0] = (kr * cos + krt * sin).astype(BF16)
    kv = _dot(ckn, wkv_ref[...])
    nk = MLA_HEADS * MLA_NOPE
    kn_ref[0] = kv[:, :nk].astype(BF16)
    v_ref[0] = kv[:, nk:].astype(BF16)


def _mla_attn_kernel(qn_ref, qr_ref, kn_ref, kr_ref, v_ref, o_ref, qbuf, m_sc, l_sc, acc_sc, *, hp):
    tq = qn_ref.shape[1]
    tk = tq
    i = pl.program_id(2)
    for h in range(hp):
        cols = slice(h * LANES, (h + 1) * LANES)
        qbuf[h, :, :LANES] = qn_ref[0, :, cols]
        qbuf[h, :, LANES:] = qr_ref[0, :, cols]
    m_sc[...] = jnp.full_like(m_sc, -jnp.inf)
    l_sc[...] = jnp.zeros_like(l_sc)
    acc_sc[...] = jnp.zeros_like(acc_sc)

    def step(j, masked):
        r0 = pl.multiple_of(j * tk, tk)
        krt = kr_ref[0, pl.ds(r0, tk), :]
        if masked:
            ri = lax.broadcasted_iota(jnp.int32, (tq, tk), 0)
            ci = lax.broadcasted_iota(jnp.int32, (tq, tk), 1)
            keep = ci <= ri
        scs, ps, alphas = [], [], []
        for h in range(hp):
            cols = slice(h * LANES, (h + 1) * LANES)
            kt = jnp.concatenate([kn_ref[0, pl.ds(r0, tk), cols], krt], -1)
            scs.append(_dot_nt(qbuf[h], kt))
        for h in range(hp):
            sc = jnp.where(keep, scs[h], -jnp.inf) if masked else scs[h]
            m_prev = m_sc[h]
            m_new = jnp.maximum(m_prev, jnp.max(sc, -1, keepdims=True))
            alpha = jnp.exp2(m_prev - m_new)
            p = jnp.exp2(sc - m_new)
            l_sc[h] = alpha * l_sc[h] + jnp.sum(p, -1, keepdims=True)
            m_sc[h] = m_new
            ps.append(p.astype(BF16))
            alphas.append(alpha)
        for h in range(hp):
            cols = slice(h * LANES, (h + 1) * LANES)
            acc_sc[h] = alphas[h] * acc_sc[h] + _dot(ps[h], v_ref[0, pl.ds(r0, tk), cols])

    def body(j, carry):
        step(j, False)
        return carry

    lax.fori_loop(0, i, body, 0)
    step(i, True)
    for h in range(hp):
        o_ref[0, :, h * LANES:(h + 1) * LANES] = (acc_sc[h] / l_sc[h]).astype(BF16)


def _mla_layer(x, positions, sc, sh, gate, ln_g, ln_b, w_in, q_norm_g, kv_norm_g, w_uq, w_ukv, w_out,
               tm=512, tq=512, hp=4):
    bsz, s, d = x.shape
    nh = MLA_HEADS
    half = MLA_ROPE // 2
    rot = jnp.concatenate([jnp.arange(half, MLA_ROPE), jnp.arange(half)])
    sign = jnp.concatenate([-jnp.ones((half,), F32), jnp.ones((half,), F32)])
    zpad = jnp.zeros((d, LANES - MLA_ROPE), F32)
    w_kr = w_in[:, MLA_Q_RANK + MLA_KV_RANK:]
    w_in_ext = jnp.concatenate(
        [w_in[:, :MLA_Q_RANK + MLA_KV_RANK], w_kr, zpad, w_kr[:, rot] * sign, zpad], 1).astype(BF16)
    wq = w_uq.reshape(MLA_Q_RANK, nh, MLA_NOPE + MLA_ROPE)
    w_qn = wq[:, :, :MLA_NOPE].reshape(MLA_Q_RANK, nh * MLA_NOPE).astype(BF16)
    wq_r = wq[:, :, MLA_NOPE:]
    hpad = jnp.zeros((MLA_Q_RANK, nh, LANES - MLA_ROPE), F32)
    w_qr = jnp.concatenate([wq_r, hpad], -1).reshape(MLA_Q_RANK, nh * LANES).astype(BF16)
    w_qt = jnp.concatenate([wq_r[:, :, rot] * sign, hpad], -1).reshape(MLA_Q_RANK, nh * LANES).astype(BF16)
    wkv = w_ukv.reshape(MLA_KV_RANK, nh, MLA_NOPE + MLA_V)
    w_kv = jnp.concatenate([wkv[:, :, :MLA_NOPE].reshape(MLA_KV_RANK, nh * MLA_NOPE),
                            wkv[:, :, MLA_NOPE:].reshape(MLA_KV_RANK, nh * MLA_V)], 1).astype(BF16)
    n_in = w_in_ext.shape[1]
    wide = nh * LANES
    qn, qr, kn, kr, v = pl.pallas_call(
        _mla_in_kernel,
        grid=(bsz, s // tm),
        in_specs=[_row_spec(tm, d), _mod_spec(), _mod_spec(), _row_spec(tm, 1),
                  _full_spec((d, n_in)), _full_spec((1, MLA_Q_RANK)), _full_spec((1, MLA_KV_RANK)),
                  _full_spec((MLA_Q_RANK, wide)), _full_spec((MLA_Q_RANK, wide)),
                  _full_spec((MLA_Q_RANK, wide)), _full_spec((MLA_KV_RANK, 2 * wide))],
        out_specs=[_row_spec(tm, wide), _row_spec(tm, wide), _row_spec(tm, wide), _row_spec(tm, LANES),
                   _row_spec(tm, wide)],
        out_shape=[jax.ShapeDtypeStruct((bsz, s, wide), BF16), jax.ShapeDtypeStruct((bsz, s, wide), BF16),
                   jax.ShapeDtypeStruct((bsz, s, wide), BF16), jax.ShapeDtypeStruct((bsz, s, LANES), BF16),
                   jax.ShapeDtypeStruct((bsz, s, wide), BF16)],
        compiler_params=_cparams(("parallel", "parallel")),
        name="mla_in",
    )(x, sc, sh, positions.reshape(bsz, s, 1), w_in_ext, q_norm_g.reshape(1, -1), kv_norm_g.reshape(1, -1),
      w_qn, w_qr, w_qt, w_kv)

    gw = hp * LANES
    qspec = pl.BlockSpec((1, tq, gw), lambda b, h, i: (b, i, h))
    kspec = pl.BlockSpec((1, s, gw), lambda b, h, i: (b, 0, h))
    o = pl.pallas_call(
        functools.partial(_mla_attn_kernel, hp=hp),
        grid=(bsz, nh // hp, s // tq),
        in_specs=[qspec, qspec, kspec, pl.BlockSpec((1, s, LANES), lambda b, h, i: (b, 0, 0)), kspec],
        out_specs=qspec,
        out_shape=jax.ShapeDtypeStruct((bsz, s, wide), BF16),
        scratch_shapes=[pltpu.VMEM((hp, tq, 2 * LANES), BF16), pltpu.VMEM((hp, tq, 1), F32),
                        pltpu.VMEM((hp, tq, 1), F32), pltpu.VMEM((hp, tq, MLA_V), F32)],
        compiler_params=_cparams(("parallel", "parallel", "arbitrary")),
        name="mla_attn",
    )(qn, qr, kn, kr, v)
    return _outproj_ln(o, w_out.astype(BF16), x, gate, ln_g, ln_b)


def _router_kernel(x_ref, sc_ref, sh_ref, wr_ref, meta_ref, cnt_ref, run):
    tm = x_ref.shape[1]

    @pl.when(pl.program_id(0) == 0)
    def _():
        run[...] = jnp.zeros_like(run)

    h = x_ref[0] * (1.0 + sc_ref[0]) + sh_ref[0]
    lane = lax.broadcasted_iota(jnp.int32, (tm, LANES), 1)
    lane_f = lane.astype(F32)
    logits = jnp.where(lane < N_EXPERTS, _dot_hi(h, wr_ref[...]), -jnp.inf)
    m1 = jnp.max(logits, -1, keepdims=True)
    i1 = jnp.min(jnp.where(logits == m1, lane_f, float(LANES)), -1, keepdims=True)
    oh1 = lane_f == i1
    rest = jnp.where(oh1, -jnp.inf, logits)
    m2 = jnp.max(rest, -1, keepdims=True)
    i2 = jnp.min(jnp.where(rest == m2, lane_f, float(LANES)), -1, keepdims=True)
    oh2 = lane_f == i2
    e21 = jnp.exp(m2 - m1)
    g1 = 1.0 / (1.0 + e21)
    g2 = e21 / (1.0 + e21)
    cnt = oh1.astype(F32) + oh2.astype(F32)
    ri = lax.broadcasted_iota(jnp.int32, (tm, tm), 0)
    ci = lax.broadcasted_iota(jnp.int32, (tm, tm), 1)
    before = _dot((ci < ri).astype(BF16), cnt.astype(BF16)) + run[...]
    r1 = jnp.sum(jnp.where(oh1, before, 0.0), -1, keepdims=True)
    r2 = jnp.sum(jnp.where(oh2, before, 0.0), -1, keepdims=True)
    meta = jnp.zeros((tm, LANES), F32)
    for k, col in enumerate((i1, i2, g1, g2, r1, r2)):
        meta = jnp.where(lane == k, col, meta)
    meta_ref[...] = meta
    run[...] = run[...] + jnp.sum(cnt, 0, keepdims=True)
    cnt_ref[...] = run[...]


def _dispatch_kernel(tbl_sm, x_ref, sc_ref, sh_ref, dest_hbm, xb_hbm, hbuf, zbuf, idx_sm, sem, isem, *,
                     first_tail_block):
    tm = x_ref.shape[1]
    i = pl.program_id(0)
    icp = pltpu.make_async_copy(dest_hbm.at[i], idx_sm, isem)
    icp.start()
    hbuf[...] = x_ref[0] * (1.0 + sc_ref[0]) + sh_ref[0]
    icp.wait()

    def issue(r, carry):
        pltpu.make_async_copy(hbuf.at[pl.ds(r, 1)], xb_hbm.at[pl.ds(idx_sm[2 * r], 1)], sem).start()
        pltpu.make_async_copy(hbuf.at[pl.ds(r, 1)], xb_hbm.at[pl.ds(idx_sm[2 * r + 1], 1)], sem).start()
        return carry

    lax.fori_loop(0, tm, issue, 0, unroll=8)
    for _ in range(TOP_K):
        pltpu.make_async_copy(hbuf, xb_hbm.at[pl.ds(0, tm)], sem).wait()

    @pl.when(i == pl.num_programs(0) - 1)
    def _():
        zbuf[...] = jnp.zeros_like(zbuf)
        n_blocks = xb_hbm.shape[0] // MOE_BLOCK

        def pad_copies(e):
            start = tbl_sm[e]
            end = tbl_sm[N_EXPERTS + e]
            n1 = (-start) & (SUBLANES - 1)
            a0 = start + n1
            l8 = end - a0
            out = []
            for r in range(SUBLANES - 1):
                out.append((r < n1, pltpu.make_async_copy(zbuf.at[pl.ds(r, 1)], xb_hbm.at[pl.ds(start + r, 1)], sem)))
            sz = MOE_BLOCK // 2
            while sz >= SUBLANES:
                off = pl.multiple_of(a0 + (l8 & ~(2 * sz - 1)), SUBLANES)
                out.append(((l8 & sz) != 0,
                            pltpu.make_async_copy(zbuf.at[pl.ds(0, sz)], xb_hbm.at[pl.ds(off, sz)], sem)))
                sz //= 2
            return out

        def tail_copies():
            nvalid = tbl_sm[2 * N_EXPERTS]
            return [(bi >= nvalid,
                     pltpu.make_async_copy(zbuf, xb_hbm.at[pl.ds(bi * MOE_BLOCK, MOE_BLOCK)], sem))
                    for bi in range(first_tail_block, n_blocks)]

        def start_all(e, carry):
            for pred, cp in pad_copies(e):
                pl.when(pred)(cp.start)
            return carry

        def wait_all(e, carry):
            for pred, cp in pad_copies(e):
                pl.when(pred)(cp.wait)
            return carry

        lax.fori_loop(0, N_EXPERTS, start_all, 0)
        for pred, cp in tail_copies():
            pl.when(pred)(cp.start)
        lax.fori_loop(0, N_EXPERTS, wait_all, 0)
        for pred, cp in tail_copies():
            pl.when(pred)(cp.wait)


def _expert_kernel(be_sm, nv_sm, x_ref, wa_ref, wb_ref, wo_ref, o_ref, hbuf, acc):
    i = pl.program_id(0)
    j = pl.program_id(1)

    @pl.when((i >= nv_sm[0]) & (j == pl.num_programs(1) - 1))
    def _():
        o_ref[...] = jnp.zeros_like(o_ref)

    @pl.when(i < nv_sm[0])
    def _():
        @pl.when(j == 0)
        def _():
            hbuf[...] = x_ref[...].astype(BF16)
            acc[...] = jnp.zeros_like(acc)

        h = hbuf[...]
        a = _dot(h, wa_ref[0])
        b = _dot(h, wb_ref[0])
        acc[...] += _dot((_silu(a) * b).astype(BF16), wo_ref[0])

        @pl.when(j == pl.num_programs(1) - 1)
        def _():
            o_ref[...] = acc[...]


def _combine_kernel(x_ref, gate_ref, meta_ref, g_ref, b_ref, dest_hbm, yb_hbm, o_ref,
                    y0, y1, idx_sm, sem, isem):
    tm = x_ref.shape[1]
    i = pl.program_id(0)
    icp = pltpu.make_async_copy(dest_hbm.at[i], idx_sm, isem)
    icp.start()
    icp.wait()

    def issue(r, carry):
        pltpu.make_async_copy(yb_hbm.at[pl.ds(idx_sm[2 * r], 1)], y0.at[pl.ds(r, 1)], sem).start()
        pltpu.make_async_copy(yb_hbm.at[pl.ds(idx_sm[2 * r + 1], 1)], y1.at[pl.ds(r, 1)], sem).start()
        return carry

    lax.fori_loop(0, tm, issue, 0, unroll=8)
    for buf in (y0, y1):
        pltpu.make_async_copy(yb_hbm.at[pl.ds(0, tm)], buf, sem).wait()
    meta = meta_ref[...]
    y = meta[:, 2:3] * y0[...] + meta[:, 3:4] * y1[...]
    o_ref[0] = _res_ln(x_ref[0], y, gate_ref[0], g_ref[...], b_ref[...])


def _moe_layer(x, sc, sh, gate, ln_g, ln_b, router, w_in, w_out, tm=256, th=512):
    bsz, s, d = x.shape
    n = bsz * s
    nt = n // tm
    spt = s // tm
    hid = w_out.shape[1]
    nj = hid // th
    min_blocks = n * TOP_K // MOE_BLOCK
    nb = min_blocks + N_EXPERTS

    row1 = pl.BlockSpec((1, tm, d), lambda i: (i // spt, i % spt, 0))
    mod1 = pl.BlockSpec((1, 1, d), lambda i: (i // spt, 0, 0))
    slab1 = pl.BlockSpec((tm, LANES), lambda i: (i, 0))
    w_r = jnp.concatenate([router, jnp.zeros((d, LANES - N_EXPERTS), F32)], 1)
    meta, counts = pl.pallas_call(
        _router_kernel,
        grid=(nt,),
        in_specs=[row1, mod1, mod1, pl.BlockSpec((d, LANES), lambda i: (0, 0))],
        out_specs=[slab1, pl.BlockSpec((1, LANES), lambda i: (0, 0))],
        out_shape=[jax.ShapeDtypeStruct((n, LANES), F32), jax.ShapeDtypeStruct((1, LANES), F32)],
        scratch_shapes=[pltpu.VMEM((1, LANES), F32)],
        compiler_params=_cparams(("arbitrary",)),
        name="moe_router",
    )(x, sc, sh, w_r)

    cnt = counts[0, :N_EXPERTS].astype(jnp.int32)
    nblk = (cnt + MOE_BLOCK - 1) // MOE_BLOCK
    ends = jnp.cumsum(nblk)
    first_row = (ends - nblk) * MOE_BLOCK
    nvalid = ends[-1]
    eid = meta[:, :TOP_K].astype(jnp.int32)
    rank = meta[:, 4:4 + TOP_K].astype(jnp.int32)
    base = jnp.sum(jnp.where(eid[:, :, None] == jnp.arange(N_EXPERTS), first_row, 0), -1)
    dest = (base + rank).reshape(nt, TOP_K * tm)
    tbl = jnp.concatenate([first_row + cnt, ends * MOE_BLOCK, nvalid.reshape(1)]).astype(jnp.int32)
    bi = jnp.minimum(jnp.arange(nb, dtype=jnp.int32), nvalid - 1)
    blk_e = jnp.sum(bi[:, None] >= ends[None, :], -1).astype(jnp.int32)

    row1p = pl.BlockSpec((1, tm, d), lambda i, t: (i // spt, i % spt, 0))
    mod1p = pl.BlockSpec((1, 1, d), lambda i, t: (i // spt, 0, 0))
    xb = pl.pallas_call(
        functools.partial(_dispatch_kernel, first_tail_block=min_blocks),
        grid_spec=pltpu.PrefetchScalarGridSpec(
            num_scalar_prefetch=1,
            grid=(nt,),
            in_specs=[row1p, mod1p, mod1p, pl.BlockSpec(memory_space=pl.ANY)],
            out_specs=pl.BlockSpec(memory_space=pl.ANY),
            scratch_shapes=[pltpu.VMEM((tm, d), F32), pltpu.VMEM((MOE_BLOCK, d), F32),
                            pltpu.SMEM((TOP_K * tm,), jnp.int32),
                            pltpu.SemaphoreType.DMA, pltpu.SemaphoreType.DMA]),
        out_shape=jax.ShapeDtypeStruct((nb * MOE_BLOCK, d), F32),
        compiler_params=_cparams(("arbitrary",)),
        name="moe_dispatch",
    )(tbl, x, sc, sh, dest)

    last = nj - 1

    def jsel(i, j, nv):
        return jnp.where(i < nv[0], j, last)

    w_in_b = w_in.astype(BF16)
    yb = pl.pallas_call(
        _expert_kernel,
        grid_spec=pltpu.PrefetchScalarGridSpec(
            num_scalar_prefetch=2,
            grid=(nb, nj),
            in_specs=[pl.BlockSpec((MOE_BLOCK, d), lambda i, j, be, nv: (i, 0)),
                      pl.BlockSpec((1, d, th), lambda i, j, be, nv: (be[i], 0, jsel(i, j, nv))),
                      pl.BlockSpec((1, d, th), lambda i, j, be, nv: (be[i], 0, jsel(i, j, nv) + nj)),
                      pl.BlockSpec((1, th, d), lambda i, j, be, nv: (be[i], jsel(i, j, nv), 0))],
            out_specs=pl.BlockSpec((MOE_BLOCK, d), lambda i, j, be, nv: (i, 0)),
            scratch_shapes=[pltpu.VMEM((MOE_BLOCK, d), BF16), pltpu.VMEM((MOE_BLOCK, d), F32)]),
        out_shape=jax.ShapeDtypeStruct((nb * MOE_BLOCK, d), F32),
        compiler_params=_cparams(("arbitrary", "arbitrary")),
        name="moe_experts",
    )(blk_e, nvalid.reshape(1).astype(jnp.int32), xb, w_in_b, w_in_b, w_out.astype(BF16))

    return pl.pallas_call(
        _combine_kernel,
        grid=(nt,),
        in_specs=[row1, mod1, slab1, pl.BlockSpec((1, d), lambda i: (0, 0)), pl.BlockSpec((1, d), lambda i: (0, 0)),
                  pl.BlockSpec(memory_space=pl.ANY), pl.BlockSpec(memory_space=pl.ANY)],
        out_specs=row1,
        out_shape=jax.ShapeDtypeStruct(x.shape, F32),
        scratch_shapes=[pltpu.VMEM((tm, d), F32), pltpu.VMEM((tm, d), F32),
                        pltpu.SMEM((TOP_K * tm,), jnp.int32),
                        pltpu.SemaphoreType.DMA, pltpu.SemaphoreType.DMA],
        compiler_params=_cparams(("arbitrary",)),
        name="moe_combine",
    )(x, gate, meta, ln_g, ln_b, dest, yb)


def kernel(x, c, positions, ada_w, ada_b, ln_g, ln_b, conv_w_in, conv_dw, conv_dw_b, conv_ln_g, conv_ln_b, conv_w_out, sgu_w_in, sgu_b_in, sgu_ln_g, sgu_ln_b, sgu_w_s, sgu_b_s, sgu_w_out, gdn_w_in, gdn_conv, gdn_a_log, gdn_dt_bias, gdn_norm_g, gdn_w_out, mla_w_in, mla_q_norm_g, mla_kv_norm_g, mla_w_uq, mla_w_ukv, mla_w_out, ffn_w_in, ffn_w_out, moe_router, moe_w_in, moe_w_out):
    bsz, s, d = x.shape
    mod = _ada_mod(c, ada_w, ada_b).reshape(DEPTH, bsz, 6, 1, d)
    for i in range(DEPTH):
        sh1, sc1, g1, sh2, sc2, g2 = (mod[i, :, t] for t in range(6))
        lg1, lb1 = ln_g[i, 0].reshape(1, d), ln_b[i, 0].reshape(1, d)
        lg2, lb2 = ln_g[i, 1].reshape(1, d), ln_b[i, 1].reshape(1, d)
        j = i // 4
        mixer = i % 4
        if mixer == 0:
            x = _conformer_layer(x, sc1, sh1, g1, lg1, lb1, conv_w_in[j], conv_dw[j], conv_dw_b[j],
                                 conv_ln_g[j], conv_ln_b[j], conv_w_out[j])
        elif mixer == 1:
            x = _sgu_layer(x, sc1, sh1, g1, lg1, lb1, sgu_w_in[j], sgu_b_in[j], sgu_ln_g[j], sgu_ln_b[j],
                           sgu_w_s[j], sgu_b_s[j], sgu_w_out[j])
        elif mixer == 2:
            x = _gdn_layer(x, sc1, sh1, g1, lg1, lb1, gdn_w_in[j], gdn_conv[j], gdn_a_log[j], gdn_dt_bias[j],
                           gdn_norm_g[j], gdn_w_out[j])
        else:
            x = _mla_layer(x, positions, sc1, sh1, g1, lg1, lb1, mla_w_in[j], mla_q_norm_g[j],
                           mla_kv_norm_g[j], mla_w_uq[j], mla_w_ukv[j], mla_w_out[j])
        if i % 2 == 0:
            x = _swiglu_layer(x, sc2, sh2, g2, lg2, lb2, ffn_w_in[i // 2], ffn_w_out[i // 2])
        else:
            x = _moe_layer(x, sc2, sh2, g2, lg2, lb2, moe_router[i // 2], moe_w_in[i // 2], moe_w_out[i // 2])
    return x
```

```python
import functools
import math

import jax
import jax.numpy as jnp
from jax import lax
from jax.experimental import pallas as pl
from jax.experimental.pallas import tpu as pltpu

F32 = jnp.float32
BF16 = jnp.bfloat16
HI = lax.Precision.HIGHEST

D_MODEL = 1024
DEPTH = 4
ALPHA = (2 * DEPTH) ** 0.25

CONV_WIDTH = 31
CONV_HALO = 32
CONV_ROWS = 128
SGU_CHUNK = 128
SGU_GROUPS = 8
SGU_HALF = 2 * D_MODEL
GDN_HEADS = 8
GDN_DK = 128
GDN_DV = 128
GDN_CONV = 4
GDN_TILE = 256
GDN_TAIL = 8
MLA_HEADS = 8
MLA_Q_RANK = 512
MLA_KV_RANK = 256
MLA_NOPE = 128
MLA_ROPE = 64
MLA_V = 128
ROPE_THETA = 10000.0
FFN_HIDDEN = 7 * D_MODEL // 2
N_EXPERTS = 8
TOP_K = 2
LANES = 128
SUBLANES = 8
MOE_BLOCK = 1024
VMEM_LIMIT = 56 * 1024 * 1024


def _cparams(sem):
    return pltpu.CompilerParams(dimension_semantics=sem, vmem_limit_bytes=VMEM_LIMIT)


def _sigmoid(x):
    return 1.0 / (1.0 + jnp.exp(-x))


def _silu(x):
    return x * _sigmoid(x)


def _ln(z, g, b, eps=1e-5):
    mu = jnp.mean(z, -1, keepdims=True)
    zc = z - mu
    var = jnp.mean(zc * zc, -1, keepdims=True)
    return zc * lax.rsqrt(var + eps) * g + b


def _res_ln(x, y, gate, g, b):
    return _ln(ALPHA * x + (1.0 + gate) * y, g, b)


def _dot(a, b):
    return jnp.dot(a, b, preferred_element_type=F32)


def _dot_nt(a, b):
    return lax.dot_general(a, b, (((1,), (1,)), ((), ())), preferred_element_type=F32)


def _dot_tn(a, b):
    return lax.dot_general(a, b, (((0,), (0,)), ((), ())), preferred_element_type=F32)


def _dot_hi(a, b):
    return jnp.dot(a, b, precision=HI, preferred_element_type=F32)


def _ada_kernel(c_ref, w_ref, b_ref, o_ref):
    cond = _silu(c_ref[...])
    o_ref[0] = _dot_hi(cond, w_ref[0]) + b_ref[0]


def _ada_mod(c, ada_w, ada_b):
    bsz, d = c.shape
    n_out = ada_w.shape[-1]
    tn = 1024
    return pl.pallas_call(
        _ada_kernel,
        grid=(DEPTH, n_out // tn),
        in_specs=[
            pl.BlockSpec((bsz, d), lambda i, j: (0, 0)),
            pl.BlockSpec((1, d, tn), lambda i, j: (i, 0, j)),
            pl.BlockSpec((1, 1, tn), lambda i, j: (i, 0, j)),
        ],
        out_specs=pl.BlockSpec((1, bsz, tn), lambda i, j: (i, 0, j)),
        out_shape=jax.ShapeDtypeStruct((DEPTH, bsz, n_out), F32),
        compiler_params=_cparams(("parallel", "parallel")),
        name="ada_mod",
    )(c, ada_w, ada_b.reshape(DEPTH, 1, n_out))


def _row_spec(tm, width):
    return pl.BlockSpec((1, tm, width), lambda b, i: (b, i, 0))


def _mod_spec(width=D_MODEL):
    return pl.BlockSpec((1, 1, width), lambda b, i: (b, 0, 0))


def _full_spec(shape):
    nd = len(shape)
    return pl.BlockSpec(shape, lambda b, i: (0,) * nd)


def _outproj_kernel(a_ref, w_ref, x_ref, gate_ref, g_ref, b_ref, o_ref):
    y = _dot(a_ref[0], w_ref[...])
    o_ref[0] = _res_ln(x_ref[0], y, gate_ref[0], g_ref[...], b_ref[...])


def _outproj_ln(a, w, x, gate, ln_g, ln_b, tm=512):
    bsz, s, k = a.shape
    d = x.shape[-1]
    return pl.pallas_call(
        _outproj_kernel,
        grid=(bsz, s // tm),
        in_specs=[_row_spec(tm, k), _full_spec((k, d)), _row_spec(tm, d), _mod_spec(d),
                  _full_spec((1, d)), _full_spec((1, d))],
        out_specs=_row_spec(tm, d),
        out_shape=jax.ShapeDtypeStruct(x.shape, F32),
        compiler_params=_cparams(("parallel", "parallel")),
        name="outproj_ln",
    )(a, w, x, gate, ln_g, ln_b)


def _conv_in_kernel(x_ref, sc_ref, sh_ref, w_ref, o_ref):
    h = (x_ref[0] * (1.0 + sc_ref[0]) + sh_ref[0]).astype(BF16)
    ag = _dot(h, w_ref[...])
    d = o_ref.shape[-1]
    o_ref[0] = ag[:, :d] * _sigmoid(ag[:, d:])


def _conv_mid_kernel(y_ref, halo_ref, dw_ref, dwb_ref, cg_ref, cb_ref, w_ref,
                     x_ref, gate_ref, g_ref, b_ref, o_ref, ybuf, cbuf, shbuf):
    tm = y_ref.shape[1]
    d = y_ref.shape[2]
    first = pl.program_id(1) == 0
    ybuf[0:CONV_HALO, :] = jnp.where(first, 0.0, halo_ref[0])
    ybuf[CONV_HALO:, :] = y_ref[0]
    rc = CONV_ROWS
    off = CONV_HALO - (CONV_WIDTH - 1)
    def col_block(ci, carry):
        cols = pl.ds(pl.multiple_of(ci * LANES, LANES), LANES)
        for r0 in range(0, tm, rc):
            acc = None
            for r in range(SUBLANES):
                ks = [k for k in range(CONV_WIDTH) if (off + k) % SUBLANES == r]
                span = max((off + k) // SUBLANES for k in ks) * SUBLANES + rc
                shbuf[r, 0:span, :] = ybuf[r0 + r:r0 + r + span, cols]
                for k in ks:
                    q8 = (off + k) // SUBLANES * SUBLANES
                    term = dw_ref[k:k + 1, cols] * shbuf[r, q8:q8 + rc, :]
                    acc = term if acc is None else acc + term
            cbuf[r0:r0 + rc, cols] = acc
        return carry

    lax.fori_loop(0, d // LANES, col_block, 0)
    yc = cbuf[...] + dwb_ref[...]
    yn = _silu(_ln(yc, cg_ref[...], cb_ref[...]))
    yo = _dot(yn.astype(BF16), w_ref[...])
    o_ref[0] = _res_ln(x_ref[0], yo, gate_ref[0], g_ref[...], b_ref[...])


def _conformer_layer(x, sc, sh, gate, ln_g, ln_b, w_in, dw, dw_b, cg, cb, w_out, tm=512):
    bsz, s, d = x.shape
    y = pl.pallas_call(
        _conv_in_kernel,
        grid=(bsz, s // tm),
        in_specs=[_row_spec(tm, d), _mod_spec(), _mod_spec(), _full_spec((d, 2 * d))],
        out_specs=_row_spec(tm, d),
        out_shape=jax.ShapeDtypeStruct((bsz, s, d), F32),
        compiler_params=_cparams(("parallel", "parallel")),
        name="conv_in",
    )(x, sc, sh, w_in.astype(BF16))
    hb = tm // CONV_HALO
    halo_spec = pl.BlockSpec((1, CONV_HALO, d), lambda b, i: (b, jnp.maximum(i * hb - 1, 0), 0))
    dw_pad = jnp.concatenate([dw, jnp.zeros((1, d), F32)], 0)
    return pl.pallas_call(
        _conv_mid_kernel,
        grid=(bsz, s // tm),
        in_specs=[_row_spec(tm, d), halo_spec, _full_spec((CONV_WIDTH + 1, d)), _full_spec((1, d)),
                  _full_spec((1, d)), _full_spec((1, d)), _full_spec((d, d)),
                  _row_spec(tm, d), _mod_spec(), _full_spec((1, d)), _full_spec((1, d))],
        out_specs=_row_spec(tm, d),
        out_shape=jax.ShapeDtypeStruct((bsz, s, d), F32),
        scratch_shapes=[pltpu.VMEM((tm + CONV_HALO, d), F32), pltpu.VMEM((tm, d), F32),
                        pltpu.VMEM((SUBLANES, CONV_ROWS + CONV_HALO, LANES), F32)],
        compiler_params=_cparams(("parallel", "parallel")),
        name="conv_mid",
    )(y, y, dw_pad, dw_b.reshape(1, d), cg.reshape(1, d), cb.reshape(1, d), w_out.astype(BF16),
      x, gate, ln_g, ln_b)


def _swiglu_kernel(x_ref, sc_ref, sh_ref, wa_ref, wb_ref, wo_ref, gate_ref, g_ref, b_ref,
                   o_ref, hbuf, acc):
    j = pl.program_id(2)

    @pl.when(j == 0)
    def _():
        hbuf[...] = (x_ref[0] * (1.0 + sc_ref[0]) + sh_ref[0]).astype(BF16)
        acc[...] = jnp.zeros_like(acc)

    h = hbuf[...]
    a = _dot(h, wa_ref[...])
    b = _dot(h, wb_ref[...])
    act = (_silu(a) * b).astype(BF16)
    acc[...] += _dot(act, wo_ref[...])

    @pl.when(j == pl.num_programs(2) - 1)
    def _():
        o_ref[0] = _res_ln(x_ref[0], acc[...], gate_ref[0], g_ref[...], b_ref[...])


def _swiglu_layer(x, sc, sh, gate, ln_g, ln_b, w_in, w_out, tm=512, th=1792):
    bsz, s, d = x.shape
    hid = w_out.shape[0]
    nj = hid // th
    row = pl.BlockSpec((1, tm, d), lambda b, i, j: (b, i, 0))
    mod = pl.BlockSpec((1, 1, d), lambda b, i, j: (b, 0, 0))
    vec = pl.BlockSpec((1, d), lambda b, i, j: (0, 0))
    w_in_b = w_in.astype(BF16)
    return pl.pallas_call(
        _swiglu_kernel,
        grid=(bsz, s // tm, nj),
        in_specs=[row, mod, mod,
                  pl.BlockSpec((d, th), lambda b, i, j: (0, j)),
                  pl.BlockSpec((d, th), lambda b, i, j: (0, j + nj)),
                  pl.BlockSpec((th, d), lambda b, i, j: (j, 0)),
                  mod, vec, vec],
        out_specs=row,
        out_shape=jax.ShapeDtypeStruct(x.shape, F32),
        scratch_shapes=[pltpu.VMEM((tm, d), BF16), pltpu.VMEM((tm, d), F32)],
        compiler_params=_cparams(("parallel", "parallel", "arbitrary")),
        name="swiglu",
    )(x, sc, sh, w_in_b, w_in_b, w_out.astype(BF16), gate, ln_g, ln_b)


def _sgu_kernel(x_ref, sc_ref, sh_ref, wi_ref, bi_ref, vg_ref, vb_ref, ws_ref, bs_ref, wo_ref,
                gate_ref, g_ref, b_ref, o_ref, gbuf):
    tm = x_ref.shape[1]
    h = (x_ref[0] * (1.0 + sc_ref[0]) + sh_ref[0]).astype(BF16)
    z = _dot(h, wi_ref[...]) + bi_ref[...]
    z = 0.5 * z * (1.0 + lax.erf(z * (2.0 ** -0.5)))
    u = z[:, :SGU_HALF]
    v = _ln(z[:, SGU_HALF:], vg_ref[...], vb_ref[...]).astype(BF16)
    gw = SGU_HALF // SGU_GROUPS
    ri = lax.broadcasted_iota(jnp.int32, (SGU_CHUNK, SGU_CHUNK), 0)
    ci = lax.broadcasted_iota(jnp.int32, (SGU_CHUNK, SGU_CHUNK), 1)
    causal = ci <= ri
    for g in range(SGU_GROUPS):
        wsg = jnp.where(causal, ws_ref[g], 0.0).astype(BF16)
        bias = bs_ref[:, g:g + 1]
        for c in range(tm // SGU_CHUNK):
            rows = slice(c * SGU_CHUNK, (c + 1) * SGU_CHUNK)
            cols = slice(g * gw, (g + 1) * gw)
            sv = _dot(wsg, v[rows, cols]) + bias
            gbuf[rows, cols] = (u[rows, cols] * sv).astype(BF16)
    yo = _dot(gbuf[...], wo_ref[...])
    o_ref[0] = _res_ln(x_ref[0], yo, gate_ref[0], g_ref[...], b_ref[...])


def _sgu_layer(x, sc, sh, gate, ln_g, ln_b, w_in, b_in, vg, vb, w_s, b_s, w_out, tm=512):
    bsz, s, d = x.shape
    return pl.pallas_call(
        _sgu_kernel,
        grid=(bsz, s // tm),
        in_specs=[_row_spec(tm, d), _mod_spec(), _mod_spec(),
                  _full_spec((d, 2 * SGU_HALF)), _full_spec((1, 2 * SGU_HALF)),
                  _full_spec((1, SGU_HALF)), _full_spec((1, SGU_HALF)),
                  _full_spec((SGU_GROUPS, SGU_CHUNK, SGU_CHUNK)), _full_spec((SGU_CHUNK, SGU_GROUPS)),
                  _full_spec((SGU_HALF, d)), _mod_spec(), _full_spec((1, d)), _full_spec((1, d))],
        out_specs=_row_spec(tm, d),
        out_shape=jax.ShapeDtypeStruct(x.shape, F32),
        scratch_shapes=[pltpu.VMEM((tm, SGU_HALF), BF16)],
        compiler_params=_cparams(("parallel", "parallel")),
        name="sgu",
    )(x, sc, sh, w_in.astype(BF16), b_in.reshape(1, -1), vg.reshape(1, -1), vb.reshape(1, -1),
      w_s, b_s.T, w_out.astype(BF16), gate, ln_g, ln_b)


def _gdn_in_kernel(x_ref, sc_ref, sh_ref, wq_ref, wz_ref, wba_ref, alog_ref, dtb_ref,
                   qkv_ref, z_ref, bg_ref, gct_ref):
    tm = x_ref.shape[1]
    c = GDN_TILE
    hf = x_ref[0] * (1.0 + sc_ref[0]) + sh_ref[0]
    h = hf.astype(BF16)
    qkv_ref[0] = _dot(h, wq_ref[...])
    z_ref[0] = _dot(h, wz_ref[...]).astype(BF16)
    ba = _dot_hi(hf, wba_ref[...])
    bg_ref[0, :, :LANES] = _sigmoid(ba)
    a_in = ba + dtb_ref[...]
    softplus = jnp.maximum(a_in, 0.0) + jnp.log1p(jnp.exp(-jnp.abs(a_in)))
    g_all = -jnp.exp(alog_ref[...]) * softplus
    ri = lax.broadcasted_iota(jnp.int32, (c, c), 0)
    ci = lax.broadcasted_iota(jnp.int32, (c, c), 1)
    tril = (ci <= ri).astype(F32)
    for r0 in range(0, tm, c):
        gc = _dot_hi(tril, g_all[r0:r0 + c])
        bg_ref[0, r0:r0 + c, LANES:] = gc
        gct_ref[0, :, r0:r0 + c] = gc.T[GDN_HEADS:2 * GDN_HEADS]


def _gdn_chunk_kernel(q_ref, k_ref, v_ref, cwq_ref, cwk_ref, cwv_ref, bg_ref, gct_ref,
                      z_ref, ng_ref, o_ref, tail, cbuf, state, *, hp):
    c = q_ref.shape[1]

    @pl.when(pl.program_id(2) == 0)
    def _():
        tail[...] = jnp.zeros_like(tail)
        state[...] = jnp.zeros_like(state)

    off = GDN_TAIL - (GDN_CONV - 1)
    convd = []
    for idx, (ref, cw) in enumerate(((q_ref, cwq_ref), (k_ref, cwk_ref), (v_ref, cwv_ref))):
        cbuf[idx, 0:GDN_TAIL, :] = tail[idx]
        cbuf[idx, GDN_TAIL:, :] = ref[0]
        tail[idx] = ref[0, c - GDN_TAIL:c, :]
        acc = cw[0:1, :] * cbuf[idx, pl.ds(off, c), :]
        for t in range(1, GDN_CONV):
            acc = acc + cw[t:t + 1, :] * cbuf[idx, pl.ds(off + t, c), :]
        convd.append(_silu(acc))
    qc_all, kc_all, v_all = convd

    lane = lax.broadcasted_iota(jnp.int32, (1, LANES), 1)
    sub = lax.broadcasted_iota(jnp.int32, (GDN_HEADS, 1), 0)
    ri = lax.broadcasted_iota(jnp.int32, (c, c), 0)
    ci = lax.broadcasted_iota(jnp.int32, (c, c), 1)
    causal = ci <= ri
    strict = ci < ri
    xor = ri ^ ci
    eye = (ri == ci).astype(F32)
    bg = bg_ref[0]
    gct = gct_ref[0]
    z_all = z_ref[0]
    heads = range(hp)
    q, k, v, kb, beta, gc, dmat, a_mat = ([None] * hp for _ in range(8))
    for t in heads:
        hd = pl.program_id(1) * hp + t
        cols = slice(t * LANES, (t + 1) * LANES)
        qc, kc, v[t] = qc_all[:, cols], kc_all[:, cols], v_all[:, cols]
        q[t] = qc * lax.rsqrt(jnp.sum(qc * qc, -1, keepdims=True) + 1e-6) * (GDN_DK ** -0.5)
        k[t] = kc * lax.rsqrt(jnp.sum(kc * kc, -1, keepdims=True) + 1e-6)
        sel = lane == hd
        beta[t] = jnp.sum(jnp.where(sel, bg[:, :LANES], 0.0), -1, keepdims=True)
        gc[t] = jnp.sum(jnp.where(lane == GDN_HEADS + hd, bg[:, LANES:], 0.0), -1, keepdims=True)
        gc_row = jnp.sum(jnp.where(sub == hd, gct, 0.0), 0, keepdims=True)
        dmat[t] = jnp.where(causal, jnp.exp(jnp.where(causal, gc[t] - gc_row, 0.0)), 0.0)
        kb[t] = k[t].astype(BF16)
    for t in heads:
        a_mat[t] = jnp.where(strict, beta[t] * _dot_nt(kb[t], kb[t]) * dmat[t], 0.0)
    inv = [eye - jnp.where(xor == 1, a_mat[t], 0.0) for t in heads]
    lvl = 1
    while (1 << lvl) < c:
        invb = [inv[t].astype(BF16) for t in heads]
        joins = (xor >> lvl) == 1
        dm = [_dot(invb[t], jnp.where(joins, a_mat[t], 0.0).astype(BF16)) for t in heads]
        inv = [inv[t] - _dot(dm[t].astype(BF16), invb[t]) for t in heads]
        lvl += 1

    egc = [jnp.exp(gc[t]) for t in heads]
    sol = [_dot(inv[t].astype(BF16),
                jnp.concatenate([v[t] * beta[t], k[t] * (beta[t] * egc[t])], -1).astype(BF16)) for t in heads]
    qk = [jnp.where(causal, _dot_nt(q[t].astype(BF16), kb[t]) * dmat[t], 0.0).astype(BF16) for t in heads]
    s_prev = [state[t] for t in heads]
    sb = [s_prev[t].astype(BF16) for t in heads]
    vnb = [(sol[t][:, :GDN_DV] - _dot(sol[t][:, GDN_DV:].astype(BF16), sb[t])).astype(BF16) for t in heads]
    o = [_dot((q[t] * egc[t]).astype(BF16), sb[t]) + _dot(qk[t], vnb[t]) for t in heads]
    for t in heads:
        g_last = gc[t][c - 1:c, :]
        k_dec = (k[t] * jnp.exp(g_last - gc[t])).astype(BF16)
        state[t] = s_prev[t] * jnp.exp(g_last) + _dot_tn(k_dec, vnb[t])
    for t in heads:
        cols = slice(t * LANES, (t + 1) * LANES)
        on = o[t] * lax.rsqrt(jnp.mean(o[t] * o[t], -1, keepdims=True) + 1e-6) * ng_ref[...]
        o_ref[0, :, cols] = (on * _silu(z_all[:, cols].astype(F32))).astype(BF16)


def _gdn_layer(x, sc, sh, gate, ln_g, ln_b, w_in, conv_w, a_log, dt_bias, norm_g, w_out, tm=512, hp=4):
    bsz, s, d = x.shape
    nh = GDN_HEADS
    nqkv = 3 * nh * GDN_DK
    w_qkv = w_in[:, :nqkv].astype(BF16)
    w_z = w_in[:, nqkv:nqkv + nh * GDN_DV].astype(BF16)
    w_b = w_in[:, nqkv + nh * GDN_DV:nqkv + nh * GDN_DV + nh]
    w_a = w_in[:, nqkv + nh * GDN_DV + nh:]
    w_ba = jnp.concatenate([w_b, w_a, jnp.zeros((d, LANES - 2 * nh), F32)], 1)
    head_pad, lane_pad = jnp.zeros((nh,), F32), jnp.zeros((LANES - 2 * nh,), F32)
    alog = jnp.concatenate([head_pad, a_log, lane_pad]).reshape(1, LANES)
    dtb = jnp.concatenate([head_pad, dt_bias, lane_pad]).reshape(1, LANES)
    qkv, z, bg, gct = pl.pallas_call(
        _gdn_in_kernel,
        grid=(bsz, s // tm),
        in_specs=[_row_spec(tm, d), _mod_spec(), _mod_spec(), _full_spec((d, nqkv)),
                  _full_spec((d, nh * GDN_DV)), _full_spec((d, LANES)),
                  _full_spec((1, LANES)), _full_spec((1, LANES))],
        out_specs=[_row_spec(tm, nqkv), _row_spec(tm, nh * GDN_DV), _row_spec(tm, 2 * LANES),
                   pl.BlockSpec((1, nh, tm), lambda b, i: (b, 0, i))],
        out_shape=[jax.ShapeDtypeStruct((bsz, s, nqkv), F32),
                   jax.ShapeDtypeStruct((bsz, s, nh * GDN_DV), BF16),
                   jax.ShapeDtypeStruct((bsz, s, 2 * LANES), F32),
                   jax.ShapeDtypeStruct((bsz, nh, s), F32)],
        compiler_params=_cparams(("parallel", "parallel")),
        name="gdn_in",
    )(x, sc, sh, w_qkv, w_z, w_ba, alog, dtb)

    c = GDN_TILE
    wide = hp * LANES
    ng = nh // hp

    def col(base):
        return pl.BlockSpec((1, c, wide), lambda b, h, i: (b, i, base + h))

    def cw(base):
        return pl.BlockSpec((GDN_CONV, wide), lambda b, h, i: (0, base + h))

    og = pl.pallas_call(
        functools.partial(_gdn_chunk_kernel, hp=hp),
        grid=(bsz, ng, s // c),
        in_specs=[col(0), col(ng), col(2 * ng), cw(0), cw(ng), cw(2 * ng),
                  pl.BlockSpec((1, c, 2 * LANES), lambda b, h, i: (b, i, 0)),
                  pl.BlockSpec((1, nh, c), lambda b, h, i: (b, 0, i)),
                  col(0), pl.BlockSpec((1, LANES), lambda b, h, i: (0, 0))],
        out_specs=col(0),
        out_shape=jax.ShapeDtypeStruct((bsz, s, nh * GDN_DV), BF16),
        scratch_shapes=[pltpu.VMEM((3, GDN_TAIL, wide), F32), pltpu.VMEM((3, c + GDN_TAIL, wide), F32),
                        pltpu.VMEM((hp, GDN_DK, GDN_DV), F32)],
        compiler_params=_cparams(("parallel", "parallel", "arbitrary")),
        name="gdn_chunk",
    )(qkv, qkv, qkv, conv_w, conv_w, conv_w, bg, gct, z, norm_g.reshape(1, LANES))
    return _outproj_ln(og, w_out.astype(BF16), x, gate, ln_g, ln_b)


def _mla_in_kernel(x_ref, sc_ref, sh_ref, pos_ref, wi_ref, qg_ref, kg_ref, wqn_ref, wqr_ref, wqt_ref,
                   wkv_ref, qn_ref, qr_ref, kn_ref, kr_ref, v_ref):
    h = (x_ref[0] * (1.0 + sc_ref[0]) + sh_ref[0]).astype(BF16)
    p = _dot(h, wi_ref[...])
    cq = p[:, :MLA_Q_RANK]
    ckv = p[:, MLA_Q_RANK:MLA_Q_RANK + MLA_KV_RANK]
    kr = p[:, MLA_Q_RANK + MLA_KV_RANK:MLA_Q_RANK + MLA_KV_RANK + LANES]
    krt = p[:, MLA_Q_RANK + MLA_KV_RANK + LANES:]
    cqn = (cq * lax.rsqrt(jnp.mean(cq * cq, -1, keepdims=True) + 1e-6) * qg_ref[...]).astype(BF16)
    ckn = (ckv * lax.rsqrt(jnp.mean(ckv * ckv, -1, keepdims=True) + 1e-6) * kg_ref[...]).astype(BF16)
    lane = lax.broadcasted_iota(jnp.int32, (1, LANES), 1)
    half = MLA_ROPE // 2
    fidx = (lane % half).astype(F32)
    inv_freq = jnp.exp(fidx * (-math.log(ROPE_THETA) / half))
    ang = pos_ref[0].astype(F32) * inv_freq
    live = lane < MLA_ROPE
    cos = jnp.where(live, jnp.cos(ang), 0.0)
    sin = jnp.where(live, jnp.sin(ang), 0.0)
    scale = (MLA_NOPE + MLA_ROPE) ** -0.5 * math.log2(math.e)
    qn_ref[0] = (_dot(cqn, wqn_ref[...]) * scale).astype(BF16)
    qr = _dot(cqn, wqr_ref[...])
    qrt = _dot(cqn, wqt_ref[...])
    for hd in range(MLA_HEADS):
        cs = slice(hd * LANES, (hd + 1) * LANES)
        qr_ref[0, :, cs] = ((qr[:, cs] * cos + qrt[:, cs] * sin) * scale).astype(BF16)
    kr_ref[0] = (kr * cos + krt * sin).astype(BF16)
    kv = _dot(ckn, wkv_ref[...])
    nk = MLA_HEADS * MLA_NOPE
    kn_ref[0] = kv[:, :nk].astype(BF16)
    ones = jnp.ones((kv.shape[0], LANES), BF16)
    for hd in range(MLA_HEADS):
        v_ref[0, :, 2 * hd * LANES:(2 * hd + 1) * LANES] = kv[:, nk + hd * MLA_V:nk + (hd + 1) * MLA_V].astype(BF16)
        v_ref[0, :, (2 * hd + 1) * LANES:(2 * hd + 2) * LANES] = ones


def _mla_attn_kernel(qn_ref, qr_ref, kn_ref, kr_ref, v_ref, o_ref, qbuf, m_sc, acc_sc, *, hp):
    tq = qn_ref.shape[1]
    tk = tq
    i = pl.program_id(2)
    for h in range(hp):
        cols = slice(h * LANES, (h + 1) * LANES)
        qbuf[h, :, :LANES] = qn_ref[0, :, cols]
        qbuf[h, :, LANES:] = qr_ref[0, :, cols]
    m_sc[...] = jnp.full_like(m_sc, -jnp.inf)
    acc_sc[...] = jnp.zeros_like(acc_sc)

    def step(j, masked):
        r0 = pl.multiple_of(j * tk, tk)
        krt = kr_ref[0, pl.ds(r0, tk), :]
        if masked:
            ri = lax.broadcasted_iota(jnp.int32, (tq, tk), 0)
            ci = lax.broadcasted_iota(jnp.int32, (tq, tk), 1)
            keep = ci <= ri
        scs, ps, alphas = [], [], []
        for h in range(hp):
            cols = slice(h * LANES, (h + 1) * LANES)
            kt = jnp.concatenate([kn_ref[0, pl.ds(r0, tk), cols], krt], -1)
            scs.append(_dot_nt(qbuf[h], kt))
        for h in range(hp):
            sc = jnp.where(keep, scs[h], -jnp.inf) if masked else scs[h]
            m_prev = m_sc[h]
            m_new = jnp.maximum(m_prev, jnp.max(sc, -1, keepdims=True))
            alphas.append(jnp.exp2(m_prev - m_new))
            m_sc[h] = m_new
            ps.append(jnp.concatenate(
                [jnp.exp2(sc[:, c0:c0 + LANES] - m_new) for c0 in range(0, tk, LANES)], -1).astype(BF16))
        for h in range(hp):
            vcols = slice(2 * h * LANES, (2 * h + 2) * LANES)
            alpha2 = jnp.concatenate([alphas[h], alphas[h]], -1)
            acc_sc[h] = alpha2 * acc_sc[h] + _dot(ps[h], v_ref[0, pl.ds(r0, tk), vcols])

    def body(j, carry):
        step(j, False)
        return carry

    lax.fori_loop(0, i, body, 0)
    step(i, True)
    for h in range(hp):
        o_ref[0, :, h * LANES:(h + 1) * LANES] = (acc_sc[h, :, :LANES] / acc_sc[h, :, LANES:]).astype(BF16)


def _mla_layer(x, positions, sc, sh, gate, ln_g, ln_b, w_in, q_norm_g, kv_norm_g, w_uq, w_ukv, w_out,
               tm=512, tq=512, hp=4):
    bsz, s, d = x.shape
    nh = MLA_HEADS
    half = MLA_ROPE // 2
    rot = jnp.concatenate([jnp.arange(half, MLA_ROPE), jnp.arange(half)])
    sign = jnp.concatenate([-jnp.ones((half,), F32), jnp.ones((half,), F32)])
    zpad = jnp.zeros((d, LANES - MLA_ROPE), F32)
    w_kr = w_in[:, MLA_Q_RANK + MLA_KV_RANK:]
    w_in_ext = jnp.concatenate(
        [w_in[:, :MLA_Q_RANK + MLA_KV_RANK], w_kr, zpad, w_kr[:, rot] * sign, zpad], 1).astype(BF16)
    wq = w_uq.reshape(MLA_Q_RANK, nh, MLA_NOPE + MLA_ROPE)
    w_qn = wq[:, :, :MLA_NOPE].reshape(MLA_Q_RANK, nh * MLA_NOPE).astype(BF16)
    wq_r = wq[:, :, MLA_NOPE:]
    hpad = jnp.zeros((MLA_Q_RANK, nh, LANES - MLA_ROPE), F32)
    w_qr = jnp.concatenate([wq_r, hpad], -1).reshape(MLA_Q_RANK, nh * LANES).astype(BF16)
    w_qt = jnp.concatenate([wq_r[:, :, rot] * sign, hpad], -1).reshape(MLA_Q_RANK, nh * LANES).astype(BF16)
    wkv = w_ukv.reshape(MLA_KV_RANK, nh, MLA_NOPE + MLA_V)
    w_kv = jnp.concatenate([wkv[:, :, :MLA_NOPE].reshape(MLA_KV_RANK, nh * MLA_NOPE),
                            wkv[:, :, MLA_NOPE:].reshape(MLA_KV_RANK, nh * MLA_V)], 1).astype(BF16)
    n_in = w_in_ext.shape[1]
    wide = nh * LANES
    qn, qr, kn, kr, v = pl.pallas_call(
        _mla_in_kernel,
        grid=(bsz, s // tm),
        in_specs=[_row_spec(tm, d), _mod_spec(), _mod_spec(), _row_spec(tm, 1),
                  _full_spec((d, n_in)), _full_spec((1, MLA_Q_RANK)), _full_spec((1, MLA_KV_RANK)),
                  _full_spec((MLA_Q_RANK, wide)), _full_spec((MLA_Q_RANK, wide)),
                  _full_spec((MLA_Q_RANK, wide)), _full_spec((MLA_KV_RANK, 2 * wide))],
        out_specs=[_row_spec(tm, wide), _row_spec(tm, wide), _row_spec(tm, wide), _row_spec(tm, LANES),
                   _row_spec(tm, 2 * wide)],
        out_shape=[jax.ShapeDtypeStruct((bsz, s, wide), BF16), jax.ShapeDtypeStruct((bsz, s, wide), BF16),
                   jax.ShapeDtypeStruct((bsz, s, wide), BF16), jax.ShapeDtypeStruct((bsz, s, LANES), BF16),
                   jax.ShapeDtypeStruct((bsz, s, 2 * wide), BF16)],
        compiler_params=_cparams(("parallel", "parallel")),
        name="mla_in",
    )(x, sc, sh, positions.reshape(bsz, s, 1), w_in_ext, q_norm_g.reshape(1, -1), kv_norm_g.reshape(1, -1),
      w_qn, w_qr, w_qt, w_kv)

    gw = hp * LANES
    qspec = pl.BlockSpec((1, tq, gw), lambda b, h, i: (b, i, h))
    kspec = pl.BlockSpec((1, s, gw), lambda b, h, i: (b, 0, h))
    o = pl.pallas_call(
        functools.partial(_mla_attn_kernel, hp=hp),
        grid=(bsz, nh // hp, s // tq),
        in_specs=[qspec, qspec, kspec, pl.BlockSpec((1, s, LANES), lambda b, h, i: (b, 0, 0)),
                  pl.BlockSpec((1, s, 2 * gw), lambda b, h, i: (b, 0, h))],
        out_specs=qspec,
        out_shape=jax.ShapeDtypeStruct((bsz, s, wide), BF16),
        scratch_shapes=[pltpu.VMEM((hp, tq, 2 * LANES), BF16), pltpu.VMEM((hp, tq, LANES), F32),
                        pltpu.VMEM((hp, tq, 2 * MLA_V), F32)],
        compiler_params=_cparams(("parallel", "parallel", "arbitrary")),
        name="mla_attn",
    )(qn, qr, kn, kr, v)
    return _outproj_ln(o, w_out.astype(BF16), x, gate, ln_g, ln_b)


def _router_kernel(x_ref, sc_ref, sh_ref, wr_ref, meta_ref, cnt_ref, run):
    tm = x_ref.shape[1]

    @pl.when(pl.program_id(0) == 0)
    def _():
        run[...] = jnp.zeros_like(run)

    h = x_ref[0] * (1.0 + sc_ref[0]) + sh_ref[0]
    lane = lax.broadcasted_iota(jnp.int32, (tm, LANES), 1)
    lane_f = lane.astype(F32)
    logits = jnp.where(lane < N_EXPERTS, _dot_hi(h, wr_ref[...]), -jnp.inf)
    m1 = jnp.max(logits, -1, keepdims=True)
    i1 = jnp.min(jnp.where(logits == m1, lane_f, float(LANES)), -1, keepdims=True)
    oh1 = lane_f == i1
    rest = jnp.where(oh1, -jnp.inf, logits)
    m2 = jnp.max(rest, -1, keepdims=True)
    i2 = jnp.min(jnp.where(rest == m2, lane_f, float(LANES)), -1, keepdims=True)
    oh2 = lane_f == i2
    e21 = jnp.exp(m2 - m1)
    g1 = 1.0 / (1.0 + e21)
    g2 = e21 / (1.0 + e21)
    cnt = oh1.astype(F32) + oh2.astype(F32)
    ri = lax.broadcasted_iota(jnp.int32, (tm, tm), 0)
    ci = lax.broadcasted_iota(jnp.int32, (tm, tm), 1)
    before = _dot((ci < ri).astype(BF16), cnt.astype(BF16)) + run[...]
    r1 = jnp.sum(jnp.where(oh1, before, 0.0), -1, keepdims=True)
    r2 = jnp.sum(jnp.where(oh2, before, 0.0), -1, keepdims=True)
    meta = jnp.zeros((tm, LANES), F32)
    for k, col in enumerate((i1, i2, g1, g2, r1, r2)):
        meta = jnp.where(lane == k, col, meta)
    meta_ref[...] = meta
    run[...] = run[...] + jnp.sum(cnt, 0, keepdims=True)
    cnt_ref[...] = run[...]


def _dispatch_kernel(tbl_sm, x_ref, sc_ref, sh_ref, dest_hbm, xb_hbm, hbuf, zbuf, idx_sm, sem, isem, *,
                     first_tail_block):
    tm = x_ref.shape[1]
    i = pl.program_id(0)
    icp = pltpu.make_async_copy(dest_hbm.at[i], idx_sm, isem)
    icp.start()
    hbuf[...] = x_ref[0] * (1.0 + sc_ref[0]) + sh_ref[0]
    icp.wait()

    def issue(r, carry):
        pltpu.make_async_copy(hbuf.at[pl.ds(r, 1)], xb_hbm.at[pl.ds(idx_sm[2 * r], 1)], sem).start()
        pltpu.make_async_copy(hbuf.at[pl.ds(r, 1)], xb_hbm.at[pl.ds(idx_sm[2 * r + 1], 1)], sem).start()
        return carry

    lax.fori_loop(0, tm, issue, 0, unroll=8)
    for _ in range(TOP_K):
        pltpu.make_async_copy(hbuf, xb_hbm.at[pl.ds(0, tm)], sem).wait()

    @pl.when(i == pl.num_programs(0) - 1)
    def _():
        zbuf[...] = jnp.zeros_like(zbuf)
        n_blocks = xb_hbm.shape[0] // MOE_BLOCK

        def pad_copies(e):
            start = tbl_sm[e]
            end = tbl_sm[N_EXPERTS + e]
            n1 = (-start) & (SUBLANES - 1)
            a0 = start + n1
            l8 = end - a0
            out = []
            for r in range(SUBLANES - 1):
                out.append((r < n1, pltpu.make_async_copy(zbuf.at[pl.ds(r, 1)], xb_hbm.at[pl.ds(start + r, 1)], sem)))
            sz = MOE_BLOCK // 2
            while sz >= SUBLANES:
                off = pl.multiple_of(a0 + (l8 & ~(2 * sz - 1)), SUBLANES)
                out.append(((l8 & sz) != 0,
                            pltpu.make_async_copy(zbuf.at[pl.ds(0, sz)], xb_hbm.at[pl.ds(off, sz)], sem)))
                sz //= 2
            return out

        def tail_copies():
            nvalid = tbl_sm[2 * N_EXPERTS]
            return [(bi >= nvalid,
                     pltpu.make_async_copy(zbuf, xb_hbm.at[pl.ds(bi * MOE_BLOCK, MOE_BLOCK)], sem))
                    for bi in range(first_tail_block, n_blocks)]

        def start_all(e, carry):
            for pred, cp in pad_copies(e):
                pl.when(pred)(cp.start)
            return carry

        def wait_all(e, carry):
            for pred, cp in pad_copies(e):
                pl.when(pred)(cp.wait)
            return carry

        lax.fori_loop(0, N_EXPERTS, start_all, 0)
        for pred, cp in tail_copies():
            pl.when(pred)(cp.start)
        lax.fori_loop(0, N_EXPERTS, wait_all, 0)
        for pred, cp in tail_copies():
            pl.when(pred)(cp.wait)


def _expert_kernel(be_sm, nv_sm, x_ref, wa_ref, wb_ref, wo_ref, o_ref, hbuf, acc):
    i = pl.program_id(0)
    j = pl.program_id(1)

    @pl.when((i >= nv_sm[0]) & (j == pl.num_programs(1) - 1))
    def _():
        o_ref[...] = jnp.zeros_like(o_ref)

    @pl.when(i < nv_sm[0])
    def _():
        @pl.when(j == 0)
        def _():
            hbuf[...] = x_ref[...].astype(BF16)
            acc[...] = jnp.zeros_like(acc)

        h = hbuf[...]
        a = _dot(h, wa_ref[0].astype(BF16))
        b = _dot(h, wb_ref[0].astype(BF16))
        acc[...] += _dot((_silu(a) * b).astype(BF16), wo_ref[0].astype(BF16))

        @pl.when(j == pl.num_programs(1) - 1)
        def _():
            o_ref[...] = acc[...]


def _combine_kernel(x_ref, gate_ref, meta_ref, g_ref, b_ref, dest_hbm, yb_hbm, o_ref,
                    y0, y1, idx_sm, sem, isem):
    tm = x_ref.shape[1]
    i = pl.program_id(0)
    icp = pltpu.make_async_copy(dest_hbm.at[i], idx_sm, isem)
    icp.start()
    icp.wait()

    def issue(r, carry):
        pltpu.make_async_copy(yb_hbm.at[pl.ds(idx_sm[2 * r], 1)], y0.at[pl.ds(r, 1)], sem).start()
        pltpu.make_async_copy(yb_hbm.at[pl.ds(idx_sm[2 * r + 1], 1)], y1.at[pl.ds(r, 1)], sem).start()
        return carry

    lax.fori_loop(0, tm, issue, 0, unroll=8)
    for buf in (y0, y1):
        pltpu.make_async_copy(yb_hbm.at[pl.ds(0, tm)], buf, sem).wait()
    meta = meta_ref[...]
    y = meta[:, 2:3] * y0[...] + meta[:, 3:4] * y1[...]
    o_ref[0] = _res_ln(x_ref[0], y, gate_ref[0], g_ref[...], b_ref[...])


def _moe_layer(x, sc, sh, gate, ln_g, ln_b, router, w_in, w_out, tm=256, th=512):
    bsz, s, d = x.shape
    n = bsz * s
    nt = n // tm
    spt = s // tm
    hid = w_out.shape[1]
    nj = hid // th
    min_blocks = n * TOP_K // MOE_BLOCK
    nb = min_blocks + N_EXPERTS

    row1 = pl.BlockSpec((1, tm, d), lambda i: (i // spt, i % spt, 0))
    mod1 = pl.BlockSpec((1, 1, d), lambda i: (i // spt, 0, 0))
    slab1 = pl.BlockSpec((tm, LANES), lambda i: (i, 0))
    w_r = jnp.concatenate([router, jnp.zeros((d, LANES - N_EXPERTS), F32)], 1)
    meta, counts = pl.pallas_call(
        _router_kernel,
        grid=(nt,),
        in_specs=[row1, mod1, mod1, pl.BlockSpec((d, LANES), lambda i: (0, 0))],
        out_specs=[slab1, pl.BlockSpec((1, LANES), lambda i: (0, 0))],
        out_shape=[jax.ShapeDtypeStruct((n, LANES), F32), jax.ShapeDtypeStruct((1, LANES), F32)],
        scratch_shapes=[pltpu.VMEM((1, LANES), F32)],
        compiler_params=_cparams(("arbitrary",)),
        name="moe_router",
    )(x, sc, sh, w_r)

    cnt = counts[0, :N_EXPERTS].astype(jnp.int32)
    nblk = (cnt + MOE_BLOCK - 1) // MOE_BLOCK
    ends = jnp.cumsum(nblk)
    first_row = (ends - nblk) * MOE_BLOCK
    nvalid = ends[-1]
    eid = meta[:, :TOP_K].astype(jnp.int32)
    rank = meta[:, 4:4 + TOP_K].astype(jnp.int32)
    base = jnp.sum(jnp.where(eid[:, :, None] == jnp.arange(N_EXPERTS), first_row, 0), -1)
    dest = (base + rank).reshape(nt, TOP_K * tm)
    tbl = jnp.concatenate([first_row + cnt, ends * MOE_BLOCK, nvalid.reshape(1)]).astype(jnp.int32)
    bi = jnp.minimum(jnp.arange(nb, dtype=jnp.int32), nvalid - 1)
    blk_e = jnp.sum(bi[:, None] >= ends[None, :], -1).astype(jnp.int32)

    row1p = pl.BlockSpec((1, tm, d), lambda i, t: (i // spt, i % spt, 0))
    mod1p = pl.BlockSpec((1, 1, d), lambda i, t: (i // spt, 0, 0))
    xb = pl.pallas_call(
        functools.partial(_dispatch_kernel, first_tail_block=min_blocks),
        grid_spec=pltpu.PrefetchScalarGridSpec(
            num_scalar_prefetch=1,
            grid=(nt,),
            in_specs=[row1p, mod1p, mod1p, pl.BlockSpec(memory_space=pl.ANY)],
            out_specs=pl.BlockSpec(memory_space=pl.ANY),
            scratch_shapes=[pltpu.VMEM((tm, d), F32), pltpu.VMEM((MOE_BLOCK, d), F32),
                            pltpu.SMEM((TOP_K * tm,), jnp.int32),
                            pltpu.SemaphoreType.DMA, pltpu.SemaphoreType.DMA]),
        out_shape=jax.ShapeDtypeStruct((nb * MOE_BLOCK, d), F32),
        compiler_params=_cparams(("arbitrary",)),
        name="moe_dispatch",
    )(tbl, x, sc, sh, dest)

    last = nj - 1

    def jsel(i, j, nv):
        return jnp.where(i < nv[0], j, last)

    yb = pl.pallas_call(
        _expert_kernel,
        grid_spec=pltpu.PrefetchScalarGridSpec(
            num_scalar_prefetch=2,
            grid=(nb, nj),
            in_specs=[pl.BlockSpec((MOE_BLOCK, d), lambda i, j, be, nv: (i, 0)),
                      pl.BlockSpec((1, d, th), lambda i, j, be, nv: (be[i], 0, jsel(i, j, nv))),
                      pl.BlockSpec((1, d, th), lambda i, j, be, nv: (be[i], 0, jsel(i, j, nv) + nj)),
                      pl.BlockSpec((1, th, d), lambda i, j, be, nv: (be[i], jsel(i, j, nv), 0))],
            out_specs=pl.BlockSpec((MOE_BLOCK, d), lambda i, j, be, nv: (i, 0)),
            scratch_shapes=[pltpu.VMEM((MOE_BLOCK, d), BF16), pltpu.VMEM((MOE_BLOCK, d), F32)]),
        out_shape=jax.ShapeDtypeStruct((nb * MOE_BLOCK, d), F32),
        compiler_params=_cparams(("arbitrary", "arbitrary")),
        name="moe_experts",
    )(blk_e, nvalid.reshape(1).astype(jnp.int32), xb, w_in, w_in, w_out)

    return pl.pallas_call(
        _combine_kernel,
        grid=(nt,),
        in_specs=[row1, mod1, slab1, pl.BlockSpec((1, d), lambda i: (0, 0)), pl.BlockSpec((1, d), lambda i: (0, 0)),
                  pl.BlockSpec(memory_space=pl.ANY), pl.BlockSpec(memory_space=pl.ANY)],
        out_specs=row1,
        out_shape=jax.ShapeDtypeStruct(x.shape, F32),
        scratch_shapes=[pltpu.VMEM((tm, d), F32), pltpu.VMEM((tm, d), F32),
                        pltpu.SMEM((TOP_K * tm,), jnp.int32),
                        pltpu.SemaphoreType.DMA, pltpu.SemaphoreType.DMA],
        compiler_params=_cparams(("arbitrary",)),
        name="moe_combine",
    )(x, gate, meta, ln_g, ln_b, dest, yb)


def kernel(x, c, positions, ada_w, ada_b, ln_g, ln_b, conv_w_in, conv_dw, conv_dw_b, conv_ln_g, conv_ln_b, conv_w_out, sgu_w_in, sgu_b_in, sgu_ln_g, sgu_ln_b, sgu_w_s, sgu_b_s, sgu_w_out, gdn_w_in, gdn_conv, gdn_a_log, gdn_dt_bias, gdn_norm_g, gdn_w_out, mla_w_in, mla_q_norm_g, mla_kv_norm_g, mla_w_uq, mla_w_ukv, mla_w_out, ffn_w_in, ffn_w_out, moe_router, moe_w_in, moe_w_out):
    bsz, s, d = x.shape
    mod = _ada_mod(c, ada_w, ada_b).reshape(DEPTH, bsz, 6, 1, d)
    for i in range(DEPTH):
        sh1, sc1, g1, sh2, sc2, g2 = (mod[i, :, t] for t in range(6))
        lg1, lb1 = ln_g[i, 0].reshape(1, d), ln_b[i, 0].reshape(1, d)
        lg2, lb2 = ln_g[i, 1].reshape(1, d), ln_b[i, 1].reshape(1, d)
        j = i // 4
        mixer = i % 4
        if mixer == 0:
            x = _conformer_layer(x, sc1, sh1, g1, lg1, lb1, conv_w_in[j], conv_dw[j], conv_dw_b[j],
                                 conv_ln_g[j], conv_ln_b[j], conv_w_out[j])
        elif mixer == 1:
            x = _sgu_layer(x, sc1, sh1, g1, lg1, lb1, sgu_w_in[j], sgu_b_in[j], sgu_ln_g[j], sgu_ln_b[j],
                           sgu_w_s[j], sgu_b_s[j], sgu_w_out[j])
        elif mixer == 2:
            x = _gdn_layer(x, sc1, sh1, g1, lg1, lb1, gdn_w_in[j], gdn_conv[j], gdn_a_log[j], gdn_dt_bias[j],
                           gdn_norm_g[j], gdn_w_out[j])
        else:
            x = _mla_layer(x, positions, sc1, sh1, g1, lg1, lb1, mla_w_in[j], mla_q_norm_g[j],
                           mla_kv_norm_g[j], mla_w_uq[j], mla_w_ukv[j], mla_w_out[j])
        if i % 2 == 0:
            x = _swiglu_layer(x, sc2, sh2, g2, lg2, lb2, ffn_w_in[i // 2], ffn_w_out[i // 2])
        else:
            x = _moe_layer(x, sc2, sh2, g2, lg2, lb2, moe_router[i // 2], moe_w_in[i // 2], moe_w_out[i // 2])
    return x
```

```python
import functools
import math

import jax
import jax.numpy as jnp
from jax import lax
from jax.experimental import pallas as pl
from jax.experimental.pallas import tpu as pltpu

F32 = jnp.float32
BF16 = jnp.bfloat16
HI = lax.Precision.HIGHEST

D_MODEL = 1024
DEPTH = 4
ALPHA = (2 * DEPTH) ** 0.25

CONV_WIDTH = 31
CONV_HALO = 32
CONV_ROWS = 128
SGU_CHUNK = 128
SGU_GROUPS = 8
SGU_HALF = 2 * D_MODEL
GDN_HEADS = 8
GDN_DK = 128
GDN_DV = 128
GDN_CONV = 4
GDN_TILE = 256
GDN_TAIL = 8
MLA_HEADS = 8
MLA_Q_RANK = 512
MLA_KV_RANK = 256
MLA_NOPE = 128
MLA_ROPE = 64
MLA_V = 128
ROPE_THETA = 10000.0
FFN_HIDDEN = 7 * D_MODEL // 2
N_EXPERTS = 8
TOP_K = 2
LANES = 128
SUBLANES = 8
MOE_BLOCK = 1024
FFN_ROW_SPLIT = 2
VMEM_LIMIT = 56 * 1024 * 1024


def _cparams(sem):
    return pltpu.CompilerParams(dimension_semantics=sem, vmem_limit_bytes=VMEM_LIMIT)


def _sigmoid(x):
    return 1.0 / (1.0 + jnp.exp(-x))


def _silu(x):
    return x * _sigmoid(x)


def _ln(z, g, b, eps=1e-5):
    mu = jnp.mean(z, -1, keepdims=True)
    zc = z - mu
    var = jnp.mean(zc * zc, -1, keepdims=True)
    return zc * lax.rsqrt(var + eps) * g + b


def _res_ln(x, y, gate, g, b):
    return _ln(ALPHA * x + (1.0 + gate) * y, g, b)


def _dot(a, b):
    return jnp.dot(a, b, preferred_element_type=F32)


def _dot_nt(a, b):
    return lax.dot_general(a, b, (((1,), (1,)), ((), ())), preferred_element_type=F32)


def _dot_tn(a, b):
    return lax.dot_general(a, b, (((0,), (0,)), ((), ())), preferred_element_type=F32)


def _dot_hi(a, b):
    return jnp.dot(a, b, precision=HI, preferred_element_type=F32)


def _split_bf16(a, parts):
    out = []
    for _ in range(parts):
        piece = a.astype(BF16)
        out.append(piece)
        a = a - piece.astype(F32)
    return out


def _dot3(a, b):
    (ah, al), (bh, bl) = _split_bf16(a, 2), _split_bf16(b, 2)
    return _dot(ah, bh) + (_dot(ah, bl) + _dot(al, bh))


def _ada_kernel(c_ref, w_ref, b_ref, o_ref):
    cond = _silu(c_ref[...])
    o_ref[0] = _dot_hi(cond, w_ref[0]) + b_ref[0]


def _ada_mod(c, ada_w, ada_b):
    bsz, d = c.shape
    n_out = ada_w.shape[-1]
    tn = 1024
    return pl.pallas_call(
        _ada_kernel,
        grid=(DEPTH, n_out // tn),
        in_specs=[
            pl.BlockSpec((bsz, d), lambda i, j: (0, 0)),
            pl.BlockSpec((1, d, tn), lambda i, j: (i, 0, j)),
            pl.BlockSpec((1, 1, tn), lambda i, j: (i, 0, j)),
        ],
        out_specs=pl.BlockSpec((1, bsz, tn), lambda i, j: (i, 0, j)),
        out_shape=jax.ShapeDtypeStruct((DEPTH, bsz, n_out), F32),
        compiler_params=_cparams(("parallel", "parallel")),
        name="ada_mod",
    )(c, ada_w, ada_b.reshape(DEPTH, 1, n_out))


def _row_spec(tm, width):
    return pl.BlockSpec((1, tm, width), lambda b, i: (b, i, 0))


def _mod_spec(width=D_MODEL):
    return pl.BlockSpec((1, 1, width), lambda b, i: (b, 0, 0))


def _full_spec(shape):
    nd = len(shape)
    return pl.BlockSpec(shape, lambda b, i: (0,) * nd)


def _outproj_kernel(a_ref, w_ref, x_ref, gate_ref, g_ref, b_ref, o_ref):
    y = _dot(a_ref[0], w_ref[...])
    o_ref[0] = _res_ln(x_ref[0], y, gate_ref[0], g_ref[...], b_ref[...])


def _outproj_ln(a, w, x, gate, ln_g, ln_b, tm=512):
    bsz, s, k = a.shape
    d = x.shape[-1]
    return pl.pallas_call(
        _outproj_kernel,
        grid=(bsz, s // tm),
        in_specs=[_row_spec(tm, k), _full_spec((k, d)), _row_spec(tm, d), _mod_spec(d),
                  _full_spec((1, d)), _full_spec((1, d))],
        out_specs=_row_spec(tm, d),
        out_shape=jax.ShapeDtypeStruct(x.shape, F32),
        compiler_params=_cparams(("parallel", "parallel")),
        name="outproj_ln",
    )(a, w, x, gate, ln_g, ln_b)


def _conv_in_kernel(x_ref, sc_ref, sh_ref, w_ref, o_ref):
    h = (x_ref[0] * (1.0 + sc_ref[0]) + sh_ref[0]).astype(BF16)
    ag = _dot(h, w_ref[...])
    d = o_ref.shape[-1]
    o_ref[0] = ag[:, :d] * _sigmoid(ag[:, d:])


def _conv_mid_kernel(y_ref, halo_ref, dw_ref, dwb_ref, cg_ref, cb_ref, w_ref,
                     x_ref, gate_ref, g_ref, b_ref, o_ref, ybuf, cbuf, shbuf):
    tm = y_ref.shape[1]
    d = y_ref.shape[2]
    first = pl.program_id(1) == 0
    ybuf[0:CONV_HALO, :] = jnp.where(first, 0.0, halo_ref[0])
    ybuf[CONV_HALO:, :] = y_ref[0]
    rc = CONV_ROWS
    off = CONV_HALO - (CONV_WIDTH - 1)
    def col_block(ci, carry):
        cols = pl.ds(pl.multiple_of(ci * LANES, LANES), LANES)
        for r0 in range(0, tm, rc):
            acc = None
            for r in range(SUBLANES):
                ks = [k for k in range(CONV_WIDTH) if (off + k) % SUBLANES == r]
                span = max((off + k) // SUBLANES for k in ks) * SUBLANES + rc
                shbuf[r, 0:span, :] = ybuf[r0 + r:r0 + r + span, cols]
                for k in ks:
                    q8 = (off + k) // SUBLANES * SUBLANES
                    term = dw_ref[k:k + 1, cols] * shbuf[r, q8:q8 + rc, :]
                    acc = term if acc is None else acc + term
            cbuf[r0:r0 + rc, cols] = acc
        return carry

    lax.fori_loop(0, d // LANES, col_block, 0)
    yc = cbuf[...] + dwb_ref[...]
    yn = _silu(_ln(yc, cg_ref[...], cb_ref[...]))
    yo = _dot(yn.astype(BF16), w_ref[...])
    o_ref[0] = _res_ln(x_ref[0], yo, gate_ref[0], g_ref[...], b_ref[...])


def _conformer_layer(x, sc, sh, gate, ln_g, ln_b, w_in, dw, dw_b, cg, cb, w_out, tm=512):
    bsz, s, d = x.shape
    y = pl.pallas_call(
        _conv_in_kernel,
        grid=(bsz, s // tm),
        in_specs=[_row_spec(tm, d), _mod_spec(), _mod_spec(), _full_spec((d, 2 * d))],
        out_specs=_row_spec(tm, d),
        out_shape=jax.ShapeDtypeStruct((bsz, s, d), F32),
        compiler_params=_cparams(("parallel", "parallel")),
        name="conv_in",
    )(x, sc, sh, w_in.astype(BF16))
    hb = tm // CONV_HALO
    halo_spec = pl.BlockSpec((1, CONV_HALO, d), lambda b, i: (b, jnp.maximum(i * hb - 1, 0), 0))
    dw_pad = jnp.concatenate([dw, jnp.zeros((1, d), F32)], 0)
    return pl.pallas_call(
        _conv_mid_kernel,
        grid=(bsz, s // tm),
        in_specs=[_row_spec(tm, d), halo_spec, _full_spec((CONV_WIDTH + 1, d)), _full_spec((1, d)),
                  _full_spec((1, d)), _full_spec((1, d)), _full_spec((d, d)),
                  _row_spec(tm, d), _mod_spec(), _full_spec((1, d)), _full_spec((1, d))],
        out_specs=_row_spec(tm, d),
        out_shape=jax.ShapeDtypeStruct((bsz, s, d), F32),
        scratch_shapes=[pltpu.VMEM((tm + CONV_HALO, d), F32), pltpu.VMEM((tm, d), F32),
                        pltpu.VMEM((SUBLANES, CONV_ROWS + CONV_HALO, LANES), F32)],
        compiler_params=_cparams(("parallel", "parallel")),
        name="conv_mid",
    )(y, y, dw_pad, dw_b.reshape(1, d), cg.reshape(1, d), cb.reshape(1, d), w_out.astype(BF16),
      x, gate, ln_g, ln_b)


def _swiglu_chunk(hbuf, wa, wb, wo, acc):
    rows = hbuf.shape[0] // FFN_ROW_SPLIT
    groups = [slice(g * rows, (g + 1) * rows) for g in range(FFN_ROW_SPLIT)]
    ab = [(_dot(hbuf[rs, :], wa), _dot(hbuf[rs, :], wb)) for rs in groups]
    acts = [(_silu(a) * b).astype(BF16) for a, b in ab]
    for rs, act in zip(groups, acts):
        acc[rs, :] += _dot(act, wo)


def _swiglu_kernel(x_ref, sc_ref, sh_ref, wa_ref, wb_ref, wo_ref, gate_ref, g_ref, b_ref,
                   o_ref, hbuf, acc):
    j = pl.program_id(2)

    @pl.when(j == 0)
    def _():
        hbuf[...] = (x_ref[0] * (1.0 + sc_ref[0]) + sh_ref[0]).astype(BF16)
        acc[...] = jnp.zeros_like(acc)

    _swiglu_chunk(hbuf, wa_ref[...], wb_ref[...], wo_ref[...], acc)

    @pl.when(j == pl.num_programs(2) - 1)
    def _():
        o_ref[0] = _res_ln(x_ref[0], acc[...], gate_ref[0], g_ref[...], b_ref[...])


def _swiglu_layer(x, sc, sh, gate, ln_g, ln_b, w_in, w_out, tm=512, th=1792):
    bsz, s, d = x.shape
    hid = w_out.shape[0]
    nj = hid // th
    row = pl.BlockSpec((1, tm, d), lambda b, i, j: (b, i, 0))
    mod = pl.BlockSpec((1, 1, d), lambda b, i, j: (b, 0, 0))
    vec = pl.BlockSpec((1, d), lambda b, i, j: (0, 0))
    w_in_b = w_in.astype(BF16)
    return pl.pallas_call(
        _swiglu_kernel,
        grid=(bsz, s // tm, nj),
        in_specs=[row, mod, mod,
                  pl.BlockSpec((d, th), lambda b, i, j: (0, j)),
                  pl.BlockSpec((d, th), lambda b, i, j: (0, j + nj)),
                  pl.BlockSpec((th, d), lambda b, i, j: (j, 0)),
                  mod, vec, vec],
        out_specs=row,
        out_shape=jax.ShapeDtypeStruct(x.shape, F32),
        scratch_shapes=[pltpu.VMEM((tm, d), BF16), pltpu.VMEM((tm, d), F32)],
        compiler_params=_cparams(("parallel", "parallel", "arbitrary")),
        name="swiglu",
    )(x, sc, sh, w_in_b, w_in_b, w_out.astype(BF16), gate, ln_g, ln_b)


def _sgu_kernel(x_ref, sc_ref, sh_ref, wi_ref, bi_ref, vg_ref, vb_ref, ws_ref, bs_ref, wo_ref,
                gate_ref, g_ref, b_ref, o_ref, gbuf):
    tm = x_ref.shape[1]
    h = (x_ref[0] * (1.0 + sc_ref[0]) + sh_ref[0]).astype(BF16)
    z = _dot(h, wi_ref[...]) + bi_ref[...]
    z = 0.5 * z * (1.0 + lax.erf(z * (2.0 ** -0.5)))
    u = z[:, :SGU_HALF]
    v = _ln(z[:, SGU_HALF:], vg_ref[...], vb_ref[...]).astype(BF16)
    gw = SGU_HALF // SGU_GROUPS
    ri = lax.broadcasted_iota(jnp.int32, (SGU_CHUNK, SGU_CHUNK), 0)
    ci = lax.broadcasted_iota(jnp.int32, (SGU_CHUNK, SGU_CHUNK), 1)
    causal = ci <= ri
    for g in range(SGU_GROUPS):
        wsg = jnp.where(causal, ws_ref[g], 0.0).astype(BF16)
        bias = bs_ref[:, g:g + 1]
        for c in range(tm // SGU_CHUNK):
            rows = slice(c * SGU_CHUNK, (c + 1) * SGU_CHUNK)
            cols = slice(g * gw, (g + 1) * gw)
            sv = _dot(wsg, v[rows, cols]) + bias
            gbuf[rows, cols] = (u[rows, cols] * sv).astype(BF16)
    yo = _dot(gbuf[...], wo_ref[...])
    o_ref[0] = _res_ln(x_ref[0], yo, gate_ref[0], g_ref[...], b_ref[...])


def _sgu_layer(x, sc, sh, gate, ln_g, ln_b, w_in, b_in, vg, vb, w_s, b_s, w_out, tm=512):
    bsz, s, d = x.shape
    return pl.pallas_call(
        _sgu_kernel,
        grid=(bsz, s // tm),
        in_specs=[_row_spec(tm, d), _mod_spec(), _mod_spec(),
                  _full_spec((d, 2 * SGU_HALF)), _full_spec((1, 2 * SGU_HALF)),
                  _full_spec((1, SGU_HALF)), _full_spec((1, SGU_HALF)),
                  _full_spec((SGU_GROUPS, SGU_CHUNK, SGU_CHUNK)), _full_spec((SGU_CHUNK, SGU_GROUPS)),
                  _full_spec((SGU_HALF, d)), _mod_spec(), _full_spec((1, d)), _full_spec((1, d))],
        out_specs=_row_spec(tm, d),
        out_shape=jax.ShapeDtypeStruct(x.shape, F32),
        scratch_shapes=[pltpu.VMEM((tm, SGU_HALF), BF16)],
        compiler_params=_cparams(("parallel", "parallel")),
        name="sgu",
    )(x, sc, sh, w_in.astype(BF16), b_in.reshape(1, -1), vg.reshape(1, -1), vb.reshape(1, -1),
      w_s, b_s.T, w_out.astype(BF16), gate, ln_g, ln_b)


def _gdn_in_kernel(x_ref, sc_ref, sh_ref, wq_ref, wz_ref, wba_ref, alog_ref, dtb_ref,
                   qkv_ref, z_ref, bg_ref, gct_ref):
    tm = x_ref.shape[1]
    c = GDN_TILE
    hf = x_ref[0] * (1.0 + sc_ref[0]) + sh_ref[0]
    h = hf.astype(BF16)
    qkv_ref[0] = _dot(h, wq_ref[...])
    z_ref[0] = _dot(h, wz_ref[...]).astype(BF16)
    ba = _dot3(hf, wba_ref[...])
    bg_ref[0, :, :LANES] = _sigmoid(ba)
    a_in = ba + dtb_ref[...]
    softplus = jnp.maximum(a_in, 0.0) + jnp.log1p(jnp.exp(-jnp.abs(a_in)))
    g_all = -jnp.exp(alog_ref[...]) * softplus
    ri = lax.broadcasted_iota(jnp.int32, (c, c), 0)
    ci = lax.broadcasted_iota(jnp.int32, (c, c), 1)
    tril = (ci <= ri).astype(BF16)
    for r0 in range(0, tm, c):
        g_hi, g_mid, g_lo = _split_bf16(g_all[r0:r0 + c], 3)
        gc = _dot(tril, g_hi) + (_dot(tril, g_mid) + _dot(tril, g_lo))
        bg_ref[0, r0:r0 + c, LANES:] = gc
        gct_ref[0, :, r0:r0 + c] = gc.T[GDN_HEADS:2 * GDN_HEADS]


def _gdn_chunk_kernel(q_ref, k_ref, v_ref, cwq_ref, cwk_ref, cwv_ref, bg_ref, gct_ref,
                      z_ref, ng_ref, o_ref, tail, cbuf, state, *, hp):
    c = q_ref.shape[1]

    @pl.when(pl.program_id(2) == 0)
    def _():
        tail[...] = jnp.zeros_like(tail)
        state[...] = jnp.zeros_like(state)

    off = GDN_TAIL - (GDN_CONV - 1)
    convd = []
    for idx, (ref, cw) in enumerate(((q_ref, cwq_ref), (k_ref, cwk_ref), (v_ref, cwv_ref))):
        cbuf[idx, 0:GDN_TAIL, :] = tail[idx]
        cbuf[idx, GDN_TAIL:, :] = ref[0]
        tail[idx] = ref[0, c - GDN_TAIL:c, :]
        acc = cw[0:1, :] * cbuf[idx, pl.ds(off, c), :]
        for t in range(1, GDN_CONV):
            acc = acc + cw[t:t + 1, :] * cbuf[idx, pl.ds(off + t, c), :]
        convd.append(_silu(acc))
    qc_all, kc_all, v_all = convd

    lane = lax.broadcasted_iota(jnp.int32, (1, LANES), 1)
    sub = lax.broadcasted_iota(jnp.int32, (GDN_HEADS, 1), 0)
    ri = lax.broadcasted_iota(jnp.int32, (c, c), 0)
    ci = lax.broadcasted_iota(jnp.int32, (c, c), 1)
    causal = ci <= ri
    strict = ci < ri
    xor = ri ^ ci
    eye = (ri == ci).astype(F32)
    bg = bg_ref[0]
    gct = gct_ref[0]
    z_all = z_ref[0]
    heads = range(hp)
    q, k, v, kb, beta, gc, dmat, a_mat = ([None] * hp for _ in range(8))
    for t in heads:
        hd = pl.program_id(1) * hp + t
        cols = slice(t * LANES, (t + 1) * LANES)
        qc, kc, v[t] = qc_all[:, cols], kc_all[:, cols], v_all[:, cols]
        q[t] = qc * lax.rsqrt(jnp.sum(qc * qc, -1, keepdims=True) + 1e-6) * (GDN_DK ** -0.5)
        k[t] = kc * lax.rsqrt(jnp.sum(kc * kc, -1, keepdims=True) + 1e-6)
        sel = lane == hd
        beta[t] = jnp.sum(jnp.where(sel, bg[:, :LANES], 0.0), -1, keepdims=True)
        gc[t] = jnp.sum(jnp.where(lane == GDN_HEADS + hd, bg[:, LANES:], 0.0), -1, keepdims=True)
        gc_row = jnp.sum(jnp.where(sub == hd, gct, 0.0), 0, keepdims=True)
        dmat[t] = jnp.where(causal, jnp.exp(jnp.where(causal, gc[t] - gc_row, 0.0)), 0.0)
        kb[t] = k[t].astype(BF16)
    for t in heads:
        a_mat[t] = jnp.where(strict, beta[t] * _dot_nt(kb[t], kb[t]) * dmat[t], 0.0)
    inv = [eye - jnp.where(xor == 1, a_mat[t], 0.0) for t in heads]
    lvl = 1
    while (1 << lvl) < c:
        invb = [inv[t].astype(BF16) for t in heads]
        joins = (xor >> lvl) == 1
        dm = [_dot(invb[t], jnp.where(joins, a_mat[t], 0.0).astype(BF16)) for t in heads]
        inv = [inv[t] - _dot(dm[t].astype(BF16), invb[t]) for t in heads]
        lvl += 1

    egc = [jnp.exp(gc[t]) for t in heads]
    sol = [_dot(inv[t].astype(BF16),
                jnp.concatenate([v[t] * beta[t], k[t] * (beta[t] * egc[t])], -1).astype(BF16)) for t in heads]
    qk = [jnp.where(causal, _dot_nt(q[t].astype(BF16), kb[t]) * dmat[t], 0.0).astype(BF16) for t in heads]
    s_prev = [state[t] for t in heads]
    sb = [s_prev[t].astype(BF16) for t in heads]
    vnb = [(sol[t][:, :GDN_DV] - _dot(sol[t][:, GDN_DV:].astype(BF16), sb[t])).astype(BF16) for t in heads]
    o = [_dot((q[t] * egc[t]).astype(BF16), sb[t]) + _dot(qk[t], vnb[t]) for t in heads]
    for t in heads:
        g_last = gc[t][c - 1:c, :]
        k_dec = (k[t] * jnp.exp(g_last - gc[t])).astype(BF16)
        state[t] = s_prev[t] * jnp.exp(g_last) + _dot_tn(k_dec, vnb[t])
    for t in heads:
        cols = slice(t * LANES, (t + 1) * LANES)
        on = o[t] * lax.rsqrt(jnp.mean(o[t] * o[t], -1, keepdims=True) + 1e-6) * ng_ref[...]
        o_ref[0, :, cols] = (on * _silu(z_all[:, cols].astype(F32))).astype(BF16)


def _gdn_layer(x, sc, sh, gate, ln_g, ln_b, w_in, conv_w, a_log, dt_bias, norm_g, w_out, tm=512, hp=4):
    bsz, s, d = x.shape
    nh = GDN_HEADS
    nqkv = 3 * nh * GDN_DK
    w_qkv = w_in[:, :nqkv].astype(BF16)
    w_z = w_in[:, nqkv:nqkv + nh * GDN_DV].astype(BF16)
    w_b = w_in[:, nqkv + nh * GDN_DV:nqkv + nh * GDN_DV + nh]
    w_a = w_in[:, nqkv + nh * GDN_DV + nh:]
    w_ba = jnp.concatenate([w_b, w_a, jnp.zeros((d, LANES - 2 * nh), F32)], 1)
    head_pad, lane_pad = jnp.zeros((nh,), F32), jnp.zeros((LANES - 2 * nh,), F32)
    alog = jnp.concatenate([head_pad, a_log, lane_pad]).reshape(1, LANES)
    dtb = jnp.concatenate([head_pad, dt_bias, lane_pad]).reshape(1, LANES)
    qkv, z, bg, gct = pl.pallas_call(
        _gdn_in_kernel,
        grid=(bsz, s // tm),
        in_specs=[_row_spec(tm, d), _mod_spec(), _mod_spec(), _full_spec((d, nqkv)),
                  _full_spec((d, nh * GDN_DV)), _full_spec((d, LANES)),
                  _full_spec((1, LANES)), _full_spec((1, LANES))],
        out_specs=[_row_spec(tm, nqkv), _row_spec(tm, nh * GDN_DV), _row_spec(tm, 2 * LANES),
                   pl.BlockSpec((1, nh, tm), lambda b, i: (b, 0, i))],
        out_shape=[jax.ShapeDtypeStruct((bsz, s, nqkv), F32),
                   jax.ShapeDtypeStruct((bsz, s, nh * GDN_DV), BF16),
                   jax.ShapeDtypeStruct((bsz, s, 2 * LANES), F32),
                   jax.ShapeDtypeStruct((bsz, nh, s), F32)],
        compiler_params=_cparams(("parallel", "parallel")),
        name="gdn_in",
    )(x, sc, sh, w_qkv, w_z, w_ba, alog, dtb)

    c = GDN_TILE
    wide = hp * LANES
    ng = nh // hp

    def col(base):
        return pl.BlockSpec((1, c, wide), lambda b, h, i: (b, i, base + h))

    def cw(base):
        return pl.BlockSpec((GDN_CONV, wide), lambda b, h, i: (0, base + h))

    og = pl.pallas_call(
        functools.partial(_gdn_chunk_kernel, hp=hp),
        grid=(bsz, ng, s // c),
        in_specs=[col(0), col(ng), col(2 * ng), cw(0), cw(ng), cw(2 * ng),
                  pl.BlockSpec((1, c, 2 * LANES), lambda b, h, i: (b, i, 0)),
                  pl.BlockSpec((1, nh, c), lambda b, h, i: (b, 0, i)),
                  col(0), pl.BlockSpec((1, LANES), lambda b, h, i: (0, 0))],
        out_specs=col(0),
        out_shape=jax.ShapeDtypeStruct((bsz, s, nh * GDN_DV), BF16),
        scratch_shapes=[pltpu.VMEM((3, GDN_TAIL, wide), F32), pltpu.VMEM((3, c + GDN_TAIL, wide), F32),
                        pltpu.VMEM((hp, GDN_DK, GDN_DV), F32)],
        compiler_params=_cparams(("parallel", "parallel", "arbitrary")),
        name="gdn_chunk",
    )(qkv, qkv, qkv, conv_w, conv_w, conv_w, bg, gct, z, norm_g.reshape(1, LANES))
    return _outproj_ln(og, w_out.astype(BF16), x, gate, ln_g, ln_b)


def _mla_in_kernel(x_ref, sc_ref, sh_ref, pos_ref, wi_ref, qg_ref, kg_ref, wqn_ref, wqr_ref, wqt_ref,
                   wkv_ref, qn_ref, qr_ref, kn_ref, kr_ref, v_ref):
    h = (x_ref[0] * (1.0 + sc_ref[0]) + sh_ref[0]).astype(BF16)
    p = _dot(h, wi_ref[...])
    cq = p[:, :MLA_Q_RANK]
    ckv = p[:, MLA_Q_RANK:MLA_Q_RANK + MLA_KV_RANK]
    kr = p[:, MLA_Q_RANK + MLA_KV_RANK:MLA_Q_RANK + MLA_KV_RANK + LANES]
    krt = p[:, MLA_Q_RANK + MLA_KV_RANK + LANES:]
    cqn = (cq * lax.rsqrt(jnp.mean(cq * cq, -1, keepdims=True) + 1e-6) * qg_ref[...]).astype(BF16)
    ckn = (ckv * lax.rsqrt(jnp.mean(ckv * ckv, -1, keepdims=True) + 1e-6) * kg_ref[...]).astype(BF16)
    lane = lax.broadcasted_iota(jnp.int32, (1, LANES), 1)
    half = MLA_ROPE // 2
    fidx = (lane % half).astype(F32)
    inv_freq = jnp.exp(fidx * (-math.log(ROPE_THETA) / half))
    ang = pos_ref[0].astype(F32) * inv_freq
    live = lane < MLA_ROPE
    cos = jnp.where(live, jnp.cos(ang), 0.0)
    sin = jnp.where(live, jnp.sin(ang), 0.0)
    scale = (MLA_NOPE + MLA_ROPE) ** -0.5 * math.log2(math.e)
    qn_ref[0] = (_dot(cqn, wqn_ref[...]) * scale).astype(BF16)
    qr = _dot(cqn, wqr_ref[...])
    qrt = _dot(cqn, wqt_ref[...])
    for hd in range(MLA_HEADS):
        cs = slice(hd * LANES, (hd + 1) * LANES)
        qr_ref[0, :, cs] = ((qr[:, cs] * cos + qrt[:, cs] * sin) * scale).astype(BF16)
    kr_ref[0] = (kr * cos + krt * sin).astype(BF16)
    kv = _dot(ckn, wkv_ref[...])
    nk = MLA_HEADS * MLA_NOPE
    kn_ref[0] = kv[:, :nk].astype(BF16)
    ones = jnp.ones((kv.shape[0], LANES), BF16)
    for hd in range(MLA_HEADS):
        v_ref[0, :, 2 * hd * LANES:(2 * hd + 1) * LANES] = kv[:, nk + hd * MLA_V:nk + (hd + 1) * MLA_V].astype(BF16)
        v_ref[0, :, (2 * hd + 1) * LANES:(2 * hd + 2) * LANES] = ones


def _mla_attn_kernel(qn_ref, qr_ref, kn_ref, kr_ref, v_ref, o_ref, qbuf, m_sc, acc_sc, *, hp):
    tq = qn_ref.shape[1]
    tk = tq
    i = pl.program_id(2)
    for h in range(hp):
        cols = slice(h * LANES, (h + 1) * LANES)
        qbuf[h, :, :LANES] = qn_ref[0, :, cols]
        qbuf[h, :, LANES:] = qr_ref[0, :, cols]
    m_sc[...] = jnp.full_like(m_sc, -jnp.inf)
    acc_sc[...] = jnp.zeros_like(acc_sc)

    def step(j, masked):
        r0 = pl.multiple_of(j * tk, tk)
        krt = kr_ref[0, pl.ds(r0, tk), :]
        if masked:
            ri = lax.broadcasted_iota(jnp.int32, (tq, tk), 0)
            ci = lax.broadcasted_iota(jnp.int32, (tq, tk), 1)
            keep = ci <= ri
        scs, ps, alphas = [], [], []
        for h in range(hp):
            cols = slice(h * LANES, (h + 1) * LANES)
            kt = jnp.concatenate([kn_ref[0, pl.ds(r0, tk), cols], krt], -1)
            scs.append(_dot_nt(qbuf[h], kt))
        for h in range(hp):
            sc = jnp.where(keep, scs[h], -jnp.inf) if masked else scs[h]
            m_prev = m_sc[h]
            m_new = jnp.maximum(m_prev, jnp.max(sc, -1, keepdims=True))
            alphas.append(jnp.exp2(m_prev - m_new))
            m_sc[h] = m_new
            ps.append(jnp.concatenate(
                [jnp.exp2(sc[:, c0:c0 + LANES] - m_new) for c0 in range(0, tk, LANES)], -1).astype(BF16))
        for h in range(hp):
            vcols = slice(2 * h * LANES, (2 * h + 2) * LANES)
            alpha2 = jnp.concatenate([alphas[h], alphas[h]], -1)
            acc_sc[h] = alpha2 * acc_sc[h] + _dot(ps[h], v_ref[0, pl.ds(r0, tk), vcols])

    def body(j, carry):
        step(j, False)
        return carry

    lax.fori_loop(0, i, body, 0)
    step(i, True)
    for h in range(hp):
        o_ref[0, :, h * LANES:(h + 1) * LANES] = (acc_sc[h, :, :LANES] / acc_sc[h, :, LANES:]).astype(BF16)


def _mla_layer(x, positions, sc, sh, gate, ln_g, ln_b, w_in, q_norm_g, kv_norm_g, w_uq, w_ukv, w_out,
               tm=512, tq=512, hp=4):
    bsz, s, d = x.shape
    nh = MLA_HEADS
    half = MLA_ROPE // 2
    rot = jnp.concatenate([jnp.arange(half, MLA_ROPE), jnp.arange(half)])
    sign = jnp.concatenate([-jnp.ones((half,), F32), jnp.ones((half,), F32)])
    zpad = jnp.zeros((d, LANES - MLA_ROPE), F32)
    w_kr = w_in[:, MLA_Q_RANK + MLA_KV_RANK:]
    w_in_ext = jnp.concatenate(
        [w_in[:, :MLA_Q_RANK + MLA_KV_RANK], w_kr, zpad, w_kr[:, rot] * sign, zpad], 1).astype(BF16)
    wq = w_uq.reshape(MLA_Q_RANK, nh, MLA_NOPE + MLA_ROPE)
    w_qn = wq[:, :, :MLA_NOPE].reshape(MLA_Q_RANK, nh * MLA_NOPE).astype(BF16)
    wq_r = wq[:, :, MLA_NOPE:]
    hpad = jnp.zeros((MLA_Q_RANK, nh, LANES - MLA_ROPE), F32)
    w_qr = jnp.concatenate([wq_r, hpad], -1).reshape(MLA_Q_RANK, nh * LANES).astype(BF16)
    w_qt = jnp.concatenate([wq_r[:, :, rot] * sign, hpad], -1).reshape(MLA_Q_RANK, nh * LANES).astype(BF16)
    wkv = w_ukv.reshape(MLA_KV_RANK, nh, MLA_NOPE + MLA_V)
    w_kv = jnp.concatenate([wkv[:, :, :MLA_NOPE].reshape(MLA_KV_RANK, nh * MLA_NOPE),
                            wkv[:, :, MLA_NOPE:].reshape(MLA_KV_RANK, nh * MLA_V)], 1).astype(BF16)
    n_in = w_in_ext.shape[1]
    wide = nh * LANES
    qn, qr, kn, kr, v = pl.pallas_call(
        _mla_in_kernel,
        grid=(bsz, s // tm),
        in_specs=[_row_spec(tm, d), _mod_spec(), _mod_spec(), _row_spec(tm, 1),
                  _full_spec((d, n_in)), _full_spec((1, MLA_Q_RANK)), _full_spec((1, MLA_KV_RANK)),
                  _full_spec((MLA_Q_RANK, wide)), _full_spec((MLA_Q_RANK, wide)),
                  _full_spec((MLA_Q_RANK, wide)), _full_spec((MLA_KV_RANK, 2 * wide))],
        out_specs=[_row_spec(tm, wide), _row_spec(tm, wide), _row_spec(tm, wide), _row_spec(tm, LANES),
                   _row_spec(tm, 2 * wide)],
        out_shape=[jax.ShapeDtypeStruct((bsz, s, wide), BF16), jax.ShapeDtypeStruct((bsz, s, wide), BF16),
                   jax.ShapeDtypeStruct((bsz, s, wide), BF16), jax.ShapeDtypeStruct((bsz, s, LANES), BF16),
                   jax.ShapeDtypeStruct((bsz, s, 2 * wide), BF16)],
        compiler_params=_cparams(("parallel", "parallel")),
        name="mla_in",
    )(x, sc, sh, positions.reshape(bsz, s, 1), w_in_ext, q_norm_g.reshape(1, -1), kv_norm_g.reshape(1, -1),
      w_qn, w_qr, w_qt, w_kv)

    gw = hp * LANES
    qspec = pl.BlockSpec((1, tq, gw), lambda b, h, i: (b, i, h))
    kspec = pl.BlockSpec((1, s, gw), lambda b, h, i: (b, 0, h))
    o = pl.pallas_call(
        functools.partial(_mla_attn_kernel, hp=hp),
        grid=(bsz, nh // hp, s // tq),
        in_specs=[qspec, qspec, kspec, pl.BlockSpec((1, s, LANES), lambda b, h, i: (b, 0, 0)),
                  pl.BlockSpec((1, s, 2 * gw), lambda b, h, i: (b, 0, h))],
        out_specs=qspec,
        out_shape=jax.ShapeDtypeStruct((bsz, s, wide), BF16),
        scratch_shapes=[pltpu.VMEM((hp, tq, 2 * LANES), BF16), pltpu.VMEM((hp, tq, LANES), F32),
                        pltpu.VMEM((hp, tq, 2 * MLA_V), F32)],
        compiler_params=_cparams(("parallel", "parallel", "arbitrary")),
        name="mla_attn",
    )(qn, qr, kn, kr, v)
    return _outproj_ln(o, w_out.astype(BF16), x, gate, ln_g, ln_b)


def _router_kernel(x_ref, sc_ref, sh_ref, wr_ref, meta_ref, cnt_ref, run):
    tm = x_ref.shape[1]

    @pl.when(pl.program_id(0) == 0)
    def _():
        run[...] = jnp.zeros_like(run)

    h = x_ref[0] * (1.0 + sc_ref[0]) + sh_ref[0]
    lane = lax.broadcasted_iota(jnp.int32, (tm, LANES), 1)
    lane_f = lane.astype(F32)
    logits = jnp.where(lane < N_EXPERTS, _dot3(h, wr_ref[...]), -jnp.inf)
    m1 = jnp.max(logits, -1, keepdims=True)
    i1 = jnp.min(jnp.where(logits == m1, lane_f, float(LANES)), -1, keepdims=True)
    oh1 = lane_f == i1
    rest = jnp.where(oh1, -jnp.inf, logits)
    m2 = jnp.max(rest, -1, keepdims=True)
    i2 = jnp.min(jnp.where(rest == m2, lane_f, float(LANES)), -1, keepdims=True)
    oh2 = lane_f == i2
    e21 = jnp.exp(m2 - m1)
    g1 = 1.0 / (1.0 + e21)
    g2 = e21 / (1.0 + e21)
    cnt = oh1.astype(F32) + oh2.astype(F32)
    ri = lax.broadcasted_iota(jnp.int32, (tm, tm), 0)
    ci = lax.broadcasted_iota(jnp.int32, (tm, tm), 1)
    before = _dot((ci < ri).astype(BF16), cnt.astype(BF16)) + run[...]
    r1 = jnp.sum(jnp.where(oh1, before, 0.0), -1, keepdims=True)
    r2 = jnp.sum(jnp.where(oh2, before, 0.0), -1, keepdims=True)
    meta = jnp.zeros((tm, LANES), F32)
    for k, col in enumerate((i1, i2, g1, g2, r1, r2)):
        meta = jnp.where(lane == k, col, meta)
    meta_ref[...] = meta
    run[...] = run[...] + jnp.sum(cnt, 0, keepdims=True)
    cnt_ref[...] = run[...]


def _dispatch_kernel(tbl_sm, x_ref, sc_ref, sh_ref, dest_hbm, xb_hbm, hbuf, zbuf, idx_sm, sem, isem, *,
                     first_tail_block):
    tm = x_ref.shape[1]
    i = pl.program_id(0)
    icp = pltpu.make_async_copy(dest_hbm.at[i], idx_sm, isem)
    icp.start()
    hbuf[...] = x_ref[0] * (1.0 + sc_ref[0]) + sh_ref[0]
    icp.wait()

    def issue(r, carry):
        pltpu.make_async_copy(hbuf.at[pl.ds(r, 1)], xb_hbm.at[pl.ds(idx_sm[2 * r], 1)], sem).start()
        pltpu.make_async_copy(hbuf.at[pl.ds(r, 1)], xb_hbm.at[pl.ds(idx_sm[2 * r + 1], 1)], sem).start()
        return carry

    lax.fori_loop(0, tm, issue, 0, unroll=8)
    for _ in range(TOP_K):
        pltpu.make_async_copy(hbuf, xb_hbm.at[pl.ds(0, tm)], sem).wait()

    @pl.when(i == pl.num_programs(0) - 1)
    def _():
        zbuf[...] = jnp.zeros_like(zbuf)
        n_blocks = xb_hbm.shape[0] // MOE_BLOCK

        def pad_copies(e):
            start = tbl_sm[e]
            end = tbl_sm[N_EXPERTS + e]
            n1 = (-start) & (SUBLANES - 1)
            a0 = start + n1
            l8 = end - a0
            out = []
            for r in range(SUBLANES - 1):
                out.append((r < n1, pltpu.make_async_copy(zbuf.at[pl.ds(r, 1)], xb_hbm.at[pl.ds(start + r, 1)], sem)))
            sz = MOE_BLOCK // 2
            while sz >= SUBLANES:
                off = pl.multiple_of(a0 + (l8 & ~(2 * sz - 1)), SUBLANES)
                out.append(((l8 & sz) != 0,
                            pltpu.make_async_copy(zbuf.at[pl.ds(0, sz)], xb_hbm.at[pl.ds(off, sz)], sem)))
                sz //= 2
            return out

        def tail_copies():
            nvalid = tbl_sm[2 * N_EXPERTS]
            return [(bi >= nvalid,
                     pltpu.make_async_copy(zbuf, xb_hbm.at[pl.ds(bi * MOE_BLOCK, MOE_BLOCK)], sem))
                    for bi in range(first_tail_block, n_blocks)]

        def start_all(e, carry):
            for pred, cp in pad_copies(e):
                pl.when(pred)(cp.start)
            return carry

        def wait_all(e, carry):
            for pred, cp in pad_copies(e):
                pl.when(pred)(cp.wait)
            return carry

        lax.fori_loop(0, N_EXPERTS, start_all, 0)
        for pred, cp in tail_copies():
            pl.when(pred)(cp.start)
        lax.fori_loop(0, N_EXPERTS, wait_all, 0)
        for pred, cp in tail_copies():
            pl.when(pred)(cp.wait)


def _expert_kernel(be_sm, nv_sm, x_ref, wa_ref, wb_ref, wo_ref, o_ref, hbuf, acc):
    i = pl.program_id(0)
    j = pl.program_id(1)

    @pl.when((i >= nv_sm[0]) & (j == pl.num_programs(1) - 1))
    def _():
        o_ref[...] = jnp.zeros_like(o_ref)

    @pl.when(i < nv_sm[0])
    def _():
        @pl.when(j == 0)
        def _():
            hbuf[...] = x_ref[...].astype(BF16)
            acc[...] = jnp.zeros_like(acc)

        _swiglu_chunk(hbuf, wa_ref[0, 0].astype(BF16), wb_ref[0, 0].astype(BF16), wo_ref[0, 0].astype(BF16), acc)

        @pl.when(j == pl.num_programs(1) - 1)
        def _():
            o_ref[...] = acc[...]


def _combine_kernel(x_ref, gate_ref, meta_ref, g_ref, b_ref, dest_hbm, yb_hbm, o_ref,
                    y0, y1, idx_sm, sem, isem):
    tm = x_ref.shape[1]
    i = pl.program_id(0)
    icp = pltpu.make_async_copy(dest_hbm.at[i], idx_sm, isem)
    icp.start()
    icp.wait()

    def issue(r, carry):
        pltpu.make_async_copy(yb_hbm.at[pl.ds(idx_sm[2 * r], 1)], y0.at[pl.ds(r, 1)], sem).start()
        pltpu.make_async_copy(yb_hbm.at[pl.ds(idx_sm[2 * r + 1], 1)], y1.at[pl.ds(r, 1)], sem).start()
        return carry

    lax.fori_loop(0, tm, issue, 0, unroll=8)
    for buf in (y0, y1):
        pltpu.make_async_copy(yb_hbm.at[pl.ds(0, tm)], buf, sem).wait()
    meta = meta_ref[...]
    y = meta[:, 2:3] * y0[...] + meta[:, 3:4] * y1[...]
    o_ref[0] = _res_ln(x_ref[0], y, gate_ref[0], g_ref[...], b_ref[...])


def _moe_layer(x, sc, sh, gate, ln_g, ln_b, router, w_in, w_out, layer, tm=256, tr=512, th=512):
    bsz, s, d = x.shape
    n = bsz * s
    nt = n // tm
    spt = s // tm
    hid = w_out.shape[-2]
    nj = hid // th
    min_blocks = n * TOP_K // MOE_BLOCK
    nb = min_blocks + N_EXPERTS
    tile = (d,)

    row1 = pl.BlockSpec((1, tm, d), lambda i: (i // spt, i % spt, 0))
    mod1 = pl.BlockSpec((1, 1, d), lambda i: (i // spt, 0, 0))
    slab1 = pl.BlockSpec((tm, LANES), lambda i: (i, 0))
    w_r = jnp.concatenate([router, jnp.zeros((d, LANES - N_EXPERTS), F32)], 1)
    rpt = s // tr
    meta, counts = pl.pallas_call(
        _router_kernel,
        grid=(n // tr,),
        in_specs=[pl.BlockSpec((1, tr, d), lambda i: (i // rpt, i % rpt, 0)),
                  pl.BlockSpec((1, 1, d), lambda i: (i // rpt, 0, 0)),
                  pl.BlockSpec((1, 1, d), lambda i: (i // rpt, 0, 0)),
                  pl.BlockSpec((d, LANES), lambda i: (0, 0))],
        out_specs=[pl.BlockSpec((tr, LANES), lambda i: (i, 0)), pl.BlockSpec((1, LANES), lambda i: (0, 0))],
        out_shape=[jax.ShapeDtypeStruct((n, LANES), F32), jax.ShapeDtypeStruct((1, LANES), F32)],
        scratch_shapes=[pltpu.VMEM((1, LANES), F32)],
        compiler_params=_cparams(("arbitrary",)),
        name="moe_router",
    )(x, sc, sh, w_r)

    cnt = counts[0, :N_EXPERTS].astype(jnp.int32)
    nblk = (cnt + MOE_BLOCK - 1) // MOE_BLOCK
    ends = jnp.cumsum(nblk)
    first_row = (ends - nblk) * MOE_BLOCK
    nvalid = ends[-1]
    eid = meta[:, :TOP_K].astype(jnp.int32)
    rank = meta[:, 4:4 + TOP_K].astype(jnp.int32)
    base = jnp.sum(jnp.where(eid[:, :, None] == jnp.arange(N_EXPERTS), first_row, 0), -1)
    dest = (base + rank).reshape(nt, TOP_K * tm)
    tbl = jnp.concatenate([first_row + cnt, ends * MOE_BLOCK, nvalid.reshape(1)]).astype(jnp.int32)
    bi = jnp.minimum(jnp.arange(nb, dtype=jnp.int32), nvalid - 1)
    blk_e = jnp.sum(bi[:, None] >= ends[None, :], -1).astype(jnp.int32)

    row1p = pl.BlockSpec((1, tm, d), lambda i, t: (i // spt, i % spt, 0))
    mod1p = pl.BlockSpec((1, 1, d), lambda i, t: (i // spt, 0, 0))
    xb = pl.pallas_call(
        functools.partial(_dispatch_kernel, first_tail_block=min_blocks),
        grid_spec=pltpu.PrefetchScalarGridSpec(
            num_scalar_prefetch=1,
            grid=(nt,),
            in_specs=[row1p, mod1p, mod1p, pl.BlockSpec(memory_space=pl.ANY)],
            out_specs=pl.BlockSpec(memory_space=pl.ANY),
            scratch_shapes=[pltpu.VMEM((tm,) + tile, F32), pltpu.VMEM((MOE_BLOCK,) + tile, F32),
                            pltpu.SMEM((TOP_K * tm,), jnp.int32),
                            pltpu.SemaphoreType.DMA, pltpu.SemaphoreType.DMA]),
        out_shape=jax.ShapeDtypeStruct((nb * MOE_BLOCK,) + tile, F32),
        compiler_params=_cparams(("arbitrary",)),
        name="moe_dispatch",
    )(tbl, x, sc, sh, dest)

    last = nj - 1

    def jsel(i, j, nv):
        return jnp.where(i < nv[0], j, last)

    yb = pl.pallas_call(
        _expert_kernel,
        grid_spec=pltpu.PrefetchScalarGridSpec(
            num_scalar_prefetch=2,
            grid=(nb, nj),
            in_specs=[pl.BlockSpec((MOE_BLOCK,) + tile, lambda i, j, be, nv: (i, 0)),
                      pl.BlockSpec((1, 1, d, th), lambda i, j, be, nv: (layer, be[i], 0, jsel(i, j, nv))),
                      pl.BlockSpec((1, 1, d, th), lambda i, j, be, nv: (layer, be[i], 0, jsel(i, j, nv) + nj)),
                      pl.BlockSpec((1, 1, th, d), lambda i, j, be, nv: (layer, be[i], jsel(i, j, nv), 0))],
            out_specs=pl.BlockSpec((MOE_BLOCK,) + tile, lambda i, j, be, nv: (i, 0)),
            scratch_shapes=[pltpu.VMEM((MOE_BLOCK, d), BF16), pltpu.VMEM((MOE_BLOCK, d), F32)]),
        out_shape=jax.ShapeDtypeStruct((nb * MOE_BLOCK,) + tile, F32),
        compiler_params=_cparams(("arbitrary", "arbitrary")),
        name="moe_experts",
    )(blk_e, nvalid.reshape(1).astype(jnp.int32), xb, w_in, w_in, w_out)

    return pl.pallas_call(
        _combine_kernel,
        grid=(nt,),
        in_specs=[row1, mod1, slab1, pl.BlockSpec((1, d), lambda i: (0, 0)), pl.BlockSpec((1, d), lambda i: (0, 0)),
                  pl.BlockSpec(memory_space=pl.ANY), pl.BlockSpec(memory_space=pl.ANY)],
        out_specs=row1,
        out_shape=jax.ShapeDtypeStruct(x.shape, F32),
        scratch_shapes=[pltpu.VMEM((tm,) + tile, F32), pltpu.VMEM((tm,) + tile, F32),
                        pltpu.SMEM((TOP_K * tm,), jnp.int32),
                        pltpu.SemaphoreType.DMA, pltpu.SemaphoreType.DMA],
        compiler_params=_cparams(("arbitrary",)),
        name="moe_combine",
    )(x, gate, meta, ln_g, ln_b, dest, yb)


def kernel(x, c, positions, ada_w, ada_b, ln_g, ln_b, conv_w_in, conv_dw, conv_dw_b, conv_ln_g, conv_ln_b, conv_w_out, sgu_w_in, sgu_b_in, sgu_ln_g, sgu_ln_b, sgu_w_s, sgu_b_s, sgu_w_out, gdn_w_in, gdn_conv, gdn_a_log, gdn_dt_bias, gdn_norm_g, gdn_w_out, mla_w_in, mla_q_norm_g, mla_kv_norm_g, mla_w_uq, mla_w_ukv, mla_w_out, ffn_w_in, ffn_w_out, moe_router, moe_w_in, moe_w_out):
    bsz, s, d = x.shape
    mod = _ada_mod(c, ada_w, ada_b).reshape(DEPTH, bsz, 6, 1, d)
    for i in range(DEPTH):
        sh1, sc1, g1, sh2, sc2, g2 = (mod[i, :, t] for t in range(6))
        lg1, lb1 = ln_g[i, 0].reshape(1, d), ln_b[i, 0].reshape(1, d)
        lg2, lb2 = ln_g[i, 1].reshape(1, d), ln_b[i, 1].reshape(1, d)
        j = i // 4
        mixer = i % 4
        if mixer == 0:
            x = _conformer_layer(x, sc1, sh1, g1, lg1, lb1, conv_w_in[j], conv_dw[j], conv_dw_b[j],
                                 conv_ln_g[j], conv_ln_b[j], conv_w_out[j])
        elif mixer == 1:
            x = _sgu_layer(x, sc1, sh1, g1, lg1, lb1, sgu_w_in[j], sgu_b_in[j], sgu_ln_g[j], sgu_ln_b[j],
                           sgu_w_s[j], sgu_b_s[j], sgu_w_out[j])
        elif mixer == 2:
            x = _gdn_layer(x, sc1, sh1, g1, lg1, lb1, gdn_w_in[j], gdn_conv[j], gdn_a_log[j], gdn_dt_bias[j],
                           gdn_norm_g[j], gdn_w_out[j])
        else:
            x = _mla_layer(x, positions, sc1, sh1, g1, lg1, lb1, mla_w_in[j], mla_q_norm_g[j],
                           mla_kv_norm_g[j], mla_w_uq[j], mla_w_ukv[j], mla_w_out[j])
        if i % 2 == 0:
            x = _swiglu_layer(x, sc2, sh2, g2, lg2, lb2, ffn_w_in[i // 2], ffn_w_out[i // 2])
        else:
            x = _moe_layer(x, sc2, sh2, g2, lg2, lb2, moe_router[i // 2], moe_w_in, moe_w_out, i // 2)
    return x
```

```python
import functools
import math

import jax
import jax.numpy as jnp
from jax import lax
from jax.experimental import pallas as pl
from jax.experimental.pallas import tpu as pltpu

F32 = jnp.float32
BF16 = jnp.bfloat16
HI = lax.Precision.HIGHEST

D_MODEL = 1024
DEPTH = 4
ALPHA = (2 * DEPTH) ** 0.25

CONV_WIDTH = 31
CONV_HALO = 32
CONV_ROWS = 128
SGU_CHUNK = 128
SGU_GROUPS = 8
SGU_HALF = 2 * D_MODEL
GDN_HEADS = 8
GDN_DK = 128
GDN_DV = 128
GDN_CONV = 4
GDN_TILE = 256
GDN_TAIL = 8
MLA_HEADS = 8
MLA_Q_RANK = 512
MLA_KV_RANK = 256
MLA_NOPE = 128
MLA_ROPE = 64
MLA_V = 128
ROPE_THETA = 10000.0
FFN_HIDDEN = 7 * D_MODEL // 2
N_EXPERTS = 8
TOP_K = 2
LANES = 128
SUBLANES = 8
MOE_BLOCK = 1024
FFN_ROW_SPLIT = 2
VMEM_LIMIT = 56 * 1024 * 1024


def _cparams(sem):
    return pltpu.CompilerParams(dimension_semantics=sem, vmem_limit_bytes=VMEM_LIMIT)


def _sigmoid(x):
    return 1.0 / (1.0 + jnp.exp(-x))


def _silu(x):
    return x * _sigmoid(x)


def _ln(z, g, b, eps=1e-5):
    mu = jnp.mean(z, -1, keepdims=True)
    zc = z - mu
    var = jnp.mean(zc * zc, -1, keepdims=True)
    return zc * lax.rsqrt(var + eps) * g + b


def _res_ln(x, y, gate, g, b):
    return _ln(ALPHA * x + (1.0 + gate) * y, g, b)


def _dot(a, b):
    return jnp.dot(a, b, preferred_element_type=F32)


def _dot_nt(a, b):
    return lax.dot_general(a, b, (((1,), (1,)), ((), ())), preferred_element_type=F32)


def _dot_tn(a, b):
    return lax.dot_general(a, b, (((0,), (0,)), ((), ())), preferred_element_type=F32)


def _dot_hi(a, b):
    return jnp.dot(a, b, precision=HI, preferred_element_type=F32)


def _split_bf16(a, parts):
    out = []
    for _ in range(parts):
        piece = a.astype(BF16)
        out.append(piece)
        a = a - piece.astype(F32)
    return out


def _dot3(a, b):
    (ah, al), (bh, bl) = _split_bf16(a, 2), _split_bf16(b, 2)
    return _dot(ah, bh) + (_dot(ah, bl) + _dot(al, bh))


def _ada_kernel(c_ref, w_ref, b_ref, o_ref):
    cond = _silu(c_ref[...])
    o_ref[0] = _dot_hi(cond, w_ref[0]) + b_ref[0]


def _ada_mod(c, ada_w, ada_b):
    bsz, d = c.shape
    n_out = ada_w.shape[-1]
    tn = 1024
    return pl.pallas_call(
        _ada_kernel,
        grid=(DEPTH, n_out // tn),
        in_specs=[
            pl.BlockSpec((bsz, d), lambda i, j: (0, 0)),
            pl.BlockSpec((1, d, tn), lambda i, j: (i, 0, j)),
            pl.BlockSpec((1, 1, tn), lambda i, j: (i, 0, j)),
        ],
        out_specs=pl.BlockSpec((1, bsz, tn), lambda i, j: (i, 0, j)),
        out_shape=jax.ShapeDtypeStruct((DEPTH, bsz, n_out), F32),
        compiler_params=_cparams(("parallel", "parallel")),
        name="ada_mod",
    )(c, ada_w, ada_b.reshape(DEPTH, 1, n_out))


def _row_spec(tm, width):
    return pl.BlockSpec((1, tm, width), lambda b, i: (b, i, 0))


def _mod_spec(width=D_MODEL):
    return pl.BlockSpec((1, 1, width), lambda b, i: (b, 0, 0))


def _full_spec(shape):
    nd = len(shape)
    return pl.BlockSpec(shape, lambda b, i: (0,) * nd)


def _outproj_kernel(a_ref, w_ref, x_ref, gate_ref, g_ref, b_ref, o_ref):
    y = _dot(a_ref[0], w_ref[...])
    o_ref[0] = _res_ln(x_ref[0], y, gate_ref[0], g_ref[...], b_ref[...])


def _outproj_ln(a, w, x, gate, ln_g, ln_b, tm=512):
    bsz, s, k = a.shape
    d = x.shape[-1]
    return pl.pallas_call(
        _outproj_kernel,
        grid=(bsz, s // tm),
        in_specs=[_row_spec(tm, k), _full_spec((k, d)), _row_spec(tm, d), _mod_spec(d),
                  _full_spec((1, d)), _full_spec((1, d))],
        out_specs=_row_spec(tm, d),
        out_shape=jax.ShapeDtypeStruct(x.shape, F32),
        compiler_params=_cparams(("parallel", "parallel")),
        name="outproj_ln",
    )(a, w, x, gate, ln_g, ln_b)


def _conv_in_kernel(x_ref, sc_ref, sh_ref, w_ref, o_ref):
    h = (x_ref[0] * (1.0 + sc_ref[0]) + sh_ref[0]).astype(BF16)
    ag = _dot(h, w_ref[...])
    d = o_ref.shape[-1]
    o_ref[0] = ag[:, :d] * _sigmoid(ag[:, d:])


def _conv_mid_kernel(y_ref, halo_ref, dw_ref, dwb_ref, cg_ref, cb_ref, w_ref,
                     x_ref, gate_ref, g_ref, b_ref, o_ref, ybuf, cbuf, shbuf):
    tm = y_ref.shape[1]
    d = y_ref.shape[2]
    first = pl.program_id(1) == 0
    ybuf[0:CONV_HALO, :] = jnp.where(first, 0.0, halo_ref[0])
    ybuf[CONV_HALO:, :] = y_ref[0]
    rc = CONV_ROWS
    off = CONV_HALO - (CONV_WIDTH - 1)
    def col_block(ci, carry):
        cols = pl.ds(pl.multiple_of(ci * LANES, LANES), LANES)
        for r0 in range(0, tm, rc):
            acc = None
            for r in range(SUBLANES):
                ks = [k for k in range(CONV_WIDTH) if (off + k) % SUBLANES == r]
                span = max((off + k) // SUBLANES for k in ks) * SUBLANES + rc
                shbuf[r, 0:span, :] = ybuf[r0 + r:r0 + r + span, cols]
                for k in ks:
                    q8 = (off + k) // SUBLANES * SUBLANES
                    term = dw_ref[k:k + 1, cols] * shbuf[r, q8:q8 + rc, :]
                    acc = term if acc is None else acc + term
            cbuf[r0:r0 + rc, cols] = acc
        return carry

    lax.fori_loop(0, d // LANES, col_block, 0)
    yc = cbuf[...] + dwb_ref[...]
    yn = _silu(_ln(yc, cg_ref[...], cb_ref[...]))
    yo = _dot(yn.astype(BF16), w_ref[...])
    o_ref[0] = _res_ln(x_ref[0], yo, gate_ref[0], g_ref[...], b_ref[...])


def _conformer_layer(x, sc, sh, gate, ln_g, ln_b, w_in, dw, dw_b, cg, cb, w_out, tm=512):
    bsz, s, d = x.shape
    y = pl.pallas_call(
        _conv_in_kernel,
        grid=(bsz, s // tm),
        in_specs=[_row_spec(tm, d), _mod_spec(), _mod_spec(), _full_spec((d, 2 * d))],
        out_specs=_row_spec(tm, d),
        out_shape=jax.ShapeDtypeStruct((bsz, s, d), F32),
        compiler_params=_cparams(("parallel", "parallel")),
        name="conv_in",
    )(x, sc, sh, w_in.astype(BF16))
    hb = tm // CONV_HALO
    halo_spec = pl.BlockSpec((1, CONV_HALO, d), lambda b, i: (b, jnp.maximum(i * hb - 1, 0), 0))
    dw_pad = jnp.concatenate([dw, jnp.zeros((1, d), F32)], 0)
    return pl.pallas_call(
        _conv_mid_kernel,
        grid=(bsz, s // tm),
        in_specs=[_row_spec(tm, d), halo_spec, _full_spec((CONV_WIDTH + 1, d)), _full_spec((1, d)),
                  _full_spec((1, d)), _full_spec((1, d)), _full_spec((d, d)),
                  _row_spec(tm, d), _mod_spec(), _full_spec((1, d)), _full_spec((1, d))],
        out_specs=_row_spec(tm, d),
        out_shape=jax.ShapeDtypeStruct((bsz, s, d), F32),
        scratch_shapes=[pltpu.VMEM((tm + CONV_HALO, d), F32), pltpu.VMEM((tm, d), F32),
                        pltpu.VMEM((SUBLANES, CONV_ROWS + CONV_HALO, LANES), F32)],
        compiler_params=_cparams(("parallel", "parallel")),
        name="conv_mid",
    )(y, y, dw_pad, dw_b.reshape(1, d), cg.reshape(1, d), cb.reshape(1, d), w_out.astype(BF16),
      x, gate, ln_g, ln_b)


def _swiglu_chunk(hbuf, wa, wb, wo, acc):
    rows = hbuf.shape[0] // FFN_ROW_SPLIT
    groups = [slice(g * rows, (g + 1) * rows) for g in range(FFN_ROW_SPLIT)]
    ab = [(_dot(hbuf[rs, :], wa), _dot(hbuf[rs, :], wb)) for rs in groups]
    acts = [(_silu(a) * b).astype(BF16) for a, b in ab]
    for rs, act in zip(groups, acts):
        acc[rs, :] += _dot(act, wo)


def _swiglu_kernel(x_ref, sc_ref, sh_ref, wa_ref, wb_ref, wo_ref, gate_ref, g_ref, b_ref,
                   o_ref, hbuf, acc):
    j = pl.program_id(2)

    @pl.when(j == 0)
    def _():
        hbuf[...] = (x_ref[0] * (1.0 + sc_ref[0]) + sh_ref[0]).astype(BF16)
        acc[...] = jnp.zeros_like(acc)

    _swiglu_chunk(hbuf, wa_ref[...], wb_ref[...], wo_ref[...], acc)

    @pl.when(j == pl.num_programs(2) - 1)
    def _():
        o_ref[0] = _res_ln(x_ref[0], acc[...], gate_ref[0], g_ref[...], b_ref[...])


def _swiglu_layer(x, sc, sh, gate, ln_g, ln_b, w_in, w_out, tm=512, th=1792):
    bsz, s, d = x.shape
    hid = w_out.shape[0]
    nj = hid // th
    row = pl.BlockSpec((1, tm, d), lambda b, i, j: (b, i, 0))
    mod = pl.BlockSpec((1, 1, d), lambda b, i, j: (b, 0, 0))
    vec = pl.BlockSpec((1, d), lambda b, i, j: (0, 0))
    w_in_b = w_in.astype(BF16)
    return pl.pallas_call(
        _swiglu_kernel,
        grid=(bsz, s // tm, nj),
        in_specs=[row, mod, mod,
                  pl.BlockSpec((d, th), lambda b, i, j: (0, j)),
                  pl.BlockSpec((d, th), lambda b, i, j: (0, j + nj)),
                  pl.BlockSpec((th, d), lambda b, i, j: (j, 0)),
                  mod, vec, vec],
        out_specs=row,
        out_shape=jax.ShapeDtypeStruct(x.shape, F32),
        scratch_shapes=[pltpu.VMEM((tm, d), BF16), pltpu.VMEM((tm, d), F32)],
        compiler_params=_cparams(("parallel", "parallel", "arbitrary")),
        name="swiglu",
    )(x, sc, sh, w_in_b, w_in_b, w_out.astype(BF16), gate, ln_g, ln_b)


def _sgu_kernel(x_ref, sc_ref, sh_ref, wi_ref, bi_ref, vg_ref, vb_ref, ws_ref, bs_ref, wo_ref,
                gate_ref, g_ref, b_ref, o_ref, gbuf):
    tm = x_ref.shape[1]
    h = (x_ref[0] * (1.0 + sc_ref[0]) + sh_ref[0]).astype(BF16)
    z = _dot(h, wi_ref[...]) + bi_ref[...]
    z = 0.5 * z * (1.0 + lax.erf(z * (2.0 ** -0.5)))
    u = z[:, :SGU_HALF]
    v = _ln(z[:, SGU_HALF:], vg_ref[...], vb_ref[...]).astype(BF16)
    gw = SGU_HALF // SGU_GROUPS
    ri = lax.broadcasted_iota(jnp.int32, (SGU_CHUNK, SGU_CHUNK), 0)
    ci = lax.broadcasted_iota(jnp.int32, (SGU_CHUNK, SGU_CHUNK), 1)
    causal = ci <= ri
    for g in range(SGU_GROUPS):
        wsg = jnp.where(causal, ws_ref[g], 0.0).astype(BF16)
        bias = bs_ref[:, g:g + 1]
        for c in range(tm // SGU_CHUNK):
            rows = slice(c * SGU_CHUNK, (c + 1) * SGU_CHUNK)
            cols = slice(g * gw, (g + 1) * gw)
            sv = _dot(wsg, v[rows, cols]) + bias
            gbuf[rows, cols] = (u[rows, cols] * sv).astype(BF16)
    yo = _dot(gbuf[...], wo_ref[...])
    o_ref[0] = _res_ln(x_ref[0], yo, gate_ref[0], g_ref[...], b_ref[...])


def _sgu_layer(x, sc, sh, gate, ln_g, ln_b, w_in, b_in, vg, vb, w_s, b_s, w_out, tm=512):
    bsz, s, d = x.shape
    return pl.pallas_call(
        _sgu_kernel,
        grid=(bsz, s // tm),
        in_specs=[_row_spec(tm, d), _mod_spec(), _mod_spec(),
                  _full_spec((d, 2 * SGU_HALF)), _full_spec((1, 2 * SGU_HALF)),
                  _full_spec((1, SGU_HALF)), _full_spec((1, SGU_HALF)),
                  _full_spec((SGU_GROUPS, SGU_CHUNK, SGU_CHUNK)), _full_spec((SGU_CHUNK, SGU_GROUPS)),
                  _full_spec((SGU_HALF, d)), _mod_spec(), _full_spec((1, d)), _full_spec((1, d))],
        out_specs=_row_spec(tm, d),
        out_shape=jax.ShapeDtypeStruct(x.shape, F32),
        scratch_shapes=[pltpu.VMEM((tm, SGU_HALF), BF16)],
        compiler_params=_cparams(("parallel", "parallel")),
        name="sgu",
    )(x, sc, sh, w_in.astype(BF16), b_in.reshape(1, -1), vg.reshape(1, -1), vb.reshape(1, -1),
      w_s, b_s.T, w_out.astype(BF16), gate, ln_g, ln_b)


def _gdn_in_kernel(x_ref, sc_ref, sh_ref, wq_ref, wz_ref, wba_ref, alog_ref, dtb_ref,
                   qkv_ref, z_ref, bg_ref, gct_ref):
    tm = x_ref.shape[1]
    c = GDN_TILE
    hf = x_ref[0] * (1.0 + sc_ref[0]) + sh_ref[0]
    h = hf.astype(BF16)
    qkv_ref[0] = _dot(h, wq_ref[...])
    z_ref[0] = _dot(h, wz_ref[...]).astype(BF16)
    ba = _dot3(hf, wba_ref[...])
    bg_ref[0, :, :LANES] = _sigmoid(ba)
    a_in = ba + dtb_ref[...]
    softplus = jnp.maximum(a_in, 0.0) + jnp.log1p(jnp.exp(-jnp.abs(a_in)))
    g_all = -jnp.exp(alog_ref[...]) * softplus
    ri = lax.broadcasted_iota(jnp.int32, (c, c), 0)
    ci = lax.broadcasted_iota(jnp.int32, (c, c), 1)
    tril = (ci <= ri).astype(BF16)
    for r0 in range(0, tm, c):
        g_hi, g_mid, g_lo = _split_bf16(g_all[r0:r0 + c], 3)
        gc = _dot(tril, g_hi) + (_dot(tril, g_mid) + _dot(tril, g_lo))
        bg_ref[0, r0:r0 + c, LANES:] = gc
        gct_ref[0, :, r0:r0 + c] = gc.T[GDN_HEADS:2 * GDN_HEADS]


def _gdn_chunk_kernel(q_ref, k_ref, v_ref, cwq_ref, cwk_ref, cwv_ref, bg_ref, gct_ref,
                      z_ref, ng_ref, o_ref, tail, cbuf, state, *, hp):
    c = q_ref.shape[1]

    @pl.when(pl.program_id(2) == 0)
    def _():
        tail[...] = jnp.zeros_like(tail)
        state[...] = jnp.zeros_like(state)

    off = GDN_TAIL - (GDN_CONV - 1)
    convd = []
    for idx, (ref, cw) in enumerate(((q_ref, cwq_ref), (k_ref, cwk_ref), (v_ref, cwv_ref))):
        cbuf[idx, 0:GDN_TAIL, :] = tail[idx]
        cbuf[idx, GDN_TAIL:, :] = ref[0]
        tail[idx] = ref[0, c - GDN_TAIL:c, :]
        acc = cw[0:1, :] * cbuf[idx, pl.ds(off, c), :]
        for t in range(1, GDN_CONV):
            acc = acc + cw[t:t + 1, :] * cbuf[idx, pl.ds(off + t, c), :]
        convd.append(_silu(acc))
    qc_all, kc_all, v_all = convd

    lane = lax.broadcasted_iota(jnp.int32, (1, LANES), 1)
    sub = lax.broadcasted_iota(jnp.int32, (GDN_HEADS, 1), 0)
    ri = lax.broadcasted_iota(jnp.int32, (c, c), 0)
    ci = lax.broadcasted_iota(jnp.int32, (c, c), 1)
    causal = ci <= ri
    strict = ci < ri
    xor = ri ^ ci
    eye = (ri == ci).astype(F32)
    bg = bg_ref[0]
    gct = gct_ref[0]
    z_all = z_ref[0]
    heads = range(hp)
    q, k, v, kb, beta, gc, dmat, a_mat = ([None] * hp for _ in range(8))
    for t in heads:
        hd = pl.program_id(1) * hp + t
        cols = slice(t * LANES, (t + 1) * LANES)
        qc, kc, v[t] = qc_all[:, cols], kc_all[:, cols], v_all[:, cols]
        q[t] = qc * lax.rsqrt(jnp.sum(qc * qc, -1, keepdims=True) + 1e-6) * (GDN_DK ** -0.5)
        k[t] = kc * lax.rsqrt(jnp.sum(kc * kc, -1, keepdims=True) + 1e-6)
        sel = lane == hd
        beta[t] = jnp.sum(jnp.where(sel, bg[:, :LANES], 0.0), -1, keepdims=True)
        gc[t] = jnp.sum(jnp.where(lane == GDN_HEADS + hd, bg[:, LANES:], 0.0), -1, keepdims=True)
        gc_row = jnp.sum(jnp.where(sub == hd, gct, 0.0), 0, keepdims=True)
        dmat[t] = jnp.where(causal, jnp.exp(jnp.where(causal, gc[t] - gc_row, 0.0)), 0.0)
        kb[t] = k[t].astype(BF16)
    for t in heads:
        a_mat[t] = jnp.where(strict, beta[t] * _dot_nt(kb[t], kb[t]) * dmat[t], 0.0)
    inv = [eye - jnp.where(xor == 1, a_mat[t], 0.0) for t in heads]
    lvl = 1
    while (1 << lvl) < c:
        invb = [inv[t].astype(BF16) for t in heads]
        joins = (xor >> lvl) == 1
        dm = [_dot(invb[t], jnp.where(joins, a_mat[t], 0.0).astype(BF16)) for t in heads]
        inv = [inv[t] - _dot(dm[t].astype(BF16), invb[t]) for t in heads]
        lvl += 1

    egc = [jnp.exp(gc[t]) for t in heads]
    sol = [_dot(inv[t].astype(BF16),
                jnp.concatenate([v[t] * beta[t], k[t] * (beta[t] * egc[t])], -1).astype(BF16)) for t in heads]
    qk = [jnp.where(causal, _dot_nt(q[t].astype(BF16), kb[t]) * dmat[t], 0.0).astype(BF16) for t in heads]
    s_prev = [state[t] for t in heads]
    sb = [s_prev[t].astype(BF16) for t in heads]
    vnb = [(sol[t][:, :GDN_DV] - _dot(sol[t][:, GDN_DV:].astype(BF16), sb[t])).astype(BF16) for t in heads]
    o = [_dot((q[t] * egc[t]).astype(BF16), sb[t]) + _dot(qk[t], vnb[t]) for t in heads]
    for t in heads:
        g_last = gc[t][c - 1:c, :]
        k_dec = (k[t] * jnp.exp(g_last - gc[t])).astype(BF16)
        state[t] = s_prev[t] * jnp.exp(g_last) + _dot_tn(k_dec, vnb[t])
    for t in heads:
        cols = slice(t * LANES, (t + 1) * LANES)
        on = o[t] * lax.rsqrt(jnp.mean(o[t] * o[t], -1, keepdims=True) + 1e-6) * ng_ref[...]
        o_ref[0, :, cols] = (on * _silu(z_all[:, cols].astype(F32))).astype(BF16)


def _gdn_layer(x, sc, sh, gate, ln_g, ln_b, w_in, conv_w, a_log, dt_bias, norm_g, w_out, tm=512, hp=4):
    bsz, s, d = x.shape
    nh = GDN_HEADS
    nqkv = 3 * nh * GDN_DK
    w_qkv = w_in[:, :nqkv].astype(BF16)
    w_z = w_in[:, nqkv:nqkv + nh * GDN_DV].astype(BF16)
    w_b = w_in[:, nqkv + nh * GDN_DV:nqkv + nh * GDN_DV + nh]
    w_a = w_in[:, nqkv + nh * GDN_DV + nh:]
    w_ba = jnp.concatenate([w_b, w_a, jnp.zeros((d, LANES - 2 * nh), F32)], 1)
    head_pad, lane_pad = jnp.zeros((nh,), F32), jnp.zeros((LANES - 2 * nh,), F32)
    alog = jnp.concatenate([head_pad, a_log, lane_pad]).reshape(1, LANES)
    dtb = jnp.concatenate([head_pad, dt_bias, lane_pad]).reshape(1, LANES)
    qkv, z, bg, gct = pl.pallas_call(
        _gdn_in_kernel,
        grid=(bsz, s // tm),
        in_specs=[_row_spec(tm, d), _mod_spec(), _mod_spec(), _full_spec((d, nqkv)),
                  _full_spec((d, nh * GDN_DV)), _full_spec((d, LANES)),
                  _full_spec((1, LANES)), _full_spec((1, LANES))],
        out_specs=[_row_spec(tm, nqkv), _row_spec(tm, nh * GDN_DV), _row_spec(tm, 2 * LANES),
                   pl.BlockSpec((1, nh, tm), lambda b, i: (b, 0, i))],
        out_shape=[jax.ShapeDtypeStruct((bsz, s, nqkv), F32),
                   jax.ShapeDtypeStruct((bsz, s, nh * GDN_DV), BF16),
                   jax.ShapeDtypeStruct((bsz, s, 2 * LANES), F32),
                   jax.ShapeDtypeStruct((bsz, nh, s), F32)],
        compiler_params=_cparams(("parallel", "parallel")),
        name="gdn_in",
    )(x, sc, sh, w_qkv, w_z, w_ba, alog, dtb)

    c = GDN_TILE
    wide = hp * LANES
    ng = nh // hp

    def col(base):
        return pl.BlockSpec((1, c, wide), lambda b, h, i: (b, i, base + h))

    def cw(base):
        return pl.BlockSpec((GDN_CONV, wide), lambda b, h, i: (0, base + h))

    og = pl.pallas_call(
        functools.partial(_gdn_chunk_kernel, hp=hp),
        grid=(bsz, ng, s // c),
        in_specs=[col(0), col(ng), col(2 * ng), cw(0), cw(ng), cw(2 * ng),
                  pl.BlockSpec((1, c, 2 * LANES), lambda b, h, i: (b, i, 0)),
                  pl.BlockSpec((1, nh, c), lambda b, h, i: (b, 0, i)),
                  col(0), pl.BlockSpec((1, LANES), lambda b, h, i: (0, 0))],
        out_specs=col(0),
        out_shape=jax.ShapeDtypeStruct((bsz, s, nh * GDN_DV), BF16),
        scratch_shapes=[pltpu.VMEM((3, GDN_TAIL, wide), F32), pltpu.VMEM((3, c + GDN_TAIL, wide), F32),
                        pltpu.VMEM((hp, GDN_DK, GDN_DV), F32)],
        compiler_params=_cparams(("parallel", "parallel", "arbitrary")),
        name="gdn_chunk",
    )(qkv, qkv, qkv, conv_w, conv_w, conv_w, bg, gct, z, norm_g.reshape(1, LANES))
    return _outproj_ln(og, w_out.astype(BF16), x, gate, ln_g, ln_b)


def _mla_in_kernel(x_ref, sc_ref, sh_ref, pos_ref, wi_ref, qg_ref, kg_ref, wqn_ref, wqr_ref, wqt_ref,
                   wkv_ref, qn_ref, qr_ref, kn_ref, kr_ref, v_ref):
    h = (x_ref[0] * (1.0 + sc_ref[0]) + sh_ref[0]).astype(BF16)
    p = _dot(h, wi_ref[...])
    cq = p[:, :MLA_Q_RANK]
    ckv = p[:, MLA_Q_RANK:MLA_Q_RANK + MLA_KV_RANK]
    kr = p[:, MLA_Q_RANK + MLA_KV_RANK:MLA_Q_RANK + MLA_KV_RANK + LANES]
    krt = p[:, MLA_Q_RANK + MLA_KV_RANK + LANES:]
    cqn = (cq * lax.rsqrt(jnp.mean(cq * cq, -1, keepdims=True) + 1e-6) * qg_ref[...]).astype(BF16)
    ckn = (ckv * lax.rsqrt(jnp.mean(ckv * ckv, -1, keepdims=True) + 1e-6) * kg_ref[...]).astype(BF16)
    lane = lax.broadcasted_iota(jnp.int32, (1, LANES), 1)
    half = MLA_ROPE // 2
    fidx = (lane % half).astype(F32)
    inv_freq = jnp.exp(fidx * (-math.log(ROPE_THETA) / half))
    ang = pos_ref[0].astype(F32) * inv_freq
    live = lane < MLA_ROPE
    cos = jnp.where(live, jnp.cos(ang), 0.0)
    sin = jnp.where(live, jnp.sin(ang), 0.0)
    scale = (MLA_NOPE + MLA_ROPE) ** -0.5 * math.log2(math.e)
    qn_ref[0] = (_dot(cqn, wqn_ref[...]) * scale).astype(BF16)
    qr = _dot(cqn, wqr_ref[...])
    qrt = _dot(cqn, wqt_ref[...])
    for hd in range(MLA_HEADS):
        cs = slice(hd * LANES, (hd + 1) * LANES)
        qr_ref[0, :, cs] = ((qr[:, cs] * cos + qrt[:, cs] * sin) * scale).astype(BF16)
    kr_ref[0] = (kr * cos + krt * sin).astype(BF16)
    kv = _dot(ckn, wkv_ref[...])
    nk = MLA_HEADS * MLA_NOPE
    kn_ref[0] = kv[:, :nk].astype(BF16)
    ones = jnp.ones((kv.shape[0], LANES), BF16)
    for hd in range(MLA_HEADS):
        v_ref[0, :, 2 * hd * LANES:(2 * hd + 1) * LANES] = kv[:, nk + hd * MLA_V:nk + (hd + 1) * MLA_V].astype(BF16)
        v_ref[0, :, (2 * hd + 1) * LANES:(2 * hd + 2) * LANES] = ones


def _mla_attn_kernel(qn_ref, qr_ref, kn_ref, kr_ref, v_ref, o_ref, qbuf, m_sc, acc_sc, *, hp):
    tq = qn_ref.shape[1]
    tk = tq
    i = pl.program_id(2)
    for h in range(hp):
        cols = slice(h * LANES, (h + 1) * LANES)
        qbuf[h, :, :LANES] = qn_ref[0, :, cols]
        qbuf[h, :, LANES:] = qr_ref[0, :, cols]
    m_sc[...] = jnp.full_like(m_sc, -jnp.inf)
    acc_sc[...] = jnp.zeros_like(acc_sc)

    def step(j, masked):
        r0 = pl.multiple_of(j * tk, tk)
        krt = kr_ref[0, pl.ds(r0, tk), :]
        if masked:
            ri = lax.broadcasted_iota(jnp.int32, (tq, tk), 0)
            ci = lax.broadcasted_iota(jnp.int32, (tq, tk), 1)
            keep = ci <= ri
        scs, ps, alphas = [], [], []
        for h in range(hp):
            cols = slice(h * LANES, (h + 1) * LANES)
            kt = jnp.concatenate([kn_ref[0, pl.ds(r0, tk), cols], krt], -1)
            scs.append(_dot_nt(qbuf[h], kt))
        for h in range(hp):
            sc = jnp.where(keep, scs[h], -jnp.inf) if masked else scs[h]
            m_prev = m_sc[h]
            m_new = jnp.maximum(m_prev, jnp.max(sc, -1, keepdims=True))
            alphas.append(jnp.exp2(m_prev - m_new))
            m_sc[h] = m_new
            ps.append(jnp.concatenate(
                [jnp.exp2(sc[:, c0:c0 + LANES] - m_new) for c0 in range(0, tk, LANES)], -1).astype(BF16))
        for h in range(hp):
            vcols = slice(2 * h * LANES, (2 * h + 2) * LANES)
            alpha2 = jnp.concatenate([alphas[h], alphas[h]], -1)
            acc_sc[h] = alpha2 * acc_sc[h] + _dot(ps[h], v_ref[0, pl.ds(r0, tk), vcols])

    def body(j, carry):
        step(j, False)
        return carry

    lax.fori_loop(0, i, body, 0)
    step(i, True)
    for h in range(hp):
        o_ref[0, :, h * LANES:(h + 1) * LANES] = (acc_sc[h, :, :LANES] / acc_sc[h, :, LANES:]).astype(BF16)


def _mla_layer(x, positions, sc, sh, gate, ln_g, ln_b, w_in, q_norm_g, kv_norm_g, w_uq, w_ukv, w_out,
               tm=512, tq=512, hp=4):
    bsz, s, d = x.shape
    nh = MLA_HEADS
    half = MLA_ROPE // 2
    rot = jnp.concatenate([jnp.arange(half, MLA_ROPE), jnp.arange(half)])
    sign = jnp.concatenate([-jnp.ones((half,), F32), jnp.ones((half,), F32)])
    zpad = jnp.zeros((d, LANES - MLA_ROPE), F32)
    w_kr = w_in[:, MLA_Q_RANK + MLA_KV_RANK:]
    w_in_ext = jnp.concatenate(
        [w_in[:, :MLA_Q_RANK + MLA_KV_RANK], w_kr, zpad, w_kr[:, rot] * sign, zpad], 1).astype(BF16)
    wq = w_uq.reshape(MLA_Q_RANK, nh, MLA_NOPE + MLA_ROPE)
    w_qn = wq[:, :, :MLA_NOPE].reshape(MLA_Q_RANK, nh * MLA_NOPE).astype(BF16)
    wq_r = wq[:, :, MLA_NOPE:]
    hpad = jnp.zeros((MLA_Q_RANK, nh, LANES - MLA_ROPE), F32)
    w_qr = jnp.concatenate([wq_r, hpad], -1).reshape(MLA_Q_RANK, nh * LANES).astype(BF16)
    w_qt = jnp.concatenate([wq_r[:, :, rot] * sign, hpad], -1).reshape(MLA_Q_RANK, nh * LANES).astype(BF16)
    wkv = w_ukv.reshape(MLA_KV_RANK, nh, MLA_NOPE + MLA_V)
    w_kv = jnp.concatenate([wkv[:, :, :MLA_NOPE].reshape(MLA_KV_RANK, nh * MLA_NOPE),
                            wkv[:, :, MLA_NOPE:].reshape(MLA_KV_RANK, nh * MLA_V)], 1).astype(BF16)
    n_in = w_in_ext.shape[1]
    wide = nh * LANES
    qn, qr, kn, kr, v = pl.pallas_call(
        _mla_in_kernel,
        grid=(bsz, s // tm),
        in_specs=[_row_spec(tm, d), _mod_spec(), _mod_spec(), _row_spec(tm, 1),
                  _full_spec((d, n_in)), _full_spec((1, MLA_Q_RANK)), _full_spec((1, MLA_KV_RANK)),
                  _full_spec((MLA_Q_RANK, wide)), _full_spec((MLA_Q_RANK, wide)),
                  _full_spec((MLA_Q_RANK, wide)), _full_spec((MLA_KV_RANK, 2 * wide))],
        out_specs=[_row_spec(tm, wide), _row_spec(tm, wide), _row_spec(tm, wide), _row_spec(tm, LANES),
                   _row_spec(tm, 2 * wide)],
        out_shape=[jax.ShapeDtypeStruct((bsz, s, wide), BF16), jax.ShapeDtypeStruct((bsz, s, wide), BF16),
                   jax.ShapeDtypeStruct((bsz, s, wide), BF16), jax.ShapeDtypeStruct((bsz, s, LANES), BF16),
                   jax.ShapeDtypeStruct((bsz, s, 2 * wide), BF16)],
        compiler_params=_cparams(("parallel", "parallel")),
        name="mla_in",
    )(x, sc, sh, positions.reshape(bsz, s, 1), w_in_ext, q_norm_g.reshape(1, -1), kv_norm_g.reshape(1, -1),
      w_qn, w_qr, w_qt, w_kv)

    gw = hp * LANES
    qspec = pl.BlockSpec((1, tq, gw), lambda b, h, i: (b, i, h))
    kspec = pl.BlockSpec((1, s, gw), lambda b, h, i: (b, 0, h))
    o = pl.pallas_call(
        functools.partial(_mla_attn_kernel, hp=hp),
        grid=(bsz, nh // hp, s // tq),
        in_specs=[qspec, qspec, kspec, pl.BlockSpec((1, s, LANES), lambda b, h, i: (b, 0, 0)),
                  pl.BlockSpec((1, s, 2 * gw), lambda b, h, i: (b, 0, h))],
        out_specs=qspec,
        out_shape=jax.ShapeDtypeStruct((bsz, s, wide), BF16),
        scratch_shapes=[pltpu.VMEM((hp, tq, 2 * LANES), BF16), pltpu.VMEM((hp, tq, LANES), F32),
                        pltpu.VMEM((hp, tq, 2 * MLA_V), F32)],
        compiler_params=_cparams(("parallel", "parallel", "arbitrary")),
        name="mla_attn",
    )(qn, qr, kn, kr, v)
    return _outproj_ln(o, w_out.astype(BF16), x, gate, ln_g, ln_b)


def _router_kernel(x_ref, sc_ref, sh_ref, wr_ref, meta_ref, cnt_ref, run):
    tm = x_ref.shape[1]

    @pl.when(pl.program_id(0) == 0)
    def _():
        run[...] = jnp.zeros_like(run)

    h = x_ref[0] * (1.0 + sc_ref[0]) + sh_ref[0]
    lane = lax.broadcasted_iota(jnp.int32, (tm, LANES), 1)
    lane_f = lane.astype(F32)
    logits = jnp.where(lane < N_EXPERTS, _dot3(h, wr_ref[...]), -jnp.inf)
    m1 = jnp.max(logits, -1, keepdims=True)
    i1 = jnp.min(jnp.where(logits == m1, lane_f, float(LANES)), -1, keepdims=True)
    oh1 = lane_f == i1
    rest = jnp.where(oh1, -jnp.inf, logits)
    m2 = jnp.max(rest, -1, keepdims=True)
    i2 = jnp.min(jnp.where(rest == m2, lane_f, float(LANES)), -1, keepdims=True)
    oh2 = lane_f == i2
    e21 = jnp.exp(m2 - m1)
    g1 = 1.0 / (1.0 + e21)
    g2 = e21 / (1.0 + e21)
    cnt = oh1.astype(F32) + oh2.astype(F32)
    ri = lax.broadcasted_iota(jnp.int32, (tm, tm), 0)
    ci = lax.broadcasted_iota(jnp.int32, (tm, tm), 1)
    before = _dot((ci < ri).astype(BF16), cnt.astype(BF16)) + run[...]
    r1 = jnp.sum(jnp.where(oh1, before, 0.0), -1, keepdims=True)
    r2 = jnp.sum(jnp.where(oh2, before, 0.0), -1, keepdims=True)
    meta = jnp.zeros((tm, LANES), F32)
    for k, col in enumerate((i1, i2, g1, g2, r1, r2)):
        meta = jnp.where(lane == k, col, meta)
    meta_ref[...] = meta
    run[...] = run[...] + jnp.sum(cnt, 0, keepdims=True)
    cnt_ref[...] = run[...]


def _dispatch_kernel(tbl_sm, x_ref, sc_ref, sh_ref, dest_hbm, xb_hbm, hbuf, zbuf, idx_sm, rsem, sem, isem, *,
                     first_tail_block):
    tm = x_ref.shape[1]
    i = pl.program_id(0)
    last = pl.num_programs(0) - 1
    slot = i % 2
    hslot = hbuf.at[slot]

    def wait_rows(s):
        for _ in range(TOP_K):
            pltpu.make_async_copy(hbuf.at[s], xb_hbm.at[pl.ds(0, tm)], rsem.at[s]).wait()

    icp = pltpu.make_async_copy(dest_hbm.at[i], idx_sm, isem)
    icp.start()

    @pl.when(i >= 2)
    def _():
        wait_rows(slot)

    hslot[...] = x_ref[0] * (1.0 + sc_ref[0]) + sh_ref[0]
    icp.wait()

    def issue(r, carry):
        pltpu.make_async_copy(hslot.at[pl.ds(r, 1)], xb_hbm.at[pl.ds(idx_sm[2 * r], 1)], rsem.at[slot]).start()
        pltpu.make_async_copy(hslot.at[pl.ds(r, 1)], xb_hbm.at[pl.ds(idx_sm[2 * r + 1], 1)], rsem.at[slot]).start()
        return carry

    lax.fori_loop(0, tm, issue, 0, unroll=8)

    @pl.when(i == last)
    def _():
        @pl.when(i >= 1)
        def _():
            wait_rows(1 - slot)

        wait_rows(slot)
        zbuf[...] = jnp.zeros_like(zbuf)
        n_blocks = xb_hbm.shape[0] // MOE_BLOCK

        def pad_copies(e):
            start = tbl_sm[e]
            end = tbl_sm[N_EXPERTS + e]
            n1 = (-start) & (SUBLANES - 1)
            a0 = start + n1
            l8 = end - a0
            out = []
            for r in range(SUBLANES - 1):
                out.append((r < n1, pltpu.make_async_copy(zbuf.at[pl.ds(r, 1)], xb_hbm.at[pl.ds(start + r, 1)], sem)))
            sz = MOE_BLOCK // 2
            while sz >= SUBLANES:
                off = pl.multiple_of(a0 + (l8 & ~(2 * sz - 1)), SUBLANES)
                out.append(((l8 & sz) != 0,
                            pltpu.make_async_copy(zbuf.at[pl.ds(0, sz)], xb_hbm.at[pl.ds(off, sz)], sem)))
                sz //= 2
            return out

        def tail_copies():
            nvalid = tbl_sm[2 * N_EXPERTS]
            return [(bi >= nvalid,
                     pltpu.make_async_copy(zbuf, xb_hbm.at[pl.ds(bi * MOE_BLOCK, MOE_BLOCK)], sem))
                    for bi in range(first_tail_block, n_blocks)]

        def start_all(e, carry):
            for pred, cp in pad_copies(e):
                pl.when(pred)(cp.start)
            return carry

        def wait_all(e, carry):
            for pred, cp in pad_copies(e):
                pl.when(pred)(cp.wait)
            return carry

        lax.fori_loop(0, N_EXPERTS, start_all, 0)
        for pred, cp in tail_copies():
            pl.when(pred)(cp.start)
        lax.fori_loop(0, N_EXPERTS, wait_all, 0)
        for pred, cp in tail_copies():
            pl.when(pred)(cp.wait)


def _expert_kernel(be_sm, nv_sm, x_ref, wa_ref, wb_ref, wo_ref, o_ref, hbuf, acc):
    i = pl.program_id(0)
    j = pl.program_id(1)

    @pl.when((i >= nv_sm[0]) & (j == pl.num_programs(1) - 1))
    def _():
        o_ref[...] = jnp.zeros_like(o_ref)

    @pl.when(i < nv_sm[0])
    def _():
        @pl.when(j == 0)
        def _():
            hbuf[...] = x_ref[...].astype(BF16)
            acc[...] = jnp.zeros_like(acc)

        _swiglu_chunk(hbuf, wa_ref[0, 0].astype(BF16), wb_ref[0, 0].astype(BF16), wo_ref[0, 0].astype(BF16), acc)

        @pl.when(j == pl.num_programs(1) - 1)
        def _():
            o_ref[...] = acc[...]


def _combine_kernel(x_ref, gate_ref, meta_ref, g_ref, b_ref, dest_hbm, yb_hbm, o_ref,
                    y0, y1, idx_sm, sem, isem):
    tm = x_ref.shape[1]
    i = pl.program_id(0)
    slot = i % 2

    def gather(tile, s):
        icp = pltpu.make_async_copy(dest_hbm.at[tile], idx_sm, isem)
        icp.start()
        icp.wait()
        d0, d1 = y0.at[s], y1.at[s]

        def issue(r, carry):
            pltpu.make_async_copy(yb_hbm.at[pl.ds(idx_sm[2 * r], 1)], d0.at[pl.ds(r, 1)], sem.at[s]).start()
            pltpu.make_async_copy(yb_hbm.at[pl.ds(idx_sm[2 * r + 1], 1)], d1.at[pl.ds(r, 1)], sem.at[s]).start()
            return carry

        lax.fori_loop(0, tm, issue, 0, unroll=8)

    @pl.when(i == 0)
    def _():
        gather(0, 0)

    @pl.when(i + 1 < pl.num_programs(0))
    def _():
        gather(i + 1, 1 - slot)

    for buf in (y0, y1):
        pltpu.make_async_copy(yb_hbm.at[pl.ds(0, tm)], buf.at[slot], sem.at[slot]).wait()
    meta = meta_ref[...]
    y = meta[:, 2:3] * y0[slot] + meta[:, 3:4] * y1[slot]
    o_ref[0] = _res_ln(x_ref[0], y, gate_ref[0], g_ref[...], b_ref[...])


def _moe_layer(x, sc, sh, gate, ln_g, ln_b, router, w_in, w_out, layer, tm=256, tr=512, th=512):
    bsz, s, d = x.shape
    n = bsz * s
    nt = n // tm
    spt = s // tm
    hid = w_out.shape[-2]
    nj = hid // th
    min_blocks = n * TOP_K // MOE_BLOCK
    nb = min_blocks + N_EXPERTS
    tile = (d,)

    row1 = pl.BlockSpec((1, tm, d), lambda i: (i // spt, i % spt, 0))
    mod1 = pl.BlockSpec((1, 1, d), lambda i: (i // spt, 0, 0))
    slab1 = pl.BlockSpec((tm, LANES), lambda i: (i, 0))
    w_r = jnp.concatenate([router, jnp.zeros((d, LANES - N_EXPERTS), F32)], 1)
    rpt = s // tr
    meta, counts = pl.pallas_call(
        _router_kernel,
        grid=(n // tr,),
        in_specs=[pl.BlockSpec((1, tr, d), lambda i: (i // rpt, i % rpt, 0)),
                  pl.BlockSpec((1, 1, d), lambda i: (i // rpt, 0, 0)),
                  pl.BlockSpec((1, 1, d), lambda i: (i // rpt, 0, 0)),
                  pl.BlockSpec((d, LANES), lambda i: (0, 0))],
        out_specs=[pl.BlockSpec((tr, LANES), lambda i: (i, 0)), pl.BlockSpec((1, LANES), lambda i: (0, 0))],
        out_shape=[jax.ShapeDtypeStruct((n, LANES), F32), jax.ShapeDtypeStruct((1, LANES), F32)],
        scratch_shapes=[pltpu.VMEM((1, LANES), F32)],
        compiler_params=_cparams(("arbitrary",)),
        name="moe_router",
    )(x, sc, sh, w_r)

    cnt = counts[0, :N_EXPERTS].astype(jnp.int32)
    nblk = (cnt + MOE_BLOCK - 1) // MOE_BLOCK
    ends = jnp.cumsum(nblk)
    first_row = (ends - nblk) * MOE_BLOCK
    nvalid = ends[-1]
    eid = meta[:, :TOP_K].astype(jnp.int32)
    rank = meta[:, 4:4 + TOP_K].astype(jnp.int32)
    base = jnp.sum(jnp.where(eid[:, :, None] == jnp.arange(N_EXPERTS), first_row, 0), -1)
    dest = (base + rank).reshape(nt, TOP_K * tm)
    tbl = jnp.concatenate([first_row + cnt, ends * MOE_BLOCK, nvalid.reshape(1)]).astype(jnp.int32)
    bi = jnp.minimum(jnp.arange(nb, dtype=jnp.int32), nvalid - 1)
    blk_e = jnp.sum(bi[:, None] >= ends[None, :], -1).astype(jnp.int32)

    row1p = pl.BlockSpec((1, tm, d), lambda i, t: (i // spt, i % spt, 0))
    mod1p = pl.BlockSpec((1, 1, d), lambda i, t: (i // spt, 0, 0))
    xb = pl.pallas_call(
        functools.partial(_dispatch_kernel, first_tail_block=min_blocks),
        grid_spec=pltpu.PrefetchScalarGridSpec(
            num_scalar_prefetch=1,
            grid=(nt,),
            in_specs=[row1p, mod1p, mod1p, pl.BlockSpec(memory_space=pl.ANY)],
            out_specs=pl.BlockSpec(memory_space=pl.ANY),
            scratch_shapes=[pltpu.VMEM((2, tm) + tile, F32), pltpu.VMEM((MOE_BLOCK,) + tile, F32),
                            pltpu.SMEM((TOP_K * tm,), jnp.int32),
                            pltpu.SemaphoreType.DMA((2,)), pltpu.SemaphoreType.DMA, pltpu.SemaphoreType.DMA]),
        out_shape=jax.ShapeDtypeStruct((nb * MOE_BLOCK,) + tile, F32),
        compiler_params=_cparams(("arbitrary",)),
        name="moe_dispatch",
    )(tbl, x, sc, sh, dest)

    last = nj - 1

    def jsel(i, j, nv):
        return jnp.where(i < nv[0], j, last)

    yb = pl.pallas_call(
        _expert_kernel,
        grid_spec=pltpu.PrefetchScalarGridSpec(
            num_scalar_prefetch=2,
            grid=(nb, nj),
            in_specs=[pl.BlockSpec((MOE_BLOCK,) + tile, lambda i, j, be, nv: (i, 0)),
                      pl.BlockSpec((1, 1, d, th), lambda i, j, be, nv: (layer, be[i], 0, jsel(i, j, nv))),
                      pl.BlockSpec((1, 1, d, th), lambda i, j, be, nv: (layer, be[i], 0, jsel(i, j, nv) + nj)),
                      pl.BlockSpec((1, 1, th, d), lambda i, j, be, nv: (layer, be[i], jsel(i, j, nv), 0))],
            out_specs=pl.BlockSpec((MOE_BLOCK,) + tile, lambda i, j, be, nv: (i, 0)),
            scratch_shapes=[pltpu.VMEM((MOE_BLOCK, d), BF16), pltpu.VMEM((MOE_BLOCK, d), F32)]),
        out_shape=jax.ShapeDtypeStruct((nb * MOE_BLOCK,) + tile, F32),
        compiler_params=_cparams(("arbitrary", "arbitrary")),
        name="moe_experts",
    )(blk_e, nvalid.reshape(1).astype(jnp.int32), xb, w_in, w_in, w_out)

    return pl.pallas_call(
        _combine_kernel,
        grid=(nt,),
        in_specs=[row1, mod1, slab1, pl.BlockSpec((1, d), lambda i: (0, 0)), pl.BlockSpec((1, d), lambda i: (0, 0)),
                  pl.BlockSpec(memory_space=pl.ANY), pl.BlockSpec(memory_space=pl.ANY)],
        out_specs=row1,
        out_shape=jax.ShapeDtypeStruct(x.shape, F32),
        scratch_shapes=[pltpu.VMEM((2, tm) + tile, F32), pltpu.VMEM((2, tm) + tile, F32),
                        pltpu.SMEM((TOP_K * tm,), jnp.int32),
                        pltpu.SemaphoreType.DMA((2,)), pltpu.SemaphoreType.DMA],
        compiler_params=_cparams(("arbitrary",)),
        name="moe_combine",
    )(x, gate, meta, ln_g, ln_b, dest, yb)


def kernel(x, c, positions, ada_w, ada_b, ln_g, ln_b, conv_w_in, conv_dw, conv_dw_b, conv_ln_g, conv_ln_b, conv_w_out, sgu_w_in, sgu_b_in, sgu_ln_g, sgu_ln_b, sgu_w_s, sgu_b_s, sgu_w_out, gdn_w_in, gdn_conv, gdn_a_log, gdn_dt_bias, gdn_norm_g, gdn_w_out, mla_w_in, mla_q_norm_g, mla_kv_norm_g, mla_w_uq, mla_w_ukv, mla_w_out, ffn_w_in, ffn_w_out, moe_router, moe_w_in, moe_w_out):
    bsz, s, d = x.shape
    mod = _ada_mod(c, ada_w, ada_b).reshape(DEPTH, bsz, 6, 1, d)
    for i in range(DEPTH):
        sh1, sc1, g1, sh2, sc2, g2 = (mod[i, :, t] for t in range(6))
        lg1, lb1 = ln_g[i, 0].reshape(1, d), ln_b[i, 0].reshape(1, d)
        lg2, lb2 = ln_g[i, 1].reshape(1, d), ln_b[i, 1].reshape(1, d)
        j = i // 4
        mixer = i % 4
        if mixer == 0:
            x = _conformer_layer(x, sc1, sh1, g1, lg1, lb1, conv_w_in[j], conv_dw[j], conv_dw_b[j],
                                 conv_ln_g[j], conv_ln_b[j], conv_w_out[j])
        elif mixer == 1:
            x = _sgu_layer(x, sc1, sh1, g1, lg1, lb1, sgu_w_in[j], sgu_b_in[j], sgu_ln_g[j], sgu_ln_b[j],
                           sgu_w_s[j], sgu_b_s[j], sgu_w_out[j])
        elif mixer == 2:
            x = _gdn_layer(x, sc1, sh1, g1, lg1, lb1, gdn_w_in[j], gdn_conv[j], gdn_a_log[j], gdn_dt_bias[j],
                           gdn_norm_g[j], gdn_w_out[j])
        else:
            x = _mla_layer(x, positions, sc1, sh1, g1, lg1, lb1, mla_w_in[j], mla_q_norm_g[j],
                           mla_kv_norm_g[j], mla_w_uq[j], mla_w_ukv[j], mla_w_out[j])
        if i % 2 == 0:
            x = _swiglu_layer(x, sc2, sh2, g2, lg2, lb2, ffn_w_in[i // 2], ffn_w_out[i // 2])
        else:
            x = _moe_layer(x, sc2, sh2, g2, lg2, lb2, moe_router[i // 2], moe_w_in, moe_w_out, i // 2)
    return x
```

```python
import functools
import math

import jax
import jax.numpy as jnp
from jax import lax
from jax.experimental import pallas as pl
from jax.experimental.pallas import tpu as pltpu

F32 = jnp.float32
BF16 = jnp.bfloat16
HI = lax.Precision.HIGHEST

D_MODEL = 1024
DEPTH = 4
ALPHA = (2 * DEPTH) ** 0.25

CONV_WIDTH = 31
CONV_HALO = 32
CONV_ROWS = 128
SGU_CHUNK = 128
SGU_GROUPS = 8
SGU_HALF = 2 * D_MODEL
GDN_HEADS = 8
GDN_DK = 128
GDN_DV = 128
GDN_CONV = 4
GDN_TILE = 256
GDN_TAIL = 8
GDN_CONV_COLS = 256
MLA_HEADS = 8
MLA_Q_RANK = 512
MLA_KV_RANK = 256
MLA_NOPE = 128
MLA_ROPE = 64
MLA_V = 128
ROPE_THETA = 10000.0
FFN_HIDDEN = 7 * D_MODEL // 2
N_EXPERTS = 8
TOP_K = 2
LANES = 128
SUBLANES = 8
MOE_BLOCK = 1008
FFN_ROW_SPLIT = 1
VMEM_LIMIT = 56 * 1024 * 1024


def _cparams(sem):
    return pltpu.CompilerParams(dimension_semantics=sem, vmem_limit_bytes=VMEM_LIMIT)


def _sigmoid(x):
    return 1.0 / (1.0 + jnp.exp(-x))


def _silu(x):
    return x * _sigmoid(x)


def _ln(z, g, b, eps=1e-5):
    mu = jnp.mean(z, -1, keepdims=True)
    zc = z - mu
    var = jnp.mean(zc * zc, -1, keepdims=True)
    return zc * lax.rsqrt(var + eps) * g + b


def _res_ln(x, y, gate, g, b):
    return _ln(ALPHA * x + (1.0 + gate) * y, g, b)


def _dot(a, b):
    return jnp.dot(a, b, preferred_element_type=F32)


def _dot_nt(a, b):
    return lax.dot_general(a, b, (((1,), (1,)), ((), ())), preferred_element_type=F32)


def _dot_tn(a, b):
    return lax.dot_general(a, b, (((0,), (0,)), ((), ())), preferred_element_type=F32)


def _dot_hi(a, b):
    return jnp.dot(a, b, precision=HI, preferred_element_type=F32)


def _split_bf16(a, parts):
    out = []
    for _ in range(parts):
        piece = a.astype(BF16)
        out.append(piece)
        a = a - piece.astype(F32)
    return out


def _dot3(a, b):
    (ah, al), (bh, bl) = _split_bf16(a, 2), _split_bf16(b, 2)
    return _dot(ah, bh) + (_dot(ah, bl) + _dot(al, bh))


def _ada_kernel(c_ref, w_ref, b_ref, o_ref):
    cond = _silu(c_ref[...])
    o_ref[0] = _dot3(cond, w_ref[0]) + b_ref[0]


def _ada_mod(c, ada_w, ada_b):
    bsz, d = c.shape
    n_out = ada_w.shape[-1]
    tn = 1024
    return pl.pallas_call(
        _ada_kernel,
        grid=(DEPTH, n_out // tn),
        in_specs=[
            pl.BlockSpec((bsz, d), lambda i, j: (0, 0)),
            pl.BlockSpec((1, d, tn), lambda i, j: (i, 0, j)),
            pl.BlockSpec((1, 1, tn), lambda i, j: (i, 0, j)),
        ],
        out_specs=pl.BlockSpec((1, bsz, tn), lambda i, j: (i, 0, j)),
        out_shape=jax.ShapeDtypeStruct((DEPTH, bsz, n_out), F32),
        compiler_params=_cparams(("parallel", "parallel")),
        name="ada_mod",
    )(c, ada_w, ada_b.reshape(DEPTH, 1, n_out))


def _row_spec(tm, width):
    return pl.BlockSpec((1, tm, width), lambda b, i: (b, i, 0))


def _mod_spec(width=D_MODEL):
    return pl.BlockSpec((1, 1, width), lambda b, i: (b, 0, 0))


def _full_spec(shape):
    nd = len(shape)
    return pl.BlockSpec(shape, lambda b, i: (0,) * nd)


def _outproj_kernel(a_ref, w_ref, x_ref, gate_ref, g_ref, b_ref, o_ref):
    y = _dot(a_ref[0], w_ref[...])
    o_ref[0] = _res_ln(x_ref[0], y, gate_ref[0], g_ref[...], b_ref[...])


def _outproj_ln(a, w, x, gate, ln_g, ln_b, tm=512):
    bsz, s, k = a.shape
    d = x.shape[-1]
    return pl.pallas_call(
        _outproj_kernel,
        grid=(bsz, s // tm),
        in_specs=[_row_spec(tm, k), _full_spec((k, d)), _row_spec(tm, d), _mod_spec(d),
                  _full_spec((1, d)), _full_spec((1, d))],
        out_specs=_row_spec(tm, d),
        out_shape=jax.ShapeDtypeStruct(x.shape, F32),
        compiler_params=_cparams(("parallel", "parallel")),
        name="outproj_ln",
    )(a, w, x, gate, ln_g, ln_b)


def _conv_in_kernel(x_ref, sc_ref, sh_ref, w_ref, o_ref):
    h = (x_ref[0] * (1.0 + sc_ref[0]) + sh_ref[0]).astype(BF16)
    ag = _dot(h, w_ref[...])
    d = o_ref.shape[-1]
    o_ref[0] = ag[:, :d] * _sigmoid(ag[:, d:])


def _conv_mid_kernel(y_ref, halo_ref, dw_ref, dwb_ref, cg_ref, cb_ref, w_ref,
                     x_ref, gate_ref, g_ref, b_ref, o_ref, ybuf, cbuf, shbuf):
    tm = y_ref.shape[1]
    d = y_ref.shape[2]
    first = pl.program_id(1) == 0
    ybuf[0:CONV_HALO, :] = jnp.where(first, 0.0, halo_ref[0])
    ybuf[CONV_HALO:, :] = y_ref[0]
    rc = CONV_ROWS
    off = CONV_HALO - (CONV_WIDTH - 1)
    def col_block(ci, carry):
        cols = pl.ds(pl.multiple_of(ci * LANES, LANES), LANES)
        for r0 in range(0, tm, rc):
            acc = None
            for r in range(SUBLANES):
                ks = [k for k in range(CONV_WIDTH) if (off + k) % SUBLANES == r]
                span = max((off + k) // SUBLANES for k in ks) * SUBLANES + rc
                shbuf[r, 0:span, :] = ybuf[r0 + r:r0 + r + span, cols]
                for k in ks:
                    q8 = (off + k) // SUBLANES * SUBLANES
                    term = dw_ref[k:k + 1, cols] * shbuf[r, q8:q8 + rc, :]
                    acc = term if acc is None else acc + term
            cbuf[r0:r0 + rc, cols] = acc
        return carry

    lax.fori_loop(0, d // LANES, col_block, 0)
    yc = cbuf[...] + dwb_ref[...]
    yn = _silu(_ln(yc, cg_ref[...], cb_ref[...]))
    yo = _dot(yn.astype(BF16), w_ref[...])
    o_ref[0] = _res_ln(x_ref[0], yo, gate_ref[0], g_ref[...], b_ref[...])


def _conformer_layer(x, sc, sh, gate, ln_g, ln_b, w_in, dw, dw_b, cg, cb, w_out, tm=512):
    bsz, s, d = x.shape
    y = pl.pallas_call(
        _conv_in_kernel,
        grid=(bsz, s // tm),
        in_specs=[_row_spec(tm, d), _mod_spec(), _mod_spec(), _full_spec((d, 2 * d))],
        out_specs=_row_spec(tm, d),
        out_shape=jax.ShapeDtypeStruct((bsz, s, d), F32),
        compiler_params=_cparams(("parallel", "parallel")),
        name="conv_in",
    )(x, sc, sh, w_in.astype(BF16))
    hb = tm // CONV_HALO
    halo_spec = pl.BlockSpec((1, CONV_HALO, d), lambda b, i: (b, jnp.maximum(i * hb - 1, 0), 0))
    dw_pad = jnp.concatenate([dw, jnp.zeros((1, d), F32)], 0)
    return pl.pallas_call(
        _conv_mid_kernel,
        grid=(bsz, s // tm),
        in_specs=[_row_spec(tm, d), halo_spec, _full_spec((CONV_WIDTH + 1, d)), _full_spec((1, d)),
                  _full_spec((1, d)), _full_spec((1, d)), _full_spec((d, d)),
                  _row_spec(tm, d), _mod_spec(), _full_spec((1, d)), _full_spec((1, d))],
        out_specs=_row_spec(tm, d),
        out_shape=jax.ShapeDtypeStruct((bsz, s, d), F32),
        scratch_shapes=[pltpu.VMEM((tm + CONV_HALO, d), F32), pltpu.VMEM((tm, d), F32),
                        pltpu.VMEM((SUBLANES, CONV_ROWS + CONV_HALO, LANES), F32)],
        compiler_params=_cparams(("parallel", "parallel")),
        name="conv_mid",
    )(y, y, dw_pad, dw_b.reshape(1, d), cg.reshape(1, d), cb.reshape(1, d), w_out.astype(BF16),
      x, gate, ln_g, ln_b)


def _swiglu_chunk(hbuf, wa, wb, wo, acc):
    rows = hbuf.shape[0] // FFN_ROW_SPLIT
    groups = [slice(g * rows, (g + 1) * rows) for g in range(FFN_ROW_SPLIT)]
    ab = [(_dot(hbuf[rs, :], wa), _dot(hbuf[rs, :], wb)) for rs in groups]
    acts = [(_silu(a) * b).astype(BF16) for a, b in ab]
    for rs, act in zip(groups, acts):
        acc[rs, :] += _dot(act, wo)


def _swiglu_kernel(x_ref, sc_ref, sh_ref, wa_ref, wb_ref, wo_ref, gate_ref, g_ref, b_ref,
                   o_ref, hbuf, acc):
    j = pl.program_id(2)

    @pl.when(j == 0)
    def _():
        hbuf[...] = (x_ref[0] * (1.0 + sc_ref[0]) + sh_ref[0]).astype(BF16)
        acc[...] = jnp.zeros_like(acc)

    _swiglu_chunk(hbuf, wa_ref[...], wb_ref[...], wo_ref[...], acc)

    @pl.when(j == pl.num_programs(2) - 1)
    def _():
        o_ref[0] = _res_ln(x_ref[0], acc[...], gate_ref[0], g_ref[...], b_ref[...])


def _swiglu_layer(x, sc, sh, gate, ln_g, ln_b, w_in, w_out, tm=512, th=1792):
    bsz, s, d = x.shape
    hid = w_out.shape[0]
    nj = hid // th
    row = pl.BlockSpec((1, tm, d), lambda b, i, j: (b, i, 0))
    mod = pl.BlockSpec((1, 1, d), lambda b, i, j: (b, 0, 0))
    vec = pl.BlockSpec((1, d), lambda b, i, j: (0, 0))
    w_in_b = w_in.astype(BF16)
    return pl.pallas_call(
        _swiglu_kernel,
        grid=(bsz, s // tm, nj),
        in_specs=[row, mod, mod,
                  pl.BlockSpec((d, th), lambda b, i, j: (0, j)),
                  pl.BlockSpec((d, th), lambda b, i, j: (0, j + nj)),
                  pl.BlockSpec((th, d), lambda b, i, j: (j, 0)),
                  mod, vec, vec],
        out_specs=row,
        out_shape=jax.ShapeDtypeStruct(x.shape, F32),
        scratch_shapes=[pltpu.VMEM((tm, d), BF16), pltpu.VMEM((tm, d), F32)],
        compiler_params=_cparams(("parallel", "parallel", "arbitrary")),
        name="swiglu",
    )(x, sc, sh, w_in_b, w_in_b, w_out.astype(BF16), gate, ln_g, ln_b)


def _sgu_kernel(x_ref, sc_ref, sh_ref, wi_ref, bi_ref, vg_ref, vb_ref, ws_ref, bs_ref, wo_ref,
                gate_ref, g_ref, b_ref, o_ref, gbuf):
    tm = x_ref.shape[1]
    h = (x_ref[0] * (1.0 + sc_ref[0]) + sh_ref[0]).astype(BF16)
    z = _dot(h, wi_ref[...]) + bi_ref[...]
    z = 0.5 * z * (1.0 + lax.erf(z * (2.0 ** -0.5)))
    u = z[:, :SGU_HALF]
    v = _ln(z[:, SGU_HALF:], vg_ref[...], vb_ref[...]).astype(BF16)
    gw = SGU_HALF // SGU_GROUPS
    ri = lax.broadcasted_iota(jnp.int32, (SGU_CHUNK, SGU_CHUNK), 0)
    ci = lax.broadcasted_iota(jnp.int32, (SGU_CHUNK, SGU_CHUNK), 1)
    causal = ci <= ri
    for g in range(SGU_GROUPS):
        wsg = jnp.where(causal, ws_ref[g], 0.0).astype(BF16)
        bias = bs_ref[:, g:g + 1]
        for c in range(tm // SGU_CHUNK):
            rows = slice(c * SGU_CHUNK, (c + 1) * SGU_CHUNK)
            cols = slice(g * gw, (g + 1) * gw)
            sv = _dot(wsg, v[rows, cols]) + bias
            gbuf[rows, cols] = (u[rows, cols] * sv).astype(BF16)
    yo = _dot(gbuf[...], wo_ref[...])
    o_ref[0] = _res_ln(x_ref[0], yo, gate_ref[0], g_ref[...], b_ref[...])


def _sgu_layer(x, sc, sh, gate, ln_g, ln_b, w_in, b_in, vg, vb, w_s, b_s, w_out, tm=512):
    bsz, s, d = x.shape
    return pl.pallas_call(
        _sgu_kernel,
        grid=(bsz, s // tm),
        in_specs=[_row_spec(tm, d), _mod_spec(), _mod_spec(),
                  _full_spec((d, 2 * SGU_HALF)), _full_spec((1, 2 * SGU_HALF)),
                  _full_spec((1, SGU_HALF)), _full_spec((1, SGU_HALF)),
                  _full_spec((SGU_GROUPS, SGU_CHUNK, SGU_CHUNK)), _full_spec((SGU_CHUNK, SGU_GROUPS)),
                  _full_spec((SGU_HALF, d)), _mod_spec(), _full_spec((1, d)), _full_spec((1, d))],
        out_specs=_row_spec(tm, d),
        out_shape=jax.ShapeDtypeStruct(x.shape, F32),
        scratch_shapes=[pltpu.VMEM((tm, SGU_HALF), BF16)],
        compiler_params=_cparams(("parallel", "parallel")),
        name="sgu",
    )(x, sc, sh, w_in.astype(BF16), b_in.reshape(1, -1), vg.reshape(1, -1), vb.reshape(1, -1),
      w_s, b_s.T, w_out.astype(BF16), gate, ln_g, ln_b)


def _gdn_in_kernel(x_ref, sc_ref, sh_ref, wq_ref, wz_ref, wba_ref, alog_ref, dtb_ref, cw_ref,
                   qkv_ref, z_ref, bg_ref, gct_ref, tail, cbuf):
    tm = x_ref.shape[1]
    c = GDN_TILE
    nqk = 2 * GDN_HEADS * GDN_DK
    hf = x_ref[0] * (1.0 + sc_ref[0]) + sh_ref[0]
    h = hf.astype(BF16)

    @pl.when(pl.program_id(1) == 0)
    def _():
        tail[...] = jnp.zeros_like(tail)

    off = GDN_TAIL - (GDN_CONV - 1)
    cgw = GDN_CONV_COLS
    chunks = [slice(c0, c0 + cgw) for c0 in range(0, qkv_ref.shape[2], cgw)]
    cbuf[0:GDN_TAIL, :] = tail[...]
    for cols in chunks:
        cbuf[GDN_TAIL:, cols] = _dot(h, wq_ref[:, cols])
    tail[...] = cbuf[tm:tm + GDN_TAIL, :]
    for cols in chunks:
        c0 = cols.start
        acc = cw_ref[0:1, cols] * cbuf[off:off + tm, cols]
        for t in range(1, GDN_CONV):
            acc = acc + cw_ref[t:t + 1, cols] * cbuf[off + t:off + t + tm, cols]
        y = _silu(acc)
        for h0 in range(0, cgw, GDN_DK):
            col = c0 + h0
            blk = y[:, h0:h0 + GDN_DK]
            if col < nqk:
                blk = blk * lax.rsqrt(jnp.sum(blk * blk, -1, keepdims=True) + 1e-6)
            if col < nqk // 2:
                blk = blk * (GDN_DK ** -0.5)
            qkv_ref[0, :, col:col + GDN_DK] = blk
    z_ref[0] = _dot(h, wz_ref[...]).astype(BF16)
    ba = _dot3(hf, wba_ref[...])
    bg_ref[0, :, :LANES] = _sigmoid(ba)
    a_in = ba + dtb_ref[...]
    softplus = jnp.maximum(a_in, 0.0) + jnp.log1p(jnp.exp(-jnp.abs(a_in)))
    g_all = -jnp.exp(alog_ref[...]) * softplus
    ri = lax.broadcasted_iota(jnp.int32, (c, c), 0)
    ci = lax.broadcasted_iota(jnp.int32, (c, c), 1)
    tril = (ci <= ri).astype(BF16)
    for r0 in range(0, tm, c):
        g_hi, g_mid, g_lo = _split_bf16(g_all[r0:r0 + c], 3)
        gc = _dot(tril, g_hi) + (_dot(tril, g_mid) + _dot(tril, g_lo))
        bg_ref[0, r0:r0 + c, LANES:] = gc
        gct_ref[0, :, r0:r0 + c] = gc.T[GDN_HEADS:2 * GDN_HEADS]


def _gdn_chunk_kernel(q_ref, k_ref, v_ref, bg_ref, gct_ref, z_ref, ng_ref, o_ref, state, *, hp):
    c = q_ref.shape[1]

    @pl.when(pl.program_id(2) == 0)
    def _():
        state[...] = jnp.zeros_like(state)

    lane = lax.broadcasted_iota(jnp.int32, (1, LANES), 1)
    sub = lax.broadcasted_iota(jnp.int32, (GDN_HEADS, 1), 0)
    ri = lax.broadcasted_iota(jnp.int32, (c, c), 0)
    ci = lax.broadcasted_iota(jnp.int32, (c, c), 1)
    causal = ci <= ri
    strict = ci < ri
    xor = ri ^ ci
    eye = (ri == ci).astype(F32)
    bg = bg_ref[0]
    gct = gct_ref[0]
    z_all = z_ref[0]
    heads = range(hp)
    q, k, v, kb, beta, gc, dmat, a_mat = ([None] * hp for _ in range(8))
    for t in heads:
        hd = pl.program_id(1) * hp + t
        cols = slice(t * LANES, (t + 1) * LANES)
        q[t], k[t], v[t] = q_ref[0, :, cols], k_ref[0, :, cols], v_ref[0, :, cols]
        sel = lane == hd
        beta[t] = jnp.sum(jnp.where(sel, bg[:, :LANES], 0.0), -1, keepdims=True)
        gc[t] = jnp.sum(jnp.where(lane == GDN_HEADS + hd, bg[:, LANES:], 0.0), -1, keepdims=True)
        gc_row = jnp.sum(jnp.where(sub == hd, gct, 0.0), 0, keepdims=True)
        dmat[t] = jnp.where(causal, jnp.exp(jnp.where(causal, gc[t] - gc_row, 0.0)), 0.0)
        kb[t] = k[t].astype(BF16)
    for t in heads:
        a_mat[t] = jnp.where(strict, beta[t] * _dot_nt(kb[t], kb[t]) * dmat[t], 0.0)
    inv = [eye - jnp.where(xor == 1, a_mat[t], 0.0) for t in heads]
    lvl = 1
    while (1 << lvl) < c:
        invb = [inv[t].astype(BF16) for t in heads]
        joins = (xor >> lvl) == 1
        dm = [_dot(invb[t], jnp.where(joins, a_mat[t], 0.0).astype(BF16)) for t in heads]
        inv = [inv[t] - _dot(dm[t].astype(BF16), invb[t]) for t in heads]
        lvl += 1

    egc = [jnp.exp(gc[t]) for t in heads]
    sol = [_dot(inv[t].astype(BF16),
                jnp.concatenate([v[t] * beta[t], k[t] * (beta[t] * egc[t])], -1).astype(BF16)) for t in heads]
    qk = [jnp.where(causal, _dot_nt(q[t].astype(BF16), kb[t]) * dmat[t], 0.0).astype(BF16) for t in heads]
    s_prev = [state[t] for t in heads]
    sb = [s_prev[t].astype(BF16) for t in heads]
    vnb = [(sol[t][:, :GDN_DV] - _dot(sol[t][:, GDN_DV:].astype(BF16), sb[t])).astype(BF16) for t in heads]
    o = [_dot((q[t] * egc[t]).astype(BF16), sb[t]) + _dot(qk[t], vnb[t]) for t in heads]
    for t in heads:
        g_last = gc[t][c - 1:c, :]
        k_dec = (k[t] * jnp.exp(g_last - gc[t])).astype(BF16)
        state[t] = s_prev[t] * jnp.exp(g_last) + _dot_tn(k_dec, vnb[t])
    for t in heads:
        cols = slice(t * LANES, (t + 1) * LANES)
        on = o[t] * lax.rsqrt(jnp.mean(o[t] * o[t], -1, keepdims=True) + 1e-6) * ng_ref[...]
        o_ref[0, :, cols] = (on * _silu(z_all[:, cols].astype(F32))).astype(BF16)


def _gdn_layer(x, sc, sh, gate, ln_g, ln_b, w_in, conv_w, a_log, dt_bias, norm_g, w_out, tm=512, hp=4):
    bsz, s, d = x.shape
    nh = GDN_HEADS
    nqkv = 3 * nh * GDN_DK
    w_qkv = w_in[:, :nqkv].astype(BF16)
    w_z = w_in[:, nqkv:nqkv + nh * GDN_DV].astype(BF16)
    w_b = w_in[:, nqkv + nh * GDN_DV:nqkv + nh * GDN_DV + nh]
    w_a = w_in[:, nqkv + nh * GDN_DV + nh:]
    w_ba = jnp.concatenate([w_b, w_a, jnp.zeros((d, LANES - 2 * nh), F32)], 1)
    head_pad, lane_pad = jnp.zeros((nh,), F32), jnp.zeros((LANES - 2 * nh,), F32)
    alog = jnp.concatenate([head_pad, a_log, lane_pad]).reshape(1, LANES)
    dtb = jnp.concatenate([head_pad, dt_bias, lane_pad]).reshape(1, LANES)
    qkv, z, bg, gct = pl.pallas_call(
        _gdn_in_kernel,
        grid=(bsz, s // tm),
        in_specs=[_row_spec(tm, d), _mod_spec(), _mod_spec(), _full_spec((d, nqkv)),
                  _full_spec((d, nh * GDN_DV)), _full_spec((d, LANES)),
                  _full_spec((1, LANES)), _full_spec((1, LANES)), _full_spec((GDN_CONV, nqkv))],
        out_specs=[_row_spec(tm, nqkv), _row_spec(tm, nh * GDN_DV), _row_spec(tm, 2 * LANES),
                   pl.BlockSpec((1, nh, tm), lambda b, i: (b, 0, i))],
        out_shape=[jax.ShapeDtypeStruct((bsz, s, nqkv), F32),
                   jax.ShapeDtypeStruct((bsz, s, nh * GDN_DV), BF16),
                   jax.ShapeDtypeStruct((bsz, s, 2 * LANES), F32),
                   jax.ShapeDtypeStruct((bsz, nh, s), F32)],
        scratch_shapes=[pltpu.VMEM((GDN_TAIL, nqkv), F32), pltpu.VMEM((tm + GDN_TAIL, nqkv), F32)],
        compiler_params=_cparams(("parallel", "arbitrary")),
        name="gdn_in",
    )(x, sc, sh, w_qkv, w_z, w_ba, alog, dtb, conv_w)

    c = GDN_TILE
    wide = hp * LANES
    ng = nh // hp

    def col(base):
        return pl.BlockSpec((1, c, wide), lambda b, h, i: (b, i, base + h))

    og = pl.pallas_call(
        functools.partial(_gdn_chunk_kernel, hp=hp),
        grid=(bsz, ng, s // c),
        in_specs=[col(0), col(ng), col(2 * ng),
                  pl.BlockSpec((1, c, 2 * LANES), lambda b, h, i: (b, i, 0)),
                  pl.BlockSpec((1, nh, c), lambda b, h, i: (b, 0, i)),
                  col(0), pl.BlockSpec((1, LANES), lambda b, h, i: (0, 0))],
        out_specs=col(0),
        out_shape=jax.ShapeDtypeStruct((bsz, s, nh * GDN_DV), BF16),
        scratch_shapes=[pltpu.VMEM((hp, GDN_DK, GDN_DV), F32)],
        compiler_params=_cparams(("parallel", "parallel", "arbitrary")),
        name="gdn_chunk",
    )(qkv, qkv, qkv, bg, gct, z, norm_g.reshape(1, LANES))
    return _outproj_ln(og, w_out.astype(BF16), x, gate, ln_g, ln_b)


def _mla_in_kernel(x_ref, sc_ref, sh_ref, pos_ref, wi_ref, qg_ref, kg_ref, wqn_ref, wqr_ref, wqt_ref,
                   wkv_ref, qn_ref, qr_ref, kn_ref, kr_ref, v_ref):
    h = (x_ref[0] * (1.0 + sc_ref[0]) + sh_ref[0]).astype(BF16)
    p = _dot(h, wi_ref[...])
    cq = p[:, :MLA_Q_RANK]
    ckv = p[:, MLA_Q_RANK:MLA_Q_RANK + MLA_KV_RANK]
    kr = p[:, MLA_Q_RANK + MLA_KV_RANK:MLA_Q_RANK + MLA_KV_RANK + LANES]
    krt = p[:, MLA_Q_RANK + MLA_KV_RANK + LANES:]
    cqn = (cq * lax.rsqrt(jnp.mean(cq * cq, -1, keepdims=True) + 1e-6) * qg_ref[...]).astype(BF16)
    ckn = (ckv * lax.rsqrt(jnp.mean(ckv * ckv, -1, keepdims=True) + 1e-6) * kg_ref[...]).astype(BF16)
    lane = lax.broadcasted_iota(jnp.int32, (1, LANES), 1)
    half = MLA_ROPE // 2
    fidx = (lane % half).astype(F32)
    inv_freq = jnp.exp(fidx * (-math.log(ROPE_THETA) / half))
    ang = pos_ref[0].astype(F32) * inv_freq
    live = lane < MLA_ROPE
    cos = jnp.where(live, jnp.cos(ang), 0.0)
    sin = jnp.where(live, jnp.sin(ang), 0.0)
    scale = (MLA_NOPE + MLA_ROPE) ** -0.5 * math.log2(math.e)
    qn_ref[0] = (_dot(cqn, wqn_ref[...]) * scale).astype(BF16)
    qr = _dot(cqn, wqr_ref[...])
    qrt = _dot(cqn, wqt_ref[...])
    for hd in range(MLA_HEADS):
        cs = slice(hd * LANES, (hd + 1) * LANES)
        qr_ref[0, :, cs] = ((qr[:, cs] * cos + qrt[:, cs] * sin) * scale).astype(BF16)
    kr_ref[0] = (kr * cos + krt * sin).astype(BF16)
    kv = _dot(ckn, wkv_ref[...])
    nk = MLA_HEADS * MLA_NOPE
    kn_ref[0] = kv[:, :nk].astype(BF16)
    ones = jnp.ones((kv.shape[0], LANES), BF16)
    for hd in range(MLA_HEADS):
        v_ref[0, :, 2 * hd * LANES:(2 * hd + 1) * LANES] = kv[:, nk + hd * MLA_V:nk + (hd + 1) * MLA_V].astype(BF16)
        v_ref[0, :, (2 * hd + 1) * LANES:(2 * hd + 2) * LANES] = ones


def _mla_attn_kernel(qn_ref, qr_ref, kn_ref, kr_ref, v_ref, o_ref, qbuf, m_sc, acc_sc, *, hp):
    tq = qn_ref.shape[1]
    tk = tq
    i = pl.program_id(2)
    for h in range(hp):
        cols = slice(h * LANES, (h + 1) * LANES)
        qbuf[h, :, :LANES] = qn_ref[0, :, cols]
        qbuf[h, :, LANES:] = qr_ref[0, :, cols]
    m_sc[...] = jnp.full_like(m_sc, -jnp.inf)
    acc_sc[...] = jnp.zeros_like(acc_sc)

    def step(j, masked):
        r0 = pl.multiple_of(j * tk, tk)
        krt = kr_ref[0, pl.ds(r0, tk), :]
        if masked:
            ri = lax.broadcasted_iota(jnp.int32, (tq, tk), 0)
            ci = lax.broadcasted_iota(jnp.int32, (tq, tk), 1)
            keep = ci <= ri
        scs, ps, alphas = [], [], []
        for h in range(hp):
            cols = slice(h * LANES, (h + 1) * LANES)
            kt = jnp.concatenate([kn_ref[0, pl.ds(r0, tk), cols], krt], -1)
            scs.append(_dot_nt(qbuf[h], kt))
        for h in range(hp):
            sc = jnp.where(keep, scs[h], -jnp.inf) if masked else scs[h]
            m_prev = m_sc[h]
            m_new = jnp.maximum(m_prev, jnp.max(sc, -1, keepdims=True))
            alphas.append(jnp.exp2(m_prev - m_new))
            m_sc[h] = m_new
            ps.append(jnp.concatenate(
                [jnp.exp2(sc[:, c0:c0 + LANES] - m_new) for c0 in range(0, tk, LANES)], -1).astype(BF16))
        for h in range(hp):
            vcols = slice(2 * h * LANES, (2 * h + 2) * LANES)
            alpha2 = jnp.concatenate([alphas[h], alphas[h]], -1)
            acc_sc[h] = alpha2 * acc_sc[h] + _dot(ps[h], v_ref[0, pl.ds(r0, tk), vcols])

    def body(j, carry):
        step(j, False)
        return carry

    lax.fori_loop(0, i, body, 0)
    step(i, True)
    for h in range(hp):
        o_ref[0, :, h * LANES:(h + 1) * LANES] = (acc_sc[h, :, :LANES] / acc_sc[h, :, LANES:]).astype(BF16)


def _mla_layer(x, positions, sc, sh, gate, ln_g, ln_b, w_in, q_norm_g, kv_norm_g, w_uq, w_ukv, w_out,
               tm=512, tq=512, hp=4):
    bsz, s, d = x.shape
    nh = MLA_HEADS
    half = MLA_ROPE // 2
    rot = jnp.concatenate([jnp.arange(half, MLA_ROPE), jnp.arange(half)])
    sign = jnp.concatenate([-jnp.ones((half,), F32), jnp.ones((half,), F32)])
    zpad = jnp.zeros((d, LANES - MLA_ROPE), F32)
    w_kr = w_in[:, MLA_Q_RANK + MLA_KV_RANK:]
    w_in_ext = jnp.concatenate(
        [w_in[:, :MLA_Q_RANK + MLA_KV_RANK], w_kr, zpad, w_kr[:, rot] * sign, zpad], 1).astype(BF16)
    wq = w_uq.reshape(MLA_Q_RANK, nh, MLA_NOPE + MLA_ROPE)
    w_qn = wq[:, :, :MLA_NOPE].reshape(MLA_Q_RANK, nh * MLA_NOPE).astype(BF16)
    wq_r = wq[:, :, MLA_NOPE:]
    hpad = jnp.zeros((MLA_Q_RANK, nh, LANES - MLA_ROPE), F32)
    w_qr = jnp.concatenate([wq_r, hpad], -1).reshape(MLA_Q_RANK, nh * LANES).astype(BF16)
    w_qt = jnp.concatenate([wq_r[:, :, rot] * sign, hpad], -1).reshape(MLA_Q_RANK, nh * LANES).astype(BF16)
    wkv = w_ukv.reshape(MLA_KV_RANK, nh, MLA_NOPE + MLA_V)
    w_kv = jnp.concatenate([wkv[:, :, :MLA_NOPE].reshape(MLA_KV_RANK, nh * MLA_NOPE),
                            wkv[:, :, MLA_NOPE:].reshape(MLA_KV_RANK, nh * MLA_V)], 1).astype(BF16)
    n_in = w_in_ext.shape[1]
    wide = nh * LANES
    qn, qr, kn, kr, v = pl.pallas_call(
        _mla_in_kernel,
        grid=(bsz, s // tm),
        in_specs=[_row_spec(tm, d), _mod_spec(), _mod_spec(), _row_spec(tm, 1),
                  _full_spec((d, n_in)), _full_spec((1, MLA_Q_RANK)), _full_spec((1, MLA_KV_RANK)),
                  _full_spec((MLA_Q_RANK, wide)), _full_spec((MLA_Q_RANK, wide)),
                  _full_spec((MLA_Q_RANK, wide)), _full_spec((MLA_KV_RANK, 2 * wide))],
        out_specs=[_row_spec(tm, wide), _row_spec(tm, wide), _row_spec(tm, wide), _row_spec(tm, LANES),
                   _row_spec(tm, 2 * wide)],
        out_shape=[jax.ShapeDtypeStruct((bsz, s, wide), BF16), jax.ShapeDtypeStruct((bsz, s, wide), BF16),
                   jax.ShapeDtypeStruct((bsz, s, wide), BF16), jax.ShapeDtypeStruct((bsz, s, LANES), BF16),
                   jax.ShapeDtypeStruct((bsz, s, 2 * wide), BF16)],
        compiler_params=_cparams(("parallel", "parallel")),
        name="mla_in",
    )(x, sc, sh, positions.reshape(bsz, s, 1), w_in_ext, q_norm_g.reshape(1, -1), kv_norm_g.reshape(1, -1),
      w_qn, w_qr, w_qt, w_kv)

    gw = hp * LANES
    qspec = pl.BlockSpec((1, tq, gw), lambda b, h, i: (b, i, h))
    kspec = pl.BlockSpec((1, s, gw), lambda b, h, i: (b, 0, h))
    o = pl.pallas_call(
        functools.partial(_mla_attn_kernel, hp=hp),
        grid=(bsz, nh // hp, s // tq),
        in_specs=[qspec, qspec, kspec, pl.BlockSpec((1, s, LANES), lambda b, h, i: (b, 0, 0)),
                  pl.BlockSpec((1, s, 2 * gw), lambda b, h, i: (b, 0, h))],
        out_specs=qspec,
        out_shape=jax.ShapeDtypeStruct((bsz, s, wide), BF16),
        scratch_shapes=[pltpu.VMEM((hp, tq, 2 * LANES), BF16), pltpu.VMEM((hp, tq, LANES), F32),
                        pltpu.VMEM((hp, tq, 2 * MLA_V), F32)],
        compiler_params=_cparams(("parallel", "parallel", "arbitrary")),
        name="mla_attn",
    )(qn, qr, kn, kr, v)
    return _outproj_ln(o, w_out.astype(BF16), x, gate, ln_g, ln_b)


def _router_kernel(x_ref, sc_ref, sh_ref, wr_ref, meta_ref, cnt_ref, h_ref, run):
    tm = x_ref.shape[1]

    @pl.when(pl.program_id(0) == 0)
    def _():
        run[...] = jnp.zeros_like(run)

    h = x_ref[0] * (1.0 + sc_ref[0]) + sh_ref[0]
    h_ref[...] = h
    lane = lax.broadcasted_iota(jnp.int32, (tm, LANES), 1)
    lane_f = lane.astype(F32)
    logits = jnp.where(lane < N_EXPERTS, _dot3(h, wr_ref[...]), -jnp.inf)
    m1 = jnp.max(logits, -1, keepdims=True)
    i1 = jnp.min(jnp.where(logits == m1, lane_f, float(LANES)), -1, keepdims=True)
    oh1 = lane_f == i1
    rest = jnp.where(oh1, -jnp.inf, logits)
    m2 = jnp.max(rest, -1, keepdims=True)
    i2 = jnp.min(jnp.where(rest == m2, lane_f, float(LANES)), -1, keepdims=True)
    oh2 = lane_f == i2
    e21 = jnp.exp(m2 - m1)
    g1 = 1.0 / (1.0 + e21)
    g2 = e21 / (1.0 + e21)
    cnt = oh1.astype(F32) + oh2.astype(F32)
    ri = lax.broadcasted_iota(jnp.int32, (tm, tm), 0)
    ci = lax.broadcasted_iota(jnp.int32, (tm, tm), 1)
    before = _dot((ci < ri).astype(BF16), cnt.astype(BF16)) + run[...]
    r1 = jnp.sum(jnp.where(oh1, before, 0.0), -1, keepdims=True)
    r2 = jnp.sum(jnp.where(oh2, before, 0.0), -1, keepdims=True)
    meta = jnp.zeros((tm, LANES), F32)
    for k, col in enumerate((i1, i2, g1, g2, r1, r2)):
        meta = jnp.where(lane == k, col, meta)
    meta_ref[...] = meta
    run[...] = run[...] + jnp.sum(cnt, 0, keepdims=True)
    cnt_ref[...] = run[...]


def _expert_kernel(be_sm, nv_sm, src_hbm, h_hbm, wa_ref, wb_ref, wo_ref, o_ref,
                   xbuf, idx_sm, hbuf, acc, gsem, isem, *, nj):
    i = pl.program_id(0)
    j = pl.program_id(1)
    last_i = pl.num_programs(0) - 1
    slot = i % 2
    rows = MOE_BLOCK // nj

    def fetch_list(block, s):
        return pltpu.make_async_copy(src_hbm.at[block], idx_sm.at[s], isem)

    def row_copy(s, r):
        return pltpu.make_async_copy(h_hbm.at[pl.ds(idx_sm[s, r], 1)], xbuf.at[s, pl.ds(r, 1)], gsem.at[s])

    def wait_block(s):
        pltpu.make_async_copy(h_hbm.at[pl.ds(0, MOE_BLOCK)], xbuf.at[s], gsem.at[s]).wait()

    @pl.when((i == 0) & (j == 0))
    def _():
        for blk in range(2):
            cp = fetch_list(blk, blk)
            cp.start()
            cp.wait()

        def first_block(r, carry):
            row_copy(0, r).start()
            return carry

        lax.fori_loop(0, MOE_BLOCK, first_block, 0, unroll=8)

    @pl.when(j == 0)
    def _():
        wait_block(slot)
        fetch_list(i + 2, slot).start()

    def gather_next():
        base = pl.multiple_of(j * rows, SUBLANES)
        for u in range(rows):
            row_copy(1 - slot, base + u).start()

    @pl.when(i >= nv_sm[0])
    def _():
        gather_next()

        @pl.when(j == nj - 1)
        def _():
            o_ref[...] = jnp.zeros_like(o_ref)

    @pl.when(i < nv_sm[0])
    def _():
        @pl.when(j == 0)
        def _():
            hbuf[...] = xbuf[slot].astype(BF16)
            acc[...] = jnp.zeros_like(acc)

        gather_next()
        _swiglu_chunk(hbuf, wa_ref[0, 0].astype(BF16), wb_ref[0, 0].astype(BF16), wo_ref[0, 0].astype(BF16), acc)

        @pl.when(j == nj - 1)
        def _():
            o_ref[...] = acc[...]

    @pl.when(j == nj - 1)
    def _():
        fetch_list(i + 2, slot).wait()

        @pl.when(i == last_i)
        def _():
            wait_block(1 - slot)


def _combine_kernel(x_ref, gate_ref, meta_ref, g_ref, b_ref, dest_hbm, yb_hbm, o_ref,
                    y0, y1, idx_sm, sem, isem):
    tm = x_ref.shape[1]
    i = pl.program_id(0)
    slot = i % 2

    def gather(tile, s):
        icp = pltpu.make_async_copy(dest_hbm.at[tile], idx_sm, isem)
        icp.start()
        icp.wait()
        d0, d1 = y0.at[s], y1.at[s]

        def issue(r, carry):
            pltpu.make_async_copy(yb_hbm.at[pl.ds(idx_sm[2 * r], 1)], d0.at[pl.ds(r, 1)], sem.at[s]).start()
            pltpu.make_async_copy(yb_hbm.at[pl.ds(idx_sm[2 * r + 1], 1)], d1.at[pl.ds(r, 1)], sem.at[s]).start()
            return carry

        lax.fori_loop(0, tm, issue, 0, unroll=8)

    @pl.when(i == 0)
    def _():
        gather(0, 0)

    @pl.when(i + 1 < pl.num_programs(0))
    def _():
        gather(i + 1, 1 - slot)

    for buf in (y0, y1):
        pltpu.make_async_copy(yb_hbm.at[pl.ds(0, tm)], buf.at[slot], sem.at[slot]).wait()
    meta = meta_ref[...]
    y = meta[:, 2:3] * y0[slot] + meta[:, 3:4] * y1[slot]
    o_ref[0] = _res_ln(x_ref[0], y, gate_ref[0], g_ref[...], b_ref[...])


def _moe_layer(x, sc, sh, gate, ln_g, ln_b, router, w_in, w_out, layer, tm=256, tr=512, th=512):
    bsz, s, d = x.shape
    n = bsz * s
    nt = n // tm
    spt = s // tm
    hid = w_out.shape[-2]
    nj = hid // th
    nb = -(-n * TOP_K // MOE_BLOCK) + N_EXPERTS
    tile = (d,)

    row1 = pl.BlockSpec((1, tm, d), lambda i: (i // spt, i % spt, 0))
    mod1 = pl.BlockSpec((1, 1, d), lambda i: (i // spt, 0, 0))
    slab1 = pl.BlockSpec((tm, LANES), lambda i: (i, 0))
    w_r = jnp.concatenate([router, jnp.zeros((d, LANES - N_EXPERTS), F32)], 1)
    rpt = s // tr
    meta, counts, hmod = pl.pallas_call(
        _router_kernel,
        grid=(n // tr,),
        in_specs=[pl.BlockSpec((1, tr, d), lambda i: (i // rpt, i % rpt, 0)),
                  pl.BlockSpec((1, 1, d), lambda i: (i // rpt, 0, 0)),
                  pl.BlockSpec((1, 1, d), lambda i: (i // rpt, 0, 0)),
                  pl.BlockSpec((d, LANES), lambda i: (0, 0))],
        out_specs=[pl.BlockSpec((tr, LANES), lambda i: (i, 0)), pl.BlockSpec((1, LANES), lambda i: (0, 0)),
                   pl.BlockSpec((tr, d), lambda i: (i, 0))],
        out_shape=[jax.ShapeDtypeStruct((n, LANES), F32), jax.ShapeDtypeStruct((1, LANES), F32),
                   jax.ShapeDtypeStruct((n, d), F32)],
        scratch_shapes=[pltpu.VMEM((1, LANES), F32)],
        compiler_params=_cparams(("arbitrary",)),
        name="moe_router",
    )(x, sc, sh, w_r)

    cnt = counts[0, :N_EXPERTS].astype(jnp.int32)
    nblk = (cnt + MOE_BLOCK - 1) // MOE_BLOCK
    ends = jnp.cumsum(nblk)
    first_row = (ends - nblk) * MOE_BLOCK
    nvalid = ends[-1]
    eid = meta[:, :TOP_K].astype(jnp.int32)
    rank = meta[:, 4:4 + TOP_K].astype(jnp.int32)
    base = jnp.sum(jnp.where(eid[:, :, None] == jnp.arange(N_EXPERTS), first_row, 0), -1)
    dest = (base + rank).reshape(nt, TOP_K * tm)
    bi = jnp.minimum(jnp.arange(nb, dtype=jnp.int32), nvalid - 1)
    blk_e = jnp.sum(bi[:, None] >= ends[None, :], -1).astype(jnp.int32)
    tok = jnp.repeat(jnp.arange(n, dtype=jnp.int32), TOP_K)
    src = jnp.zeros(((nb + 2) * MOE_BLOCK,), jnp.int32).at[dest.reshape(-1)].set(tok, unique_indices=True)
    src = src.reshape(nb + 2, MOE_BLOCK)

    last = nj - 1

    def jsel(i, j, nv):
        return jnp.where(i < nv[0], j, last)

    yb = pl.pallas_call(
        functools.partial(_expert_kernel, nj=nj),
        grid_spec=pltpu.PrefetchScalarGridSpec(
            num_scalar_prefetch=2,
            grid=(nb, nj),
            in_specs=[pl.BlockSpec(memory_space=pl.ANY), pl.BlockSpec(memory_space=pl.ANY),
                      pl.BlockSpec((1, 1, d, th), lambda i, j, be, nv: (layer, be[i], 0, jsel(i, j, nv))),
                      pl.BlockSpec((1, 1, d, th), lambda i, j, be, nv: (layer, be[i], 0, jsel(i, j, nv) + nj)),
                      pl.BlockSpec((1, 1, th, d), lambda i, j, be, nv: (layer, be[i], jsel(i, j, nv), 0))],
            out_specs=pl.BlockSpec((MOE_BLOCK,) + tile, lambda i, j, be, nv: (i, 0)),
            scratch_shapes=[pltpu.VMEM((2, MOE_BLOCK, d), F32), pltpu.SMEM((2, MOE_BLOCK), jnp.int32),
                            pltpu.VMEM((MOE_BLOCK, d), BF16), pltpu.VMEM((MOE_BLOCK, d), F32),
                            pltpu.SemaphoreType.DMA((2,)), pltpu.SemaphoreType.DMA]),
        out_shape=jax.ShapeDtypeStruct((nb * MOE_BLOCK,) + tile, F32),
        compiler_params=_cparams(("arbitrary", "arbitrary")),
        name="moe_experts",
    )(blk_e, nvalid.reshape(1).astype(jnp.int32), src, hmod, w_in, w_in, w_out)

    return pl.pallas_call(
        _combine_kernel,
        grid=(nt,),
        in_specs=[row1, mod1, slab1, pl.BlockSpec((1, d), lambda i: (0, 0)), pl.BlockSpec((1, d), lambda i: (0, 0)),
                  pl.BlockSpec(memory_space=pl.ANY), pl.BlockSpec(memory_space=pl.ANY)],
        out_specs=row1,
        out_shape=jax.ShapeDtypeStruct(x.shape, F32),
        scratch_shapes=[pltpu.VMEM((2, tm) + tile, F32), pltpu.VMEM((2, tm) + tile, F32),
                        pltpu.SMEM((TOP_K * tm,), jnp.int32),
                        pltpu.SemaphoreType.DMA((2,)), pltpu.SemaphoreType.DMA],
        compiler_params=_cparams(("arbitrary",)),
        name="moe_combine",
    )(x, gate, meta, ln_g, ln_b, dest, yb)


def kernel(x, c, positions, ada_w, ada_b, ln_g, ln_b, conv_w_in, conv_dw, conv_dw_b, conv_ln_g, conv_ln_b, conv_w_out, sgu_w_in, sgu_b_in, sgu_ln_g, sgu_ln_b, sgu_w_s, sgu_b_s, sgu_w_out, gdn_w_in, gdn_conv, gdn_a_log, gdn_dt_bias, gdn_norm_g, gdn_w_out, mla_w_in, mla_q_norm_g, mla_kv_norm_g, mla_w_uq, mla_w_ukv, mla_w_out, ffn_w_in, ffn_w_out, moe_router, moe_w_in, moe_w_out):
    bsz, s, d = x.shape
    mod = _ada_mod(c, ada_w, ada_b).reshape(DEPTH, bsz, 6, 1, d)
    for i in range(DEPTH):
        sh1, sc1, g1, sh2, sc2, g2 = (mod[i, :, t] for t in range(6))
        lg1, lb1 = ln_g[i, 0].reshape(1, d), ln_b[i, 0].reshape(1, d)
        lg2, lb2 = ln_g[i, 1].reshape(1, d), ln_b[i, 1].reshape(1, d)
        j = i // 4
        mixer = i % 4
        if mixer == 0:
            x = _conformer_layer(x, sc1, sh1, g1, lg1, lb1, conv_w_in[j], conv_dw[j], conv_dw_b[j],
                                 conv_ln_g[j], conv_ln_b[j], conv_w_out[j])
        elif mixer == 1:
            x = _sgu_layer(x, sc1, sh1, g1, lg1, lb1, sgu_w_in[j], sgu_b_in[j], sgu_ln_g[j], sgu_ln_b[j],
                           sgu_w_s[j], sgu_b_s[j], sgu_w_out[j])
        elif mixer == 2:
            x = _gdn_layer(x, sc1, sh1, g1, lg1, lb1, gdn_w_in[j], gdn_conv[j], gdn_a_log[j], gdn_dt_bias[j],
                           gdn_norm_g[j], gdn_w_out[j])
        else:
            x = _mla_layer(x, positions, sc1, sh1, g1, lg1, lb1, mla_w_in[j], mla_q_norm_g[j],
                           mla_kv_norm_g[j], mla_w_uq[j], mla_w_ukv[j], mla_w_out[j])
        if i % 2 == 0:
            x = _swiglu_layer(x, sc2, sh2, g2, lg2, lb2, ffn_w_in[i // 2], ffn_w_out[i // 2])
        else:
            x = _moe_layer(x, sc2, sh2, g2, lg2, lb2, moe_router[i // 2], moe_w_in, moe_w_out, i // 2)
    return x
```

```python
import functools
import math

import jax
import jax.numpy as jnp
from jax import lax
from jax.experimental import pallas as pl
from jax.experimental.pallas import tpu as pltpu

F32 = jnp.float32
BF16 = jnp.bfloat16

D_MODEL = 1024
DEPTH = 4
ALPHA = (2 * DEPTH) ** 0.25

CONV_WIDTH = 31
CONV_HALO = 32
CONV_ROWS = 128
SGU_CHUNK = 128
SGU_GROUPS = 8
SGU_HALF = 2 * D_MODEL
GDN_HEADS = 8
GDN_DK = 128
GDN_DV = 128
GDN_CONV = 4
GDN_TILE = 256
GDN_TAIL = 8
MLA_HEADS = 8
MLA_Q_RANK = 512
MLA_KV_RANK = 256
MLA_NOPE = 128
MLA_ROPE = 64
MLA_V = 128
ROPE_THETA = 10000.0
FFN_HIDDEN = 7 * D_MODEL // 2
N_EXPERTS = 8
TOP_K = 2
LANES = 128
SUBLANES = 8
MOE_BLOCK = 1024
FFN_ROW_SPLIT = 2
VMEM_LIMIT = 56 * 1024 * 1024


def _cparams(sem):
    return pltpu.CompilerParams(dimension_semantics=sem, vmem_limit_bytes=VMEM_LIMIT)


def _sigmoid(x):
    return 1.0 / (1.0 + jnp.exp(-x))


def _silu(x):
    return x * _sigmoid(x)


def _ln(z, g, b, eps=1e-5):
    mu = jnp.mean(z, -1, keepdims=True)
    zc = z - mu
    var = jnp.mean(zc * zc, -1, keepdims=True)
    return zc * lax.rsqrt(var + eps) * g + b


def _res_ln(x, y, gate, g, b):
    return _ln(ALPHA * x + (1.0 + gate) * y, g, b)


def _dot(a, b):
    return jnp.dot(a, b, preferred_element_type=F32)


def _dot_nt(a, b):
    return lax.dot_general(a, b, (((1,), (1,)), ((), ())), preferred_element_type=F32)


def _dot_tn(a, b):
    return lax.dot_general(a, b, (((0,), (0,)), ((), ())), preferred_element_type=F32)


def _split_bf16(a, parts):
    out = []
    for _ in range(parts):
        piece = a.astype(BF16)
        out.append(piece)
        a = a - piece.astype(F32)
    return out


def _dot3(a, b):
    (ah, al), (bh, bl) = _split_bf16(a, 2), _split_bf16(b, 2)
    return _dot(ah, bh) + (_dot(ah, bl) + _dot(al, bh))


def _ada_kernel(c_ref, w_ref, b_ref, o_ref):
    cond = _silu(c_ref[...])
    o_ref[0] = _dot3(cond, w_ref[0]) + b_ref[0]


def _ada_mod(c, ada_w, ada_b):
    bsz, d = c.shape
    n_out = ada_w.shape[-1]
    tn = 1024
    return pl.pallas_call(
        _ada_kernel,
        grid=(DEPTH, n_out // tn),
        in_specs=[
            pl.BlockSpec((bsz, d), lambda i, j: (0, 0)),
            pl.BlockSpec((1, d, tn), lambda i, j: (i, 0, j)),
            pl.BlockSpec((1, 1, tn), lambda i, j: (i, 0, j)),
        ],
        out_specs=pl.BlockSpec((1, bsz, tn), lambda i, j: (i, 0, j)),
        out_shape=jax.ShapeDtypeStruct((DEPTH, bsz, n_out), F32),
        compiler_params=_cparams(("parallel", "parallel")),
        name="ada_mod",
    )(c, ada_w, ada_b.reshape(DEPTH, 1, n_out))


def _row_spec(tm, width):
    return pl.BlockSpec((1, tm, width), lambda b, i: (b, i, 0))


def _mod_spec(width=D_MODEL):
    return pl.BlockSpec((1, 1, width), lambda b, i: (b, 0, 0))


def _full_spec(shape):
    nd = len(shape)
    return pl.BlockSpec(shape, lambda b, i: (0,) * nd)


def _outproj_kernel(a_ref, w_ref, x_ref, gate_ref, g_ref, b_ref, o_ref):
    y = _dot(a_ref[0], w_ref[...])
    o_ref[0] = _res_ln(x_ref[0], y, gate_ref[0], g_ref[...], b_ref[...])


def _outproj_ln(a, w, x, gate, ln_g, ln_b, tm=1024):
    bsz, s, k = a.shape
    d = x.shape[-1]
    return pl.pallas_call(
        _outproj_kernel,
        grid=(bsz, s // tm),
        in_specs=[_row_spec(tm, k), _full_spec((k, d)), _row_spec(tm, d), _mod_spec(d),
                  _full_spec((1, d)), _full_spec((1, d))],
        out_specs=_row_spec(tm, d),
        out_shape=jax.ShapeDtypeStruct(x.shape, F32),
        compiler_params=_cparams(("parallel", "parallel")),
        name="outproj_ln",
    )(a, w, x, gate, ln_g, ln_b)


def _conv_in_kernel(x_ref, sc_ref, sh_ref, w_ref, o_ref):
    h = (x_ref[0] * (1.0 + sc_ref[0]) + sh_ref[0]).astype(BF16)
    ag = _dot(h, w_ref[...])
    d = o_ref.shape[-1]
    o_ref[0] = ag[:, :d] * _sigmoid(ag[:, d:])


def _conv_mid_kernel(y_ref, halo_ref, dw_ref, dwb_ref, cg_ref, cb_ref, w_ref,
                     x_ref, gate_ref, g_ref, b_ref, o_ref, ybuf, cbuf, shbuf):
    tm = y_ref.shape[1]
    d = y_ref.shape[2]
    first = pl.program_id(1) == 0
    ybuf[0:CONV_HALO, :] = jnp.where(first, 0.0, halo_ref[0])
    ybuf[CONV_HALO:, :] = y_ref[0]
    rc = CONV_ROWS
    off = CONV_HALO - (CONV_WIDTH - 1)
    def col_block(ci, carry):
        cols = pl.ds(pl.multiple_of(ci * LANES, LANES), LANES)
        for r0 in range(0, tm, rc):
            acc = None
            for r in range(SUBLANES):
                ks = [k for k in range(CONV_WIDTH) if (off + k) % SUBLANES == r]
                span = max((off + k) // SUBLANES for k in ks) * SUBLANES + rc
                shbuf[r, 0:span, :] = ybuf[r0 + r:r0 + r + span, cols]
                for k in ks:
                    q8 = (off + k) // SUBLANES * SUBLANES
                    term = dw_ref[k:k + 1, cols] * shbuf[r, q8:q8 + rc, :]
                    acc = term if acc is None else acc + term
            cbuf[r0:r0 + rc, cols] = acc
        return carry

    lax.fori_loop(0, d // LANES, col_block, 0)
    yc = cbuf[...] + dwb_ref[...]
    yn = _silu(_ln(yc, cg_ref[...], cb_ref[...]))
    yo = _dot(yn.astype(BF16), w_ref[...])
    o_ref[0] = _res_ln(x_ref[0], yo, gate_ref[0], g_ref[...], b_ref[...])


def _conformer_layer(x, sc, sh, gate, ln_g, ln_b, w_in, dw, dw_b, cg, cb, w_out, tm=512, tin=1024):
    bsz, s, d = x.shape
    y = pl.pallas_call(
        _conv_in_kernel,
        grid=(bsz, s // tin),
        in_specs=[_row_spec(tin, d), _mod_spec(), _mod_spec(), _full_spec((d, 2 * d))],
        out_specs=_row_spec(tin, d),
        out_shape=jax.ShapeDtypeStruct((bsz, s, d), F32),
        compiler_params=_cparams(("parallel", "parallel")),
        name="conv_in",
    )(x, sc, sh, w_in.astype(BF16))
    hb = tm // CONV_HALO
    halo_spec = pl.BlockSpec((1, CONV_HALO, d), lambda b, i: (b, jnp.maximum(i * hb - 1, 0), 0))
    dw_pad = jnp.concatenate([dw, jnp.zeros((1, d), F32)], 0)
    return pl.pallas_call(
        _conv_mid_kernel,
        grid=(bsz, s // tm),
        in_specs=[_row_spec(tm, d), halo_spec, _full_spec((CONV_WIDTH + 1, d)), _full_spec((1, d)),
                  _full_spec((1, d)), _full_spec((1, d)), _full_spec((d, d)),
                  _row_spec(tm, d), _mod_spec(), _full_spec((1, d)), _full_spec((1, d))],
        out_specs=_row_spec(tm, d),
        out_shape=jax.ShapeDtypeStruct((bsz, s, d), F32),
        scratch_shapes=[pltpu.VMEM((tm + CONV_HALO, d), F32), pltpu.VMEM((tm, d), F32),
                        pltpu.VMEM((SUBLANES, CONV_ROWS + CONV_HALO, LANES), F32)],
        compiler_params=_cparams(("parallel", "parallel")),
        name="conv_mid",
    )(y, y, dw_pad, dw_b.reshape(1, d), cg.reshape(1, d), cb.reshape(1, d), w_out.astype(BF16),
      x, gate, ln_g, ln_b)


def _swiglu_chunk(hbuf, wa, wb, wo, acc):
    rows = hbuf.shape[0] // FFN_ROW_SPLIT
    groups = [slice(g * rows, (g + 1) * rows) for g in range(FFN_ROW_SPLIT)]
    ab = [(_dot(hbuf[rs, :], wa), _dot(hbuf[rs, :], wb)) for rs in groups]
    acts = [(_silu(a) * b).astype(BF16) for a, b in ab]
    for rs, act in zip(groups, acts):
        acc[rs, :] += _dot(act, wo)


def _swiglu_kernel(x_ref, sc_ref, sh_ref, wa_ref, wb_ref, wo_ref, gate_ref, g_ref, b_ref,
                   o_ref, hbuf, acc):
    j = pl.program_id(2)

    @pl.when(j == 0)
    def _():
        hbuf[...] = (x_ref[0] * (1.0 + sc_ref[0]) + sh_ref[0]).astype(BF16)
        acc[...] = jnp.zeros_like(acc)

    _swiglu_chunk(hbuf, wa_ref[...], wb_ref[...], wo_ref[...], acc)

    @pl.when(j == pl.num_programs(2) - 1)
    def _():
        o_ref[0] = _res_ln(x_ref[0], acc[...], gate_ref[0], g_ref[...], b_ref[...])


def _swiglu_layer(x, sc, sh, gate, ln_g, ln_b, w_in, w_out, tm=512, th=1792):
    bsz, s, d = x.shape
    hid = w_out.shape[0]
    nj = hid // th
    row = pl.BlockSpec((1, tm, d), lambda b, i, j: (b, i, 0))
    mod = pl.BlockSpec((1, 1, d), lambda b, i, j: (b, 0, 0))
    vec = pl.BlockSpec((1, d), lambda b, i, j: (0, 0))
    w_in_b = w_in.astype(BF16)
    return pl.pallas_call(
        _swiglu_kernel,
        grid=(bsz, s // tm, nj),
        in_specs=[row, mod, mod,
                  pl.BlockSpec((d, th), lambda b, i, j: (0, j)),
                  pl.BlockSpec((d, th), lambda b, i, j: (0, j + nj)),
                  pl.BlockSpec((th, d), lambda b, i, j: (j, 0)),
                  mod, vec, vec],
        out_specs=row,
        out_shape=jax.ShapeDtypeStruct(x.shape, F32),
        scratch_shapes=[pltpu.VMEM((tm, d), BF16), pltpu.VMEM((tm, d), F32)],
        compiler_params=_cparams(("parallel", "parallel", "arbitrary")),
        name="swiglu",
    )(x, sc, sh, w_in_b, w_in_b, w_out.astype(BF16), gate, ln_g, ln_b)


def _sgu_kernel(x_ref, sc_ref, sh_ref, wi_ref, bi_ref, vg_ref, vb_ref, ws_ref, bs_ref, wo_ref,
                gate_ref, g_ref, b_ref, o_ref, gbuf):
    tm = x_ref.shape[1]
    h = (x_ref[0] * (1.0 + sc_ref[0]) + sh_ref[0]).astype(BF16)
    z = _dot(h, wi_ref[...]) + bi_ref[...]
    z = 0.5 * z * (1.0 + lax.erf(z * (2.0 ** -0.5)))
    u = z[:, :SGU_HALF]
    v = _ln(z[:, SGU_HALF:], vg_ref[...], vb_ref[...]).astype(BF16)
    gw = SGU_HALF // SGU_GROUPS
    ri = lax.broadcasted_iota(jnp.int32, (SGU_CHUNK, SGU_CHUNK), 0)
    ci = lax.broadcasted_iota(jnp.int32, (SGU_CHUNK, SGU_CHUNK), 1)
    causal = ci <= ri
    for g in range(SGU_GROUPS):
        wsg = jnp.where(causal, ws_ref[g], 0.0).astype(BF16)
        bias = bs_ref[:, g:g + 1]
        for c in range(tm // SGU_CHUNK):
            rows = slice(c * SGU_CHUNK, (c + 1) * SGU_CHUNK)
            cols = slice(g * gw, (g + 1) * gw)
            sv = _dot(wsg, v[rows, cols]) + bias
            gbuf[rows, cols] = (u[rows, cols] * sv).astype(BF16)
    yo = _dot(gbuf[...], wo_ref[...])
    o_ref[0] = _res_ln(x_ref[0], yo, gate_ref[0], g_ref[...], b_ref[...])


def _sgu_layer(x, sc, sh, gate, ln_g, ln_b, w_in, b_in, vg, vb, w_s, b_s, w_out, tm=512):
    bsz, s, d = x.shape
    return pl.pallas_call(
        _sgu_kernel,
        grid=(bsz, s // tm),
        in_specs=[_row_spec(tm, d), _mod_spec(), _mod_spec(),
                  _full_spec((d, 2 * SGU_HALF)), _full_spec((1, 2 * SGU_HALF)),
                  _full_spec((1, SGU_HALF)), _full_spec((1, SGU_HALF)),
                  _full_spec((SGU_GROUPS, SGU_CHUNK, SGU_CHUNK)), _full_spec((SGU_CHUNK, SGU_GROUPS)),
                  _full_spec((SGU_HALF, d)), _mod_spec(), _full_spec((1, d)), _full_spec((1, d))],
        out_specs=_row_spec(tm, d),
        out_shape=jax.ShapeDtypeStruct(x.shape, F32),
        scratch_shapes=[pltpu.VMEM((tm, SGU_HALF), BF16)],
        compiler_params=_cparams(("parallel", "parallel")),
        name="sgu",
    )(x, sc, sh, w_in.astype(BF16), b_in.reshape(1, -1), vg.reshape(1, -1), vb.reshape(1, -1),
      w_s, b_s.T, w_out.astype(BF16), gate, ln_g, ln_b)


def _gdn_in_kernel(x_ref, sc_ref, sh_ref, wq_ref, wz_ref, wba_ref, alog_ref, dtb_ref,
                   qkv_ref, z_ref, bg_ref, gct_ref):
    tm = x_ref.shape[1]
    c = GDN_TILE
    hf = x_ref[0] * (1.0 + sc_ref[0]) + sh_ref[0]
    h = hf.astype(BF16)
    qkv_ref[0] = _dot(h, wq_ref[...])
    z_ref[0] = _dot(h, wz_ref[...]).astype(BF16)
    ba = _dot3(hf, wba_ref[...])
    bg_ref[0, :, :LANES] = _sigmoid(ba)
    a_in = ba + dtb_ref[...]
    softplus = jnp.maximum(a_in, 0.0) + jnp.log1p(jnp.exp(-jnp.abs(a_in)))
    g_all = -jnp.exp(alog_ref[...]) * softplus
    ri = lax.broadcasted_iota(jnp.int32, (c, c), 0)
    ci = lax.broadcasted_iota(jnp.int32, (c, c), 1)
    tril = (ci <= ri).astype(BF16)
    for r0 in range(0, tm, c):
        g_hi, g_mid, g_lo = _split_bf16(g_all[r0:r0 + c], 3)
        gc = _dot(tril, g_hi) + (_dot(tril, g_mid) + _dot(tril, g_lo))
        bg_ref[0, r0:r0 + c, LANES:] = gc
        gct_ref[0, :, r0:r0 + c] = gc.T[GDN_HEADS:2 * GDN_HEADS]


def _gdn_chunk_kernel(q_ref, k_ref, v_ref, cwq_ref, cwk_ref, cwv_ref, bg_ref, gct_ref,
                      z_ref, ng_ref, o_ref, tail, cbuf, state, *, hp):
    c = q_ref.shape[1]

    @pl.when(pl.program_id(2) == 0)
    def _():
        tail[...] = jnp.zeros_like(tail)
        state[...] = jnp.zeros_like(state)

    off = GDN_TAIL - (GDN_CONV - 1)
    convd = []
    for idx, (ref, cw) in enumerate(((q_ref, cwq_ref), (k_ref, cwk_ref), (v_ref, cwv_ref))):
        cbuf[idx, 0:GDN_TAIL, :] = tail[idx]
        cbuf[idx, GDN_TAIL:, :] = ref[0]
        tail[idx] = ref[0, c - GDN_TAIL:c, :]
        acc = cw[0:1, :] * cbuf[idx, pl.ds(off, c), :]
        for t in range(1, GDN_CONV):
            acc = acc + cw[t:t + 1, :] * cbuf[idx, pl.ds(off + t, c), :]
        convd.append(_silu(acc))
    qc_all, kc_all, v_all = convd

    lane = lax.broadcasted_iota(jnp.int32, (1, LANES), 1)
    sub = lax.broadcasted_iota(jnp.int32, (GDN_HEADS, 1), 0)
    ri = lax.broadcasted_iota(jnp.int32, (c, c), 0)
    ci = lax.broadcasted_iota(jnp.int32, (c, c), 1)
    causal = ci <= ri
    strict = ci < ri
    xor = ri ^ ci
    eye = (ri == ci).astype(F32)
    bg = bg_ref[0]
    gct = gct_ref[0]
    z_all = z_ref[0]
    heads = range(hp)
    q, k, v, kb, beta, gc, dmat, a_mat = ([None] * hp for _ in range(8))
    for t in heads:
        hd = pl.program_id(1) * hp + t
        cols = slice(t * LANES, (t + 1) * LANES)
        qc, kc, v[t] = qc_all[:, cols], kc_all[:, cols], v_all[:, cols]
        q[t] = qc * lax.rsqrt(jnp.sum(qc * qc, -1, keepdims=True) + 1e-6) * (GDN_DK ** -0.5)
        k[t] = kc * lax.rsqrt(jnp.sum(kc * kc, -1, keepdims=True) + 1e-6)
        sel = lane == hd
        beta[t] = jnp.sum(jnp.where(sel, bg[:, :LANES], 0.0), -1, keepdims=True)
        gc[t] = jnp.sum(jnp.where(lane == GDN_HEADS + hd, bg[:, LANES:], 0.0), -1, keepdims=True)
        gc_row = jnp.sum(jnp.where(sub == hd, gct, 0.0), 0, keepdims=True)
        dmat[t] = jnp.where(causal, jnp.exp(jnp.where(causal, gc[t] - gc_row, 0.0)), 0.0)
        kb[t] = k[t].astype(BF16)
    for t in heads:
        a_mat[t] = jnp.where(strict, beta[t] * _dot_nt(kb[t], kb[t]) * dmat[t], 0.0)
    inv = [eye - jnp.where(xor == 1, a_mat[t], 0.0) for t in heads]
    lvl = 1
    while (1 << lvl) < c:
        invb = [inv[t].astype(BF16) for t in heads]
        joins = (xor >> lvl) == 1
        dm = [_dot(invb[t], jnp.where(joins, a_mat[t], 0.0).astype(BF16)) for t in heads]
        inv = [inv[t] - _dot(dm[t].astype(BF16), invb[t]) for t in heads]
        lvl += 1

    egc = [jnp.exp(gc[t]) for t in heads]
    sol = [_dot(inv[t].astype(BF16),
                jnp.concatenate([v[t] * beta[t], k[t] * (beta[t] * egc[t])], -1).astype(BF16)) for t in heads]
    qk = [jnp.where(causal, _dot_nt(q[t].astype(BF16), kb[t]) * dmat[t], 0.0).astype(BF16) for t in heads]
    s_prev = [state[t] for t in heads]
    sb = [s_prev[t].astype(BF16) for t in heads]
    vnb = [(sol[t][:, :GDN_DV] - _dot(sol[t][:, GDN_DV:].astype(BF16), sb[t])).astype(BF16) for t in heads]
    o = [_dot((q[t] * egc[t]).astype(BF16), sb[t]) + _dot(qk[t], vnb[t]) for t in heads]
    for t in heads:
        g_last = gc[t][c - 1:c, :]
        k_dec = (k[t] * jnp.exp(g_last - gc[t])).astype(BF16)
        state[t] = s_prev[t] * jnp.exp(g_last) + _dot_tn(k_dec, vnb[t])
    for t in heads:
        cols = slice(t * LANES, (t + 1) * LANES)
        on = o[t] * lax.rsqrt(jnp.mean(o[t] * o[t], -1, keepdims=True) + 1e-6) * ng_ref[...]
        o_ref[0, :, cols] = (on * _silu(z_all[:, cols].astype(F32))).astype(BF16)


def _gdn_layer(x, sc, sh, gate, ln_g, ln_b, w_in, conv_w, a_log, dt_bias, norm_g, w_out, tm=512, hp=4):
    bsz, s, d = x.shape
    nh = GDN_HEADS
    nqkv = 3 * nh * GDN_DK
    w_qkv = w_in[:, :nqkv].astype(BF16)
    w_z = w_in[:, nqkv:nqkv + nh * GDN_DV].astype(BF16)
    w_b = w_in[:, nqkv + nh * GDN_DV:nqkv + nh * GDN_DV + nh]
    w_a = w_in[:, nqkv + nh * GDN_DV + nh:]
    w_ba = jnp.concatenate([w_b, w_a, jnp.zeros((d, LANES - 2 * nh), F32)], 1)
    head_pad, lane_pad = jnp.zeros((nh,), F32), jnp.zeros((LANES - 2 * nh,), F32)
    alog = jnp.concatenate([head_pad, a_log, lane_pad]).reshape(1, LANES)
    dtb = jnp.concatenate([head_pad, dt_bias, lane_pad]).reshape(1, LANES)
    qkv, z, bg, gct = pl.pallas_call(
        _gdn_in_kernel,
        grid=(bsz, s // tm),
        in_specs=[_row_spec(tm, d), _mod_spec(), _mod_spec(), _full_spec((d, nqkv)),
                  _full_spec((d, nh * GDN_DV)), _full_spec((d, LANES)),
                  _full_spec((1, LANES)), _full_spec((1, LANES))],
        out_specs=[_row_spec(tm, nqkv), _row_spec(tm, nh * GDN_DV), _row_spec(tm, 2 * LANES),
                   pl.BlockSpec((1, nh, tm), lambda b, i: (b, 0, i))],
        out_shape=[jax.ShapeDtypeStruct((bsz, s, nqkv), F32),
                   jax.ShapeDtypeStruct((bsz, s, nh * GDN_DV), BF16),
                   jax.ShapeDtypeStruct((bsz, s, 2 * LANES), F32),
                   jax.ShapeDtypeStruct((bsz, nh, s), F32)],
        compiler_params=_cparams(("parallel", "parallel")),
        name="gdn_in",
    )(x, sc, sh, w_qkv, w_z, w_ba, alog, dtb)

    c = GDN_TILE
    wide = hp * LANES
    ng = nh // hp

    def col(base):
        return pl.BlockSpec((1, c, wide), lambda b, h, i: (b, i, base + h))

    def cw(base):
        return pl.BlockSpec((GDN_CONV, wide), lambda b, h, i: (0, base + h))

    og = pl.pallas_call(
        functools.partial(_gdn_chunk_kernel, hp=hp),
        grid=(bsz, ng, s // c),
        in_specs=[col(0), col(ng), col(2 * ng), cw(0), cw(ng), cw(2 * ng),
                  pl.BlockSpec((1, c, 2 * LANES), lambda b, h, i: (b, i, 0)),
                  pl.BlockSpec((1, nh, c), lambda b, h, i: (b, 0, i)),
                  col(0), pl.BlockSpec((1, LANES), lambda b, h, i: (0, 0))],
        out_specs=col(0),
        out_shape=jax.ShapeDtypeStruct((bsz, s, nh * GDN_DV), BF16),
        scratch_shapes=[pltpu.VMEM((3, GDN_TAIL, wide), F32), pltpu.VMEM((3, c + GDN_TAIL, wide), F32),
                        pltpu.VMEM((hp, GDN_DK, GDN_DV), F32)],
        compiler_params=_cparams(("parallel", "parallel", "arbitrary")),
        name="gdn_chunk",
    )(qkv, qkv, qkv, conv_w, conv_w, conv_w, bg, gct, z, norm_g.reshape(1, LANES))
    return _outproj_ln(og, w_out.astype(BF16), x, gate, ln_g, ln_b)


def _mla_in_kernel(x_ref, sc_ref, sh_ref, pos_ref, wi_ref, qg_ref, kg_ref, wqn_ref, wqr_ref, wqt_ref,
                   wkv_ref, qn_ref, qr_ref, kn_ref, kr_ref, v_ref):
    h = (x_ref[0] * (1.0 + sc_ref[0]) + sh_ref[0]).astype(BF16)
    p = _dot(h, wi_ref[...])
    cq = p[:, :MLA_Q_RANK]
    ckv = p[:, MLA_Q_RANK:MLA_Q_RANK + MLA_KV_RANK]
    kr = p[:, MLA_Q_RANK + MLA_KV_RANK:MLA_Q_RANK + MLA_KV_RANK + LANES]
    krt = p[:, MLA_Q_RANK + MLA_KV_RANK + LANES:]
    cqn = (cq * lax.rsqrt(jnp.mean(cq * cq, -1, keepdims=True) + 1e-6) * qg_ref[...]).astype(BF16)
    ckn = (ckv * lax.rsqrt(jnp.mean(ckv * ckv, -1, keepdims=True) + 1e-6) * kg_ref[...]).astype(BF16)
    lane = lax.broadcasted_iota(jnp.int32, (1, LANES), 1)
    half = MLA_ROPE // 2
    fidx = (lane % half).astype(F32)
    inv_freq = jnp.exp(fidx * (-math.log(ROPE_THETA) / half))
    ang = pos_ref[0].astype(F32) * inv_freq
    live = lane < MLA_ROPE
    cos = jnp.where(live, jnp.cos(ang), 0.0)
    sin = jnp.where(live, jnp.sin(ang), 0.0)
    scale = (MLA_NOPE + MLA_ROPE) ** -0.5 * math.log2(math.e)
    qn_ref[0] = (_dot(cqn, wqn_ref[...]) * scale).astype(BF16)
    qr = _dot(cqn, wqr_ref[...])
    qrt = _dot(cqn, wqt_ref[...])
    for hd in range(MLA_HEADS):
        cs = slice(hd * LANES, (hd + 1) * LANES)
        qr_ref[0, :, cs] = ((qr[:, cs] * cos + qrt[:, cs] * sin) * scale).astype(BF16)
    kr_ref[0] = (kr * cos + krt * sin).astype(BF16)
    kv = _dot(ckn, wkv_ref[...])
    nk = MLA_HEADS * MLA_NOPE
    kn_ref[0] = kv[:, :nk].astype(BF16)
    ones = jnp.ones((kv.shape[0], LANES), BF16)
    for hd in range(MLA_HEADS):
        v_ref[0, :, 2 * hd * LANES:(2 * hd + 1) * LANES] = kv[:, nk + hd * MLA_V:nk + (hd + 1) * MLA_V].astype(BF16)
        v_ref[0, :, (2 * hd + 1) * LANES:(2 * hd + 2) * LANES] = ones


def _mla_attn_kernel(qn_ref, qr_ref, kn_ref, kr_ref, v_ref, o_ref, qbuf, m_sc, acc_sc, *, hp):
    tq = qn_ref.shape[1]
    tk = tq
    i = pl.program_id(2)
    for h in range(hp):
        cols = slice(h * LANES, (h + 1) * LANES)
        qbuf[h, :, :LANES] = qn_ref[0, :, cols]
        qbuf[h, :, LANES:] = qr_ref[0, :, cols]
    m_sc[...] = jnp.full_like(m_sc, -jnp.inf)
    acc_sc[...] = jnp.zeros_like(acc_sc)

    def step(j, masked):
        r0 = pl.multiple_of(j * tk, tk)
        krt = kr_ref[0, pl.ds(r0, tk), :]
        if masked:
            ri = lax.broadcasted_iota(jnp.int32, (tq, tk), 0)
            ci = lax.broadcasted_iota(jnp.int32, (tq, tk), 1)
            keep = ci <= ri
        scs, ps, alphas = [], [], []
        for h in range(hp):
            cols = slice(h * LANES, (h + 1) * LANES)
            kt = jnp.concatenate([kn_ref[0, pl.ds(r0, tk), cols], krt], -1)
            scs.append(_dot_nt(qbuf[h], kt))
        for h in range(hp):
            sc = jnp.where(keep, scs[h], -jnp.inf) if masked else scs[h]
            m_prev = m_sc[h]
            m_new = jnp.maximum(m_prev, jnp.max(sc, -1, keepdims=True))
            alphas.append(jnp.exp2(m_prev - m_new))
            m_sc[h] = m_new
            ps.append(jnp.concatenate(
                [jnp.exp2(sc[:, c0:c0 + LANES] - m_new) for c0 in range(0, tk, LANES)], -1).astype(BF16))
        for h in range(hp):
            vcols = slice(2 * h * LANES, (2 * h + 2) * LANES)
            alpha2 = jnp.concatenate([alphas[h], alphas[h]], -1)
            acc_sc[h] = alpha2 * acc_sc[h] + _dot(ps[h], v_ref[0, pl.ds(r0, tk), vcols])

    def body(j, carry):
        step(j, False)
        return carry

    lax.fori_loop(0, i, body, 0)
    step(i, True)
    for h in range(hp):
        o_ref[0, :, h * LANES:(h + 1) * LANES] = (acc_sc[h, :, :LANES] / acc_sc[h, :, LANES:]).astype(BF16)


def _mla_layer(x, positions, sc, sh, gate, ln_g, ln_b, w_in, q_norm_g, kv_norm_g, w_uq, w_ukv, w_out,
               tm=1024, tq=512, hp=4):
    bsz, s, d = x.shape
    nh = MLA_HEADS
    half = MLA_ROPE // 2
    rot = jnp.concatenate([jnp.arange(half, MLA_ROPE), jnp.arange(half)])
    sign = jnp.concatenate([-jnp.ones((half,), F32), jnp.ones((half,), F32)])
    zpad = jnp.zeros((d, LANES - MLA_ROPE), F32)
    w_kr = w_in[:, MLA_Q_RANK + MLA_KV_RANK:]
    w_in_ext = jnp.concatenate(
        [w_in[:, :MLA_Q_RANK + MLA_KV_RANK], w_kr, zpad, w_kr[:, rot] * sign, zpad], 1).astype(BF16)
    wq = w_uq.reshape(MLA_Q_RANK, nh, MLA_NOPE + MLA_ROPE)
    w_qn = wq[:, :, :MLA_NOPE].reshape(MLA_Q_RANK, nh * MLA_NOPE).astype(BF16)
    wq_r = wq[:, :, MLA_NOPE:]
    hpad = jnp.zeros((MLA_Q_RANK, nh, LANES - MLA_ROPE), F32)
    w_qr = jnp.concatenate([wq_r, hpad], -1).reshape(MLA_Q_RANK, nh * LANES).astype(BF16)
    w_qt = jnp.concatenate([wq_r[:, :, rot] * sign, hpad], -1).reshape(MLA_Q_RANK, nh * LANES).astype(BF16)
    wkv = w_ukv.reshape(MLA_KV_RANK, nh, MLA_NOPE + MLA_V)
    w_kv = jnp.concatenate([wkv[:, :, :MLA_NOPE].reshape(MLA_KV_RANK, nh * MLA_NOPE),
                            wkv[:, :, MLA_NOPE:].reshape(MLA_KV_RANK, nh * MLA_V)], 1).astype(BF16)
    n_in = w_in_ext.shape[1]
    wide = nh * LANES
    qn, qr, kn, kr, v = pl.pallas_call(
        _mla_in_kernel,
        grid=(bsz, s // tm),
        in_specs=[_row_spec(tm, d), _mod_spec(), _mod_spec(), _row_spec(tm, 1),
                  _full_spec((d, n_in)), _full_spec((1, MLA_Q_RANK)), _full_spec((1, MLA_KV_RANK)),
                  _full_spec((MLA_Q_RANK, wide)), _full_spec((MLA_Q_RANK, wide)),
                  _full_spec((MLA_Q_RANK, wide)), _full_spec((MLA_KV_RANK, 2 * wide))],
        out_specs=[_row_spec(tm, wide), _row_spec(tm, wide), _row_spec(tm, wide), _row_spec(tm, LANES),
                   _row_spec(tm, 2 * wide)],
        out_shape=[jax.ShapeDtypeStruct((bsz, s, wide), BF16), jax.ShapeDtypeStruct((bsz, s, wide), BF16),
                   jax.ShapeDtypeStruct((bsz, s, wide), BF16), jax.ShapeDtypeStruct((bsz, s, LANES), BF16),
                   jax.ShapeDtypeStruct((bsz, s, 2 * wide), BF16)],
        compiler_params=_cparams(("parallel", "parallel")),
        name="mla_in",
    )(x, sc, sh, positions.reshape(bsz, s, 1), w_in_ext, q_norm_g.reshape(1, -1), kv_norm_g.reshape(1, -1),
      w_qn, w_qr, w_qt, w_kv)

    gw = hp * LANES
    qspec = pl.BlockSpec((1, tq, gw), lambda b, h, i: (b, i, h))
    kspec = pl.BlockSpec((1, s, gw), lambda b, h, i: (b, 0, h))
    o = pl.pallas_call(
        functools.partial(_mla_attn_kernel, hp=hp),
        grid=(bsz, nh // hp, s // tq),
        in_specs=[qspec, qspec, kspec, pl.BlockSpec((1, s, LANES), lambda b, h, i: (b, 0, 0)),
                  pl.BlockSpec((1, s, 2 * gw), lambda b, h, i: (b, 0, h))],
        out_specs=qspec,
        out_shape=jax.ShapeDtypeStruct((bsz, s, wide), BF16),
        scratch_shapes=[pltpu.VMEM((hp, tq, 2 * LANES), BF16), pltpu.VMEM((hp, tq, LANES), F32),
                        pltpu.VMEM((hp, tq, 2 * MLA_V), F32)],
        compiler_params=_cparams(("parallel", "parallel", "arbitrary")),
        name="mla_attn",
    )(qn, qr, kn, kr, v)
    return _outproj_ln(o, w_out.astype(BF16), x, gate, ln_g, ln_b)


def _router_kernel(x_ref, sc_ref, sh_ref, wr_ref, meta_ref, cnt_ref, run):
    tm = x_ref.shape[1]

    @pl.when(pl.program_id(0) == 0)
    def _():
        run[...] = jnp.zeros_like(run)

    h = x_ref[0] * (1.0 + sc_ref[0]) + sh_ref[0]
    lane = lax.broadcasted_iota(jnp.int32, (tm, LANES), 1)
    lane_f = lane.astype(F32)
    logits = jnp.where(lane < N_EXPERTS, _dot3(h, wr_ref[...]), -jnp.inf)
    m1 = jnp.max(logits, -1, keepdims=True)
    i1 = jnp.min(jnp.where(logits == m1, lane_f, float(LANES)), -1, keepdims=True)
    oh1 = lane_f == i1
    rest = jnp.where(oh1, -jnp.inf, logits)
    m2 = jnp.max(rest, -1, keepdims=True)
    i2 = jnp.min(jnp.where(rest == m2, lane_f, float(LANES)), -1, keepdims=True)
    oh2 = lane_f == i2
    e21 = jnp.exp(m2 - m1)
    g1 = 1.0 / (1.0 + e21)
    g2 = e21 / (1.0 + e21)
    cnt = oh1.astype(F32) + oh2.astype(F32)
    ri = lax.broadcasted_iota(jnp.int32, (tm, tm), 0)
    ci = lax.broadcasted_iota(jnp.int32, (tm, tm), 1)
    before = _dot((ci < ri).astype(BF16), cnt.astype(BF16)) + run[...]
    r1 = jnp.sum(jnp.where(oh1, before, 0.0), -1, keepdims=True)
    r2 = jnp.sum(jnp.where(oh2, before, 0.0), -1, keepdims=True)
    meta = jnp.zeros((tm, LANES), F32)
    for k, col in enumerate((i1, i2, g1, g2, r1, r2)):
        meta = jnp.where(lane == k, col, meta)
    meta_ref[...] = meta
    run[...] = run[...] + jnp.sum(cnt, 0, keepdims=True)
    cnt_ref[...] = run[...]


def _dispatch_kernel(tbl_sm, x_ref, sc_ref, sh_ref, dest_hbm, xb_hbm, hbuf, zbuf, idx_sm, rsem, sem, isem, *,
                     first_tail_block):
    tm = x_ref.shape[1]
    i = pl.program_id(0)
    last = pl.num_programs(0) - 1
    slot = i % 2
    hslot = hbuf.at[slot]

    def wait_rows(s):
        for _ in range(TOP_K):
            pltpu.make_async_copy(hbuf.at[s], xb_hbm.at[pl.ds(0, tm)], rsem.at[s]).wait()

    icp = pltpu.make_async_copy(dest_hbm.at[i], idx_sm, isem)
    icp.start()

    @pl.when(i >= 2)
    def _():
        wait_rows(slot)

    hslot[...] = x_ref[0] * (1.0 + sc_ref[0]) + sh_ref[0]
    icp.wait()

    def issue(r, carry):
        pltpu.make_async_copy(hslot.at[pl.ds(r, 1)], xb_hbm.at[pl.ds(idx_sm[2 * r], 1)], rsem.at[slot]).start()
        pltpu.make_async_copy(hslot.at[pl.ds(r, 1)], xb_hbm.at[pl.ds(idx_sm[2 * r + 1], 1)], rsem.at[slot]).start()
        return carry

    lax.fori_loop(0, tm, issue, 0, unroll=8)

    @pl.when(i == last)
    def _():
        @pl.when(i >= 1)
        def _():
            wait_rows(1 - slot)

        wait_rows(slot)
        zbuf[...] = jnp.zeros_like(zbuf)
        n_blocks = xb_hbm.shape[0] // MOE_BLOCK

        def pad_copies(e):
            start = tbl_sm[e]
            end = tbl_sm[N_EXPERTS + e]
            n1 = (-start) & (SUBLANES - 1)
            a0 = start + n1
            l8 = end - a0
            out = []
            for r in range(SUBLANES - 1):
                out.append((r < n1, pltpu.make_async_copy(zbuf.at[pl.ds(r, 1)], xb_hbm.at[pl.ds(start + r, 1)], sem)))
            sz = MOE_BLOCK // 2
            while sz >= SUBLANES:
                off = pl.multiple_of(a0 + (l8 & ~(2 * sz - 1)), SUBLANES)
                out.append(((l8 & sz) != 0,
                            pltpu.make_async_copy(zbuf.at[pl.ds(0, sz)], xb_hbm.at[pl.ds(off, sz)], sem)))
                sz //= 2
            return out

        def tail_copies():
            nvalid = tbl_sm[2 * N_EXPERTS]
            return [(bi >= nvalid,
                     pltpu.make_async_copy(zbuf, xb_hbm.at[pl.ds(bi * MOE_BLOCK, MOE_BLOCK)], sem))
                    for bi in range(first_tail_block, n_blocks)]

        def start_all(e, carry):
            for pred, cp in pad_copies(e):
                pl.when(pred)(cp.start)
            return carry

        def wait_all(e, carry):
            for pred, cp in pad_copies(e):
                pl.when(pred)(cp.wait)
            return carry

        lax.fori_loop(0, N_EXPERTS, start_all, 0)
        for pred, cp in tail_copies():
            pl.when(pred)(cp.start)
        lax.fori_loop(0, N_EXPERTS, wait_all, 0)
        for pred, cp in tail_copies():
            pl.when(pred)(cp.wait)


def _expert_kernel(be_sm, nv_sm, x_ref, wa_ref, wb_ref, wo_ref, o_ref, hbuf, acc):
    i = pl.program_id(0)
    j = pl.program_id(1)

    @pl.when((i >= nv_sm[0]) & (j == pl.num_programs(1) - 1))
    def _():
        o_ref[...] = jnp.zeros_like(o_ref)

    @pl.when(i < nv_sm[0])
    def _():
        @pl.when(j == 0)
        def _():
            hbuf[...] = x_ref[...].astype(BF16)
            acc[...] = jnp.zeros_like(acc)

        _swiglu_chunk(hbuf, wa_ref[0, 0].astype(BF16), wb_ref[0, 0].astype(BF16), wo_ref[0, 0].astype(BF16), acc)

        @pl.when(j == pl.num_programs(1) - 1)
        def _():
            o_ref[...] = acc[...]


def _combine_kernel(x_ref, gate_ref, meta_ref, g_ref, b_ref, dest_hbm, yb_hbm, o_ref,
                    y0, y1, idx_sm, sem, isem):
    tm = x_ref.shape[1]
    i = pl.program_id(0)
    slot = i % 2

    def gather(tile, s):
        icp = pltpu.make_async_copy(dest_hbm.at[tile], idx_sm, isem)
        icp.start()
        icp.wait()
        d0, d1 = y0.at[s], y1.at[s]

        def issue(r, carry):
            pltpu.make_async_copy(yb_hbm.at[pl.ds(idx_sm[2 * r], 1)], d0.at[pl.ds(r, 1)], sem.at[s]).start()
            pltpu.make_async_copy(yb_hbm.at[pl.ds(idx_sm[2 * r + 1], 1)], d1.at[pl.ds(r, 1)], sem.at[s]).start()
            return carry

        lax.fori_loop(0, tm, issue, 0, unroll=8)

    @pl.when(i == 0)
    def _():
        gather(0, 0)

    @pl.when(i + 1 < pl.num_programs(0))
    def _():
        gather(i + 1, 1 - slot)

    for buf in (y0, y1):
        pltpu.make_async_copy(yb_hbm.at[pl.ds(0, tm)], buf.at[slot], sem.at[slot]).wait()
    meta = meta_ref[...]
    y = meta[:, 2:3] * y0[slot] + meta[:, 3:4] * y1[slot]
    o_ref[0] = _res_ln(x_ref[0], y, gate_ref[0], g_ref[...], b_ref[...])


def _moe_layer(x, sc, sh, gate, ln_g, ln_b, router, w_in, w_out, layer, tm=512, tr=512, th=512):
    bsz, s, d = x.shape
    n = bsz * s
    nt = n // tm
    spt = s // tm
    hid = w_out.shape[-2]
    nj = hid // th
    min_blocks = n * TOP_K // MOE_BLOCK
    nb = min_blocks + N_EXPERTS
    tile = (d,)

    row1 = pl.BlockSpec((1, tm, d), lambda i: (i // spt, i % spt, 0))
    mod1 = pl.BlockSpec((1, 1, d), lambda i: (i // spt, 0, 0))
    slab1 = pl.BlockSpec((tm, LANES), lambda i: (i, 0))
    w_r = jnp.concatenate([router, jnp.zeros((d, LANES - N_EXPERTS), F32)], 1)
    rpt = s // tr
    meta, counts = pl.pallas_call(
        _router_kernel,
        grid=(n // tr,),
        in_specs=[pl.BlockSpec((1, tr, d), lambda i: (i // rpt, i % rpt, 0)),
                  pl.BlockSpec((1, 1, d), lambda i: (i // rpt, 0, 0)),
                  pl.BlockSpec((1, 1, d), lambda i: (i // rpt, 0, 0)),
                  pl.BlockSpec((d, LANES), lambda i: (0, 0))],
        out_specs=[pl.BlockSpec((tr, LANES), lambda i: (i, 0)), pl.BlockSpec((1, LANES), lambda i: (0, 0))],
        out_shape=[jax.ShapeDtypeStruct((n, LANES), F32), jax.ShapeDtypeStruct((1, LANES), F32)],
        scratch_shapes=[pltpu.VMEM((1, LANES), F32)],
        compiler_params=_cparams(("arbitrary",)),
        name="moe_router",
    )(x, sc, sh, w_r)

    cnt = counts[0, :N_EXPERTS].astype(jnp.int32)
    nblk = (cnt + MOE_BLOCK - 1) // MOE_BLOCK
    ends = jnp.cumsum(nblk)
    first_row = (ends - nblk) * MOE_BLOCK
    nvalid = ends[-1]
    eid = meta[:, :TOP_K].astype(jnp.int32)
    rank = meta[:, 4:4 + TOP_K].astype(jnp.int32)
    base = jnp.sum(jnp.where(eid[:, :, None] == jnp.arange(N_EXPERTS), first_row, 0), -1)
    dest = (base + rank).reshape(nt, TOP_K * tm)
    tbl = jnp.concatenate([first_row + cnt, ends * MOE_BLOCK, nvalid.reshape(1)]).astype(jnp.int32)
    bi = jnp.minimum(jnp.arange(nb, dtype=jnp.int32), nvalid - 1)
    blk_e = jnp.sum(bi[:, None] >= ends[None, :], -1).astype(jnp.int32)

    row1p = pl.BlockSpec((1, tm, d), lambda i, t: (i // spt, i % spt, 0))
    mod1p = pl.BlockSpec((1, 1, d), lambda i, t: (i // spt, 0, 0))
    xb = pl.pallas_call(
        functools.partial(_dispatch_kernel, first_tail_block=min_blocks),
        grid_spec=pltpu.PrefetchScalarGridSpec(
            num_scalar_prefetch=1,
            grid=(nt,),
            in_specs=[row1p, mod1p, mod1p, pl.BlockSpec(memory_space=pl.ANY)],
            out_specs=pl.BlockSpec(memory_space=pl.ANY),
            scratch_shapes=[pltpu.VMEM((2, tm) + tile, F32), pltpu.VMEM((MOE_BLOCK,) + tile, F32),
                            pltpu.SMEM((TOP_K * tm,), jnp.int32),
                            pltpu.SemaphoreType.DMA((2,)), pltpu.SemaphoreType.DMA, pltpu.SemaphoreType.DMA]),
        out_shape=jax.ShapeDtypeStruct((nb * MOE_BLOCK,) + tile, F32),
        compiler_params=_cparams(("arbitrary",)),
        name="moe_dispatch",
    )(tbl, x, sc, sh, dest)

    last = nj - 1

    def jsel(i, j, nv):
        return jnp.where(i < nv[0], j, last)

    yb = pl.pallas_call(
        _expert_kernel,
        grid_spec=pltpu.PrefetchScalarGridSpec(
            num_scalar_prefetch=2,
            grid=(nb, nj),
            in_specs=[pl.BlockSpec((MOE_BLOCK,) + tile, lambda i, j, be, nv: (i, 0)),
                      pl.BlockSpec((1, 1, d, th), lambda i, j, be, nv: (layer, be[i], 0, jsel(i, j, nv))),
                      pl.BlockSpec((1, 1, d, th), lambda i, j, be, nv: (layer, be[i], 0, jsel(i, j, nv) + nj)),
                      pl.BlockSpec((1, 1, th, d), lambda i, j, be, nv: (layer, be[i], jsel(i, j, nv), 0))],
            out_specs=pl.BlockSpec((MOE_BLOCK,) + tile, lambda i, j, be, nv: (i, 0)),
            scratch_shapes=[pltpu.VMEM((MOE_BLOCK, d), BF16), pltpu.VMEM((MOE_BLOCK, d), F32)]),
        out_shape=jax.ShapeDtypeStruct((nb * MOE_BLOCK,) + tile, F32),
        compiler_params=_cparams(("arbitrary", "arbitrary")),
        name="moe_experts",
    )(blk_e, nvalid.reshape(1).astype(jnp.int32), xb, w_in, w_in, w_out)

    return pl.pallas_call(
        _combine_kernel,
        grid=(nt,),
        in_specs=[row1, mod1, slab1, pl.BlockSpec((1, d), lambda i: (0, 0)), pl.BlockSpec((1, d), lambda i: (0, 0)),
                  pl.BlockSpec(memory_space=pl.ANY), pl.BlockSpec(memory_space=pl.ANY)],
        out_specs=row1,
        out_shape=jax.ShapeDtypeStruct(x.shape, F32),
        scratch_shapes=[pltpu.VMEM((2, tm) + tile, F32), pltpu.VMEM((2, tm) + tile, F32),
                        pltpu.SMEM((TOP_K * tm,), jnp.int32),
                        pltpu.SemaphoreType.DMA((2,)), pltpu.SemaphoreType.DMA],
        compiler_params=_cparams(("arbitrary",)),
        name="moe_combine",
    )(x, gate, meta, ln_g, ln_b, dest, yb)


def kernel(x, c, positions, ada_w, ada_b, ln_g, ln_b, conv_w_in, conv_dw, conv_dw_b, conv_ln_g, conv_ln_b, conv_w_out, sgu_w_in, sgu_b_in, sgu_ln_g, sgu_ln_b, sgu_w_s, sgu_b_s, sgu_w_out, gdn_w_in, gdn_conv, gdn_a_log, gdn_dt_bias, gdn_norm_g, gdn_w_out, mla_w_in, mla_q_norm_g, mla_kv_norm_g, mla_w_uq, mla_w_ukv, mla_w_out, ffn_w_in, ffn_w_out, moe_router, moe_w_in, moe_w_out):
    bsz, s, d = x.shape
    mod = _ada_mod(c, ada_w, ada_b).reshape(DEPTH, bsz, 6, 1, d)
    for i in range(DEPTH):
        sh1, sc1, g1, sh2, sc2, g2 = (mod[i, :, t] for t in range(6))
        lg1, lb1 = ln_g[i, 0].reshape(1, d), ln_b[i, 0].reshape(1, d)
        lg2, lb2 = ln_g[i, 1].reshape(1, d), ln_b[i, 1].reshape(1, d)
        j = i // 4
        mixer = i % 4
        if mixer == 0:
            x = _conformer_layer(x, sc1, sh1, g1, lg1, lb1, conv_w_in[j], conv_dw[j], conv_dw_b[j],
                                 conv_ln_g[j], conv_ln_b[j], conv_w_out[j])
        elif mixer == 1:
            x = _sgu_layer(x, sc1, sh1, g1, lg1, lb1, sgu_w_in[j], sgu_b_in[j], sgu_ln_g[j], sgu_ln_b[j],
                           sgu_w_s[j], sgu_b_s[j], sgu_w_out[j])
        elif mixer == 2:
            x = _gdn_layer(x, sc1, sh1, g1, lg1, lb1, gdn_w_in[j], gdn_conv[j], gdn_a_log[j], gdn_dt_bias[j],
                           gdn_norm_g[j], gdn_w_out[j])
        else:
            x = _mla_layer(x, positions, sc1, sh1, g1, lg1, lb1, mla_w_in[j], mla_q_norm_g[j],
                           mla_kv_norm_g[j], mla_w_uq[j], mla_w_ukv[j], mla_w_out[j])
        if i % 2 == 0:
            x = _swiglu_layer(x, sc2, sh2, g2, lg2, lb2, ffn_w_in[i // 2], ffn_w_out[i // 2])
        else:
            x = _moe_layer(x, sc2, sh2, g2, lg2, lb2, moe_router[i // 2], moe_w_in, moe_w_out, i // 2)
    return x
```

```python
import functools
import math

import jax
import jax.numpy as jnp
from jax import lax
from jax.experimental import pallas as pl
from jax.experimental.pallas import tpu as pltpu

F32 = jnp.float32
BF16 = jnp.bfloat16

D_MODEL = 1024
DEPTH = 4
ALPHA = (2 * DEPTH) ** 0.25

CONV_WIDTH = 31
CONV_HALO = 32
CONV_ROWS = 128
SGU_CHUNK = 128
SGU_GROUPS = 8
SGU_HALF = 2 * D_MODEL
GDN_HEADS = 8
GDN_DK = 128
GDN_DV = 128
GDN_CONV = 4
GDN_TILE = 256
GDN_TAIL = 8
MLA_HEADS = 8
MLA_Q_RANK = 512
MLA_KV_RANK = 256
MLA_NOPE = 128
MLA_ROPE = 64
MLA_V = 128
ROPE_THETA = 10000.0
FFN_HIDDEN = 7 * D_MODEL // 2
N_EXPERTS = 8
TOP_K = 2
LANES = 128
SUBLANES = 8
MOE_BLOCK = 1024
FFN_ROW_SPLIT = 2
VMEM_LIMIT = 56 * 1024 * 1024


def _cparams(sem):
    return pltpu.CompilerParams(dimension_semantics=sem, vmem_limit_bytes=VMEM_LIMIT)


def _sigmoid(x):
    return 1.0 / (1.0 + jnp.exp(-x))


def _silu(x):
    return x * _sigmoid(x)


def _ln(z, g, b, eps=1e-5):
    mu = jnp.mean(z, -1, keepdims=True)
    zc = z - mu
    var = jnp.mean(zc * zc, -1, keepdims=True)
    return zc * lax.rsqrt(var + eps) * g + b


def _res_ln(x, y, gate, g, b):
    return _ln(ALPHA * x + (1.0 + gate) * y, g, b)


def _dot(a, b):
    return jnp.dot(a, b, preferred_element_type=F32)


def _dot_nt(a, b):
    return lax.dot_general(a, b, (((1,), (1,)), ((), ())), preferred_element_type=F32)


def _dot_tn(a, b):
    return lax.dot_general(a, b, (((0,), (0,)), ((), ())), preferred_element_type=F32)


def _split_bf16(a, parts):
    out = []
    for _ in range(parts):
        piece = a.astype(BF16)
        out.append(piece)
        a = a - piece.astype(F32)
    return out


def _dot3(a, b):
    (ah, al), (bh, bl) = _split_bf16(a, 2), _split_bf16(b, 2)
    return _dot(ah, bh) + (_dot(ah, bl) + _dot(al, bh))


def _ada_kernel(c_ref, w_ref, b_ref, o_ref):
    cond = _silu(c_ref[...])
    o_ref[0] = _dot3(cond, w_ref[0]) + b_ref[0]


def _ada_mod(c, ada_w, ada_b):
    bsz, d = c.shape
    n_out = ada_w.shape[-1]
    tn = 1024
    return pl.pallas_call(
        _ada_kernel,
        grid=(DEPTH, n_out // tn),
        in_specs=[
            pl.BlockSpec((bsz, d), lambda i, j: (0, 0)),
            pl.BlockSpec((1, d, tn), lambda i, j: (i, 0, j)),
            pl.BlockSpec((1, 1, tn), lambda i, j: (i, 0, j)),
        ],
        out_specs=pl.BlockSpec((1, bsz, tn), lambda i, j: (i, 0, j)),
        out_shape=jax.ShapeDtypeStruct((DEPTH, bsz, n_out), F32),
        compiler_params=_cparams(("parallel", "parallel")),
        name="ada_mod",
    )(c, ada_w, ada_b.reshape(DEPTH, 1, n_out))


def _row_spec(tm, width):
    return pl.BlockSpec((1, tm, width), lambda b, i: (b, i, 0))


def _mod_spec(width=D_MODEL):
    return pl.BlockSpec((1, 1, width), lambda b, i: (b, 0, 0))


def _full_spec(shape):
    nd = len(shape)
    return pl.BlockSpec(shape, lambda b, i: (0,) * nd)


def _outproj_kernel(a_ref, w_ref, x_ref, gate_ref, g_ref, b_ref, o_ref):
    y = _dot(a_ref[0], w_ref[...])
    o_ref[0] = _res_ln(x_ref[0], y, gate_ref[0], g_ref[...], b_ref[...])


def _outproj_ln(a, w, x, gate, ln_g, ln_b, tm=1024):
    bsz, s, k = a.shape
    d = x.shape[-1]
    return pl.pallas_call(
        _outproj_kernel,
        grid=(bsz, s // tm),
        in_specs=[_row_spec(tm, k), _full_spec((k, d)), _row_spec(tm, d), _mod_spec(d),
                  _full_spec((1, d)), _full_spec((1, d))],
        out_specs=_row_spec(tm, d),
        out_shape=jax.ShapeDtypeStruct(x.shape, F32),
        compiler_params=_cparams(("parallel", "parallel")),
        name="outproj_ln",
    )(a, w, x, gate, ln_g, ln_b)


def _conv_in_kernel(x_ref, sc_ref, sh_ref, w_ref, o_ref):
    h = (x_ref[0] * (1.0 + sc_ref[0]) + sh_ref[0]).astype(BF16)
    ag = _dot(h, w_ref[...])
    d = o_ref.shape[-1]
    o_ref[0] = ag[:, :d] * _sigmoid(ag[:, d:])


def _conv_mid_kernel(y_ref, halo_ref, dw_ref, dwb_ref, cg_ref, cb_ref, w_ref,
                     x_ref, gate_ref, g_ref, b_ref, o_ref, ybuf, cbuf, shbuf):
    tm = y_ref.shape[1]
    d = y_ref.shape[2]
    first = pl.program_id(1) == 0
    ybuf[0:CONV_HALO, :] = jnp.where(first, 0.0, halo_ref[0])
    ybuf[CONV_HALO:, :] = y_ref[0]
    rc = CONV_ROWS
    off = CONV_HALO - (CONV_WIDTH - 1)
    def col_block(ci, carry):
        cols = pl.ds(pl.multiple_of(ci * LANES, LANES), LANES)
        for r0 in range(0, tm, rc):
            acc = None
            for r in range(SUBLANES):
                ks = [k for k in range(CONV_WIDTH) if (off + k) % SUBLANES == r]
                span = max((off + k) // SUBLANES for k in ks) * SUBLANES + rc
                shbuf[r, 0:span, :] = ybuf[r0 + r:r0 + r + span, cols]
                for k in ks:
                    q8 = (off + k) // SUBLANES * SUBLANES
                    term = dw_ref[k:k + 1, cols] * shbuf[r, q8:q8 + rc, :]
                    acc = term if acc is None else acc + term
            cbuf[r0:r0 + rc, cols] = acc
        return carry

    lax.fori_loop(0, d // LANES, col_block, 0)
    yc = cbuf[...] + dwb_ref[...]
    yn = _silu(_ln(yc, cg_ref[...], cb_ref[...]))
    yo = _dot(yn.astype(BF16), w_ref[...])
    o_ref[0] = _res_ln(x_ref[0], yo, gate_ref[0], g_ref[...], b_ref[...])


def _conformer_layer(x, sc, sh, gate, ln_g, ln_b, w_in, dw, dw_b, cg, cb, w_out, tm=512, tin=1024):
    bsz, s, d = x.shape
    y = pl.pallas_call(
        _conv_in_kernel,
        grid=(bsz, s // tin),
        in_specs=[_row_spec(tin, d), _mod_spec(), _mod_spec(), _full_spec((d, 2 * d))],
        out_specs=_row_spec(tin, d),
        out_shape=jax.ShapeDtypeStruct((bsz, s, d), F32),
        compiler_params=_cparams(("parallel", "parallel")),
        name="conv_in",
    )(x, sc, sh, w_in.astype(BF16))
    hb = tm // CONV_HALO
    halo_spec = pl.BlockSpec((1, CONV_HALO, d), lambda b, i: (b, jnp.maximum(i * hb - 1, 0), 0))
    dw_pad = jnp.concatenate([dw, jnp.zeros((1, d), F32)], 0)
    return pl.pallas_call(
        _conv_mid_kernel,
        grid=(bsz, s // tm),
        in_specs=[_row_spec(tm, d), halo_spec, _full_spec((CONV_WIDTH + 1, d)), _full_spec((1, d)),
                  _full_spec((1, d)), _full_spec((1, d)), _full_spec((d, d)),
                  _row_spec(tm, d), _mod_spec(), _full_spec((1, d)), _full_spec((1, d))],
        out_specs=_row_spec(tm, d),
        out_shape=jax.ShapeDtypeStruct((bsz, s, d), F32),
        scratch_shapes=[pltpu.VMEM((tm + CONV_HALO, d), F32), pltpu.VMEM((tm, d), F32),
                        pltpu.VMEM((SUBLANES, CONV_ROWS + CONV_HALO, LANES), F32)],
        compiler_params=_cparams(("parallel", "parallel")),
        name="conv_mid",
    )(y, y, dw_pad, dw_b.reshape(1, d), cg.reshape(1, d), cb.reshape(1, d), w_out.astype(BF16),
      x, gate, ln_g, ln_b)


def _swiglu_chunk(hbuf, wa, wb, wo, acc):
    rows = hbuf.shape[0] // FFN_ROW_SPLIT
    groups = [slice(g * rows, (g + 1) * rows) for g in range(FFN_ROW_SPLIT)]
    ab = [(_dot(hbuf[rs, :], wa), _dot(hbuf[rs, :], wb)) for rs in groups]
    acts = [(_silu(a) * b).astype(BF16) for a, b in ab]
    for rs, act in zip(groups, acts):
        acc[rs, :] += _dot(act, wo)


def _swiglu_kernel(x_ref, sc_ref, sh_ref, wa_ref, wb_ref, wo_ref, gate_ref, g_ref, b_ref,
                   o_ref, hbuf, acc):
    j = pl.program_id(2)

    @pl.when(j == 0)
    def _():
        hbuf[...] = (x_ref[0] * (1.0 + sc_ref[0]) + sh_ref[0]).astype(BF16)
        acc[...] = jnp.zeros_like(acc)

    _swiglu_chunk(hbuf, wa_ref[...], wb_ref[...], wo_ref[...], acc)

    @pl.when(j == pl.num_programs(2) - 1)
    def _():
        o_ref[0] = _res_ln(x_ref[0], acc[...], gate_ref[0], g_ref[...], b_ref[...])


def _swiglu_layer(x, sc, sh, gate, ln_g, ln_b, w_in, w_out, tm=512, th=1792):
    bsz, s, d = x.shape
    hid = w_out.shape[0]
    nj = hid // th
    row = pl.BlockSpec((1, tm, d), lambda b, i, j: (b, i, 0))
    mod = pl.BlockSpec((1, 1, d), lambda b, i, j: (b, 0, 0))
    vec = pl.BlockSpec((1, d), lambda b, i, j: (0, 0))
    w_in_b = w_in.astype(BF16)
    return pl.pallas_call(
        _swiglu_kernel,
        grid=(bsz, s // tm, nj),
        in_specs=[row, mod, mod,
                  pl.BlockSpec((d, th), lambda b, i, j: (0, j)),
                  pl.BlockSpec((d, th), lambda b, i, j: (0, j + nj)),
                  pl.BlockSpec((th, d), lambda b, i, j: (j, 0)),
                  mod, vec, vec],
        out_specs=row,
        out_shape=jax.ShapeDtypeStruct(x.shape, F32),
        scratch_shapes=[pltpu.VMEM((tm, d), BF16), pltpu.VMEM((tm, d), F32)],
        compiler_params=_cparams(("parallel", "parallel", "arbitrary")),
        name="swiglu",
    )(x, sc, sh, w_in_b, w_in_b, w_out.astype(BF16), gate, ln_g, ln_b)


def _sgu_kernel(x_ref, sc_ref, sh_ref, wi_ref, bi_ref, vg_ref, vb_ref, ws_ref, bs_ref, wo_ref,
                gate_ref, g_ref, b_ref, o_ref, gbuf):
    tm = x_ref.shape[1]
    h = (x_ref[0] * (1.0 + sc_ref[0]) + sh_ref[0]).astype(BF16)
    z = _dot(h, wi_ref[...]) + bi_ref[...]
    z = 0.5 * z * (1.0 + lax.erf(z * (2.0 ** -0.5)))
    u = z[:, :SGU_HALF]
    v = _ln(z[:, SGU_HALF:], vg_ref[...], vb_ref[...]).astype(BF16)
    gw = SGU_HALF // SGU_GROUPS
    ri = lax.broadcasted_iota(jnp.int32, (SGU_CHUNK, SGU_CHUNK), 0)
    ci = lax.broadcasted_iota(jnp.int32, (SGU_CHUNK, SGU_CHUNK), 1)
    causal = ci <= ri
    for g in range(SGU_GROUPS):
        wsg = jnp.where(causal, ws_ref[g], 0.0).astype(BF16)
        bias = bs_ref[:, g:g + 1]
        for c in range(tm // SGU_CHUNK):
            rows = slice(c * SGU_CHUNK, (c + 1) * SGU_CHUNK)
            cols = slice(g * gw, (g + 1) * gw)
            sv = _dot(wsg, v[rows, cols]) + bias
            gbuf[rows, cols] = (u[rows, cols] * sv).astype(BF16)
    yo = _dot(gbuf[...], wo_ref[...])
    o_ref[0] = _res_ln(x_ref[0], yo, gate_ref[0], g_ref[...], b_ref[...])


def _sgu_layer(x, sc, sh, gate, ln_g, ln_b, w_in, b_in, vg, vb, w_s, b_s, w_out, tm=512):
    bsz, s, d = x.shape
    return pl.pallas_call(
        _sgu_kernel,
        grid=(bsz, s // tm),
        in_specs=[_row_spec(tm, d), _mod_spec(), _mod_spec(),
                  _full_spec((d, 2 * SGU_HALF)), _full_spec((1, 2 * SGU_HALF)),
                  _full_spec((1, SGU_HALF)), _full_spec((1, SGU_HALF)),
                  _full_spec((SGU_GROUPS, SGU_CHUNK, SGU_CHUNK)), _full_spec((SGU_CHUNK, SGU_GROUPS)),
                  _full_spec((SGU_HALF, d)), _mod_spec(), _full_spec((1, d)), _full_spec((1, d))],
        out_specs=_row_spec(tm, d),
        out_shape=jax.ShapeDtypeStruct(x.shape, F32),
        scratch_shapes=[pltpu.VMEM((tm, SGU_HALF), BF16)],
        compiler_params=_cparams(("parallel", "parallel")),
        name="sgu",
    )(x, sc, sh, w_in.astype(BF16), b_in.reshape(1, -1), vg.reshape(1, -1), vb.reshape(1, -1),
      w_s, b_s.T, w_out.astype(BF16), gate, ln_g, ln_b)


def _gdn_in_kernel(x_ref, sc_ref, sh_ref, wq_ref, wz_ref, wba_ref, alog_ref, dtb_ref,
                   qkv_ref, z_ref, bg_ref, gct_ref):
    tm = x_ref.shape[1]
    c = GDN_TILE
    hf = x_ref[0] * (1.0 + sc_ref[0]) + sh_ref[0]
    h = hf.astype(BF16)
    qkv_ref[0] = _dot(h, wq_ref[...])
    z_ref[0] = _dot(h, wz_ref[...]).astype(BF16)
    ba = _dot3(hf, wba_ref[...])
    bg_ref[0, :, :LANES] = _sigmoid(ba)
    a_in = ba + dtb_ref[...]
    softplus = jnp.maximum(a_in, 0.0) + jnp.log1p(jnp.exp(-jnp.abs(a_in)))
    g_all = -jnp.exp(alog_ref[...]) * softplus
    ri = lax.broadcasted_iota(jnp.int32, (c, c), 0)
    ci = lax.broadcasted_iota(jnp.int32, (c, c), 1)
    tril = (ci <= ri).astype(BF16)
    for r0 in range(0, tm, c):
        g_hi, g_mid, g_lo = _split_bf16(g_all[r0:r0 + c], 3)
        gc = _dot(tril, g_hi) + (_dot(tril, g_mid) + _dot(tril, g_lo))
        bg_ref[0, r0:r0 + c, LANES:] = gc
        gct_ref[0, :, r0:r0 + c] = gc.T[GDN_HEADS:2 * GDN_HEADS]


def _gdn_chunk_kernel(q_ref, k_ref, v_ref, cwq_ref, cwk_ref, cwv_ref, bg_ref, gct_ref,
                      z_ref, ng_ref, o_ref, tail, cbuf, state, *, hp):
    c = q_ref.shape[1]

    @pl.when(pl.program_id(2) == 0)
    def _():
        tail[...] = jnp.zeros_like(tail)
        state[...] = jnp.zeros_like(state)

    off = GDN_TAIL - (GDN_CONV - 1)
    convd = []
    for idx, (ref, cw) in enumerate(((q_ref, cwq_ref), (k_ref, cwk_ref), (v_ref, cwv_ref))):
        cbuf[idx, 0:GDN_TAIL, :] = tail[idx]
        cbuf[idx, GDN_TAIL:, :] = ref[0]
        tail[idx] = ref[0, c - GDN_TAIL:c, :]
        acc = cw[0:1, :] * cbuf[idx, pl.ds(off, c), :]
        for t in range(1, GDN_CONV):
            acc = acc + cw[t:t + 1, :] * cbuf[idx, pl.ds(off + t, c), :]
        convd.append(_silu(acc))
    qc_all, kc_all, v_all = convd

    lane = lax.broadcasted_iota(jnp.int32, (1, LANES), 1)
    sub = lax.broadcasted_iota(jnp.int32, (GDN_HEADS, 1), 0)
    ri = lax.broadcasted_iota(jnp.int32, (c, c), 0)
    ci = lax.broadcasted_iota(jnp.int32, (c, c), 1)
    causal = ci <= ri
    strict = ci < ri
    xor = ri ^ ci
    eye = (ri == ci).astype(F32)
    bg = bg_ref[0]
    gct = gct_ref[0]
    z_all = z_ref[0]
    heads = range(hp)
    q, k, v, kb, beta, gc, dmat, a_mat = ([None] * hp for _ in range(8))
    for t in heads:
        hd = pl.program_id(1) * hp + t
        cols = slice(t * LANES, (t + 1) * LANES)
        qc, kc, v[t] = qc_all[:, cols], kc_all[:, cols], v_all[:, cols]
        q[t] = qc * lax.rsqrt(jnp.sum(qc * qc, -1, keepdims=True) + 1e-6) * (GDN_DK ** -0.5)
        k[t] = kc * lax.rsqrt(jnp.sum(kc * kc, -1, keepdims=True) + 1e-6)
        sel = lane == hd
        beta[t] = jnp.sum(jnp.where(sel, bg[:, :LANES], 0.0), -1, keepdims=True)
        gc[t] = jnp.sum(jnp.where(lane == GDN_HEADS + hd, bg[:, LANES:], 0.0), -1, keepdims=True)
        gc_row = jnp.sum(jnp.where(sub == hd, gct, 0.0), 0, keepdims=True)
        dmat[t] = jnp.where(causal, jnp.exp(jnp.where(causal, gc[t] - gc_row, 0.0)), 0.0)
        kb[t] = k[t].astype(BF16)
    for t in heads:
        a_mat[t] = jnp.where(strict, beta[t] * _dot_nt(kb[t], kb[t]) * dmat[t], 0.0)
    inv = [eye - jnp.where(xor == 1, a_mat[t], 0.0) for t in heads]
    lvl = 1
    while (1 << lvl) < c:
        invb = [inv[t].astype(BF16) for t in heads]
        joins = (xor >> lvl) == 1
        dm = [_dot(invb[t], jnp.where(joins, a_mat[t], 0.0).astype(BF16)) for t in heads]
        inv = [inv[t] - _dot(dm[t].astype(BF16), invb[t]) for t in heads]
        lvl += 1

    egc = [jnp.exp(gc[t]) for t in heads]
    sol = [_dot(inv[t].astype(BF16),
                jnp.concatenate([v[t] * beta[t], k[t] * (beta[t] * egc[t])], -1).astype(BF16)) for t in heads]
    qk = [jnp.where(causal, _dot_nt(q[t].astype(BF16), kb[t]) * dmat[t], 0.0).astype(BF16) for t in heads]
    s_prev = [state[t] for t in heads]
    sb = [s_prev[t].astype(BF16) for t in heads]
    vnb = [(sol[t][:, :GDN_DV] - _dot(sol[t][:, GDN_DV:].astype(BF16), sb[t])).astype(BF16) for t in heads]
    o = [_dot((q[t] * egc[t]).astype(BF16), sb[t]) + _dot(qk[t], vnb[t]) for t in heads]
    for t in heads:
        g_last = gc[t][c - 1:c, :]
        k_dec = (k[t] * jnp.exp(g_last - gc[t])).astype(BF16)
        state[t] = s_prev[t] * jnp.exp(g_last) + _dot_tn(k_dec, vnb[t])
    for t in heads:
        cols = slice(t * LANES, (t + 1) * LANES)
        on = o[t] * lax.rsqrt(jnp.mean(o[t] * o[t], -1, keepdims=True) + 1e-6) * ng_ref[...]
        o_ref[0, :, cols] = (on * _silu(z_all[:, cols].astype(F32))).astype(BF16)


def _gdn_layer(x, sc, sh, gate, ln_g, ln_b, w_in, conv_w, a_log, dt_bias, norm_g, w_out, tm=512, hp=4):
    bsz, s, d = x.shape
    nh = GDN_HEADS
    nqkv = 3 * nh * GDN_DK
    w_qkv = w_in[:, :nqkv].astype(BF16)
    w_z = w_in[:, nqkv:nqkv + nh * GDN_DV].astype(BF16)
    w_b = w_in[:, nqkv + nh * GDN_DV:nqkv + nh * GDN_DV + nh]
    w_a = w_in[:, nqkv + nh * GDN_DV + nh:]
    w_ba = jnp.concatenate([w_b, w_a, jnp.zeros((d, LANES - 2 * nh), F32)], 1)
    head_pad, lane_pad = jnp.zeros((nh,), F32), jnp.zeros((LANES - 2 * nh,), F32)
    alog = jnp.concatenate([head_pad, a_log, lane_pad]).reshape(1, LANES)
    dtb = jnp.concatenate([head_pad, dt_bias, lane_pad]).reshape(1, LANES)
    qkv, z, bg, gct = pl.pallas_call(
        _gdn_in_kernel,
        grid=(bsz, s // tm),
        in_specs=[_row_spec(tm, d), _mod_spec(), _mod_spec(), _full_spec((d, nqkv)),
                  _full_spec((d, nh * GDN_DV)), _full_spec((d, LANES)),
                  _full_spec((1, LANES)), _full_spec((1, LANES))],
        out_specs=[_row_spec(tm, nqkv), _row_spec(tm, nh * GDN_DV), _row_spec(tm, 2 * LANES),
                   pl.BlockSpec((1, nh, tm), lambda b, i: (b, 0, i))],
        out_shape=[jax.ShapeDtypeStruct((bsz, s, nqkv), F32),
                   jax.ShapeDtypeStruct((bsz, s, nh * GDN_DV), BF16),
                   jax.ShapeDtypeStruct((bsz, s, 2 * LANES), F32),
                   jax.ShapeDtypeStruct((bsz, nh, s), F32)],
        compiler_params=_cparams(("parallel", "parallel")),
        name="gdn_in",
    )(x, sc, sh, w_qkv, w_z, w_ba, alog, dtb)

    c = GDN_TILE
    wide = hp * LANES
    ng = nh // hp

    def col(base):
        return pl.BlockSpec((1, c, wide), lambda b, h, i: (b, i, base + h))

    def cw(base):
        return pl.BlockSpec((GDN_CONV, wide), lambda b, h, i: (0, base + h))

    og = pl.pallas_call(
        functools.partial(_gdn_chunk_kernel, hp=hp),
        grid=(bsz, ng, s // c),
        in_specs=[col(0), col(ng), col(2 * ng), cw(0), cw(ng), cw(2 * ng),
                  pl.BlockSpec((1, c, 2 * LANES), lambda b, h, i: (b, i, 0)),
                  pl.BlockSpec((1, nh, c), lambda b, h, i: (b, 0, i)),
                  col(0), pl.BlockSpec((1, LANES), lambda b, h, i: (0, 0))],
        out_specs=col(0),
        out_shape=jax.ShapeDtypeStruct((bsz, s, nh * GDN_DV), BF16),
        scratch_shapes=[pltpu.VMEM((3, GDN_TAIL, wide), F32), pltpu.VMEM((3, c + GDN_TAIL, wide), F32),
                        pltpu.VMEM((hp, GDN_DK, GDN_DV), F32)],
        compiler_params=_cparams(("parallel", "parallel", "arbitrary")),
        name="gdn_chunk",
    )(qkv, qkv, qkv, conv_w, conv_w, conv_w, bg, gct, z, norm_g.reshape(1, LANES))
    return _outproj_ln(og, w_out.astype(BF16), x, gate, ln_g, ln_b)


def _mla_in_kernel(x_ref, sc_ref, sh_ref, pos_ref, wi_ref, qg_ref, kg_ref, wqn_ref, wqr_ref, wqt_ref,
                   wkv_ref, qn_ref, qr_ref, kn_ref, kr_ref, v_ref):
    h = (x_ref[0] * (1.0 + sc_ref[0]) + sh_ref[0]).astype(BF16)
    p = _dot(h, wi_ref[...])
    cq = p[:, :MLA_Q_RANK]
    ckv = p[:, MLA_Q_RANK:MLA_Q_RANK + MLA_KV_RANK]
    kr = p[:, MLA_Q_RANK + MLA_KV_RANK:MLA_Q_RANK + MLA_KV_RANK + LANES]
    krt = p[:, MLA_Q_RANK + MLA_KV_RANK + LANES:]
    cqn = (cq * lax.rsqrt(jnp.mean(cq * cq, -1, keepdims=True) + 1e-6) * qg_ref[...]).astype(BF16)
    ckn = (ckv * lax.rsqrt(jnp.mean(ckv * ckv, -1, keepdims=True) + 1e-6) * kg_ref[...]).astype(BF16)
    lane = lax.broadcasted_iota(jnp.int32, (1, LANES), 1)
    half = MLA_ROPE // 2
    fidx = (lane % half).astype(F32)
    inv_freq = jnp.exp(fidx * (-math.log(ROPE_THETA) / half))
    ang = pos_ref[0].astype(F32) * inv_freq
    live = lane < MLA_ROPE
    cos = jnp.where(live, jnp.cos(ang), 0.0)
    sin = jnp.where(live, jnp.sin(ang), 0.0)
    scale = (MLA_NOPE + MLA_ROPE) ** -0.5 * math.log2(math.e)
    qn_ref[0] = (_dot(cqn, wqn_ref[...]) * scale).astype(BF16)
    qr = _dot(cqn, wqr_ref[...])
    qrt = _dot(cqn, wqt_ref[...])
    for hd in range(MLA_HEADS):
        cs = slice(hd * LANES, (hd + 1) * LANES)
        qr_ref[0, :, cs] = ((qr[:, cs] * cos + qrt[:, cs] * sin) * scale).astype(BF16)
    kr_ref[0] = (kr * cos + krt * sin).astype(BF16)
    kv = _dot(ckn, wkv_ref[...])
    nk = MLA_HEADS * MLA_NOPE
    kn_ref[0] = kv[:, :nk].astype(BF16)
    ones = jnp.ones((kv.shape[0], LANES), BF16)
    for hd in range(MLA_HEADS):
        v_ref[0, :, 2 * hd * LANES:(2 * hd + 1) * LANES] = kv[:, nk + hd * MLA_V:nk + (hd + 1) * MLA_V].astype(BF16)
        v_ref[0, :, (2 * hd + 1) * LANES:(2 * hd + 2) * LANES] = ones


def _mla_attn_kernel(qn_ref, qr_ref, kn_ref, kr_ref, v_ref, o_ref, qbuf, m_sc, acc_sc, *, hp):
    tq = qn_ref.shape[1]
    tk = tq
    i = pl.program_id(2)
    for h in range(hp):
        cols = slice(h * LANES, (h + 1) * LANES)
        qbuf[h, :, :LANES] = qn_ref[0, :, cols]
        qbuf[h, :, LANES:] = qr_ref[0, :, cols]
    m_sc[...] = jnp.full_like(m_sc, -jnp.inf)
    acc_sc[...] = jnp.zeros_like(acc_sc)

    def step(j, masked):
        r0 = pl.multiple_of(j * tk, tk)
        krt = kr_ref[0, pl.ds(r0, tk), :]
        if masked:
            ri = lax.broadcasted_iota(jnp.int32, (tq, tk), 0)
            ci = lax.broadcasted_iota(jnp.int32, (tq, tk), 1)
            keep = ci <= ri
        scs, ps, alphas = [], [], []
        for h in range(hp):
            cols = slice(h * LANES, (h + 1) * LANES)
            kt = jnp.concatenate([kn_ref[0, pl.ds(r0, tk), cols], krt], -1)
            scs.append(_dot_nt(qbuf[h], kt))
        for h in range(hp):
            sc = jnp.where(keep, scs[h], -jnp.inf) if masked else scs[h]
            m_prev = m_sc[h]
            m_new = jnp.maximum(m_prev, jnp.max(sc, -1, keepdims=True))
            alphas.append(jnp.exp2(m_prev - m_new))
            m_sc[h] = m_new
            ps.append(jnp.concatenate(
                [jnp.exp2(sc[:, c0:c0 + LANES] - m_new) for c0 in range(0, tk, LANES)], -1).astype(BF16))
        for h in range(hp):
            vcols = slice(2 * h * LANES, (2 * h + 2) * LANES)
            alpha2 = jnp.concatenate([alphas[h], alphas[h]], -1)
            acc_sc[h] = alpha2 * acc_sc[h] + _dot(ps[h], v_ref[0, pl.ds(r0, tk), vcols])

    def body(j, carry):
        step(j, False)
        return carry

    lax.fori_loop(0, i, body, 0)
    step(i, True)
    for h in range(hp):
        o_ref[0, :, h * LANES:(h + 1) * LANES] = (acc_sc[h, :, :LANES] / acc_sc[h, :, LANES:]).astype(BF16)


def _mla_layer(x, positions, sc, sh, gate, ln_g, ln_b, w_in, q_norm_g, kv_norm_g, w_uq, w_ukv, w_out,
               tm=1024, tq=512, hp=4):
    bsz, s, d = x.shape
    nh = MLA_HEADS
    half = MLA_ROPE // 2
    rot = jnp.concatenate([jnp.arange(half, MLA_ROPE), jnp.arange(half)])
    sign = jnp.concatenate([-jnp.ones((half,), F32), jnp.ones((half,), F32)])
    zpad = jnp.zeros((d, LANES - MLA_ROPE), F32)
    w_kr = w_in[:, MLA_Q_RANK + MLA_KV_RANK:]
    w_in_ext = jnp.concatenate(
        [w_in[:, :MLA_Q_RANK + MLA_KV_RANK], w_kr, zpad, w_kr[:, rot] * sign, zpad], 1).astype(BF16)
    wq = w_uq.reshape(MLA_Q_RANK, nh, MLA_NOPE + MLA_ROPE)
    w_qn = wq[:, :, :MLA_NOPE].reshape(MLA_Q_RANK, nh * MLA_NOPE).astype(BF16)
    wq_r = wq[:, :, MLA_NOPE:]
    hpad = jnp.zeros((MLA_Q_RANK, nh, LANES - MLA_ROPE), F32)
    w_qr = jnp.concatenate([wq_r, hpad], -1).reshape(MLA_Q_RANK, nh * LANES).astype(BF16)
    w_qt = jnp.concatenate([wq_r[:, :, rot] * sign, hpad], -1).reshape(MLA_Q_RANK, nh * LANES).astype(BF16)
    wkv = w_ukv.reshape(MLA_KV_RANK, nh, MLA_NOPE + MLA_V)
    w_kv = jnp.concatenate([wkv[:, :, :MLA_NOPE].reshape(MLA_KV_RANK, nh * MLA_NOPE),
                            wkv[:, :, MLA_NOPE:].reshape(MLA_KV_RANK, nh * MLA_V)], 1).astype(BF16)
    n_in = w_in_ext.shape[1]
    wide = nh * LANES
    qn, qr, kn, kr, v = pl.pallas_call(
        _mla_in_kernel,
        grid=(bsz, s // tm),
        in_specs=[_row_spec(tm, d), _mod_spec(), _mod_spec(), _row_spec(tm, 1),
                  _full_spec((d, n_in)), _full_spec((1, MLA_Q_RANK)), _full_spec((1, MLA_KV_RANK)),
                  _full_spec((MLA_Q_RANK, wide)), _full_spec((MLA_Q_RANK, wide)),
                  _full_spec((MLA_Q_RANK, wide)), _full_spec((MLA_KV_RANK, 2 * wide))],
        out_specs=[_row_spec(tm, wide), _row_spec(tm, wide), _row_spec(tm, wide), _row_spec(tm, LANES),
                   _row_spec(tm, 2 * wide)],
        out_shape=[jax.ShapeDtypeStruct((bsz, s, wide), BF16), jax.ShapeDtypeStruct((bsz, s, wide), BF16),
                   jax.ShapeDtypeStruct((bsz, s, wide), BF16), jax.ShapeDtypeStruct((bsz, s, LANES), BF16),
                   jax.ShapeDtypeStruct((bsz, s, 2 * wide), BF16)],
        compiler_params=_cparams(("parallel", "parallel")),
        name="mla_in",
    )(x, sc, sh, positions.reshape(bsz, s, 1), w_in_ext, q_norm_g.reshape(1, -1), kv_norm_g.reshape(1, -1),
      w_qn, w_qr, w_qt, w_kv)

    gw = hp * LANES
    qspec = pl.BlockSpec((1, tq, gw), lambda b, h, i: (b, i, h))
    kspec = pl.BlockSpec((1, s, gw), lambda b, h, i: (b, 0, h))
    o = pl.pallas_call(
        functools.partial(_mla_attn_kernel, hp=hp),
        grid=(bsz, nh // hp, s // tq),
        in_specs=[qspec, qspec, kspec, pl.BlockSpec((1, s, LANES), lambda b, h, i: (b, 0, 0)),
                  pl.BlockSpec((1, s, 2 * gw), lambda b, h, i: (b, 0, h))],
        out_specs=qspec,
        out_shape=jax.ShapeDtypeStruct((bsz, s, wide), BF16),
        scratch_shapes=[pltpu.VMEM((hp, tq, 2 * LANES), BF16), pltpu.VMEM((hp, tq, LANES), F32),
                        pltpu.VMEM((hp, tq, 2 * MLA_V), F32)],
        compiler_params=_cparams(("parallel", "parallel", "arbitrary")),
        name="mla_attn",
    )(qn, qr, kn, kr, v)
    return _outproj_ln(o, w_out.astype(BF16), x, gate, ln_g, ln_b)


def _router_kernel(x_ref, sc_ref, sh_ref, wr_ref, meta_ref, cnt_ref, run):
    tm = x_ref.shape[1]

    @pl.when(pl.program_id(0) == 0)
    def _():
        run[...] = jnp.zeros_like(run)

    h = x_ref[0] * (1.0 + sc_ref[0]) + sh_ref[0]
    lane = lax.broadcasted_iota(jnp.int32, (tm, LANES), 1)
    lane_f = lane.astype(F32)
    logits = jnp.where(lane < N_EXPERTS, _dot3(h, wr_ref[...]), -jnp.inf)
    m1 = jnp.max(logits, -1, keepdims=True)
    i1 = jnp.min(jnp.where(logits == m1, lane_f, float(LANES)), -1, keepdims=True)
    oh1 = lane_f == i1
    rest = jnp.where(oh1, -jnp.inf, logits)
    m2 = jnp.max(rest, -1, keepdims=True)
    i2 = jnp.min(jnp.where(rest == m2, lane_f, float(LANES)), -1, keepdims=True)
    oh2 = lane_f == i2
    e21 = jnp.exp(m2 - m1)
    g1 = 1.0 / (1.0 + e21)
    g2 = e21 / (1.0 + e21)
    cnt = oh1.astype(F32) + oh2.astype(F32)
    ri = lax.broadcasted_iota(jnp.int32, (tm, tm), 0)
    ci = lax.broadcasted_iota(jnp.int32, (tm, tm), 1)
    before = _dot((ci < ri).astype(BF16), cnt.astype(BF16)) + run[...]
    r1 = jnp.sum(jnp.where(oh1, before, 0.0), -1, keepdims=True)
    r2 = jnp.sum(jnp.where(oh2, before, 0.0), -1, keepdims=True)
    meta = jnp.zeros((tm, LANES), F32)
    for k, col in enumerate((i1, i2, g1, g2, r1, r2)):
        meta = jnp.where(lane == k, col, meta)
    meta_ref[...] = meta
    run[...] = run[...] + jnp.sum(cnt, 0, keepdims=True)
    cnt_ref[...] = run[...]


def _dispatch_kernel(tbl_sm, x_ref, sc_ref, sh_ref, dest_hbm, xb_hbm, hbuf, zbuf, idx_sm, rsem, sem, isem, *,
                     first_tail_block):
    tm = x_ref.shape[1]
    i = pl.program_id(0)
    last = pl.num_programs(0) - 1
    slot = i % 2
    hslot = hbuf.at[slot]

    def wait_rows(s):
        for _ in range(TOP_K):
            pltpu.make_async_copy(hbuf.at[s], xb_hbm.at[pl.ds(0, tm)], rsem.at[s]).wait()

    icp = pltpu.make_async_copy(dest_hbm.at[i], idx_sm, isem)
    icp.start()

    @pl.when(i >= 2)
    def _():
        wait_rows(slot)

    hslot[...] = x_ref[0] * (1.0 + sc_ref[0]) + sh_ref[0]
    icp.wait()

    def issue(r, carry):
        pltpu.make_async_copy(hslot.at[pl.ds(r, 1)], xb_hbm.at[pl.ds(idx_sm[2 * r], 1)], rsem.at[slot]).start()
        pltpu.make_async_copy(hslot.at[pl.ds(r, 1)], xb_hbm.at[pl.ds(idx_sm[2 * r + 1], 1)], rsem.at[slot]).start()
        return carry

    lax.fori_loop(0, tm, issue, 0, unroll=8)

    @pl.when(i == last)
    def _():
        @pl.when(i >= 1)
        def _():
            wait_rows(1 - slot)

        wait_rows(slot)
        zbuf[...] = jnp.zeros_like(zbuf)
        n_blocks = xb_hbm.shape[0] // MOE_BLOCK

        def pad_copies(e):
            start = tbl_sm[e]
            end = tbl_sm[N_EXPERTS + e]
            n1 = (-start) & (SUBLANES - 1)
            a0 = start + n1
            l8 = end - a0
            out = []
            for r in range(SUBLANES - 1):
                out.append((r < n1, pltpu.make_async_copy(zbuf.at[pl.ds(r, 1)], xb_hbm.at[pl.ds(start + r, 1)], sem)))
            sz = MOE_BLOCK // 2
            while sz >= SUBLANES:
                off = pl.multiple_of(a0 + (l8 & ~(2 * sz - 1)), SUBLANES)
                out.append(((l8 & sz) != 0,
                            pltpu.make_async_copy(zbuf.at[pl.ds(0, sz)], xb_hbm.at[pl.ds(off, sz)], sem)))
                sz //= 2
            return out

        def tail_copies():
            nvalid = tbl_sm[2 * N_EXPERTS]
            return [(bi >= nvalid,
                     pltpu.make_async_copy(zbuf, xb_hbm.at[pl.ds(bi * MOE_BLOCK, MOE_BLOCK)], sem))
                    for bi in range(first_tail_block, n_blocks)]

        def start_all(e, carry):
            for pred, cp in pad_copies(e):
                pl.when(pred)(cp.start)
            return carry

        def wait_all(e, carry):
            for pred, cp in pad_copies(e):
                pl.when(pred)(cp.wait)
            return carry

        lax.fori_loop(0, N_EXPERTS, start_all, 0)
        for pred, cp in tail_copies():
            pl.when(pred)(cp.start)
        lax.fori_loop(0, N_EXPERTS, wait_all, 0)
        for pred, cp in tail_copies():
            pl.when(pred)(cp.wait)


def _expert_kernel(be_sm, nv_sm, x_ref, wa_ref, wb_ref, wo_ref, o_ref, hbuf, acc):
    i = pl.program_id(0)
    j = pl.program_id(1)

    @pl.when((i >= nv_sm[0]) & (j == pl.num_programs(1) - 1))
    def _():
        o_ref[...] = jnp.zeros_like(o_ref)

    @pl.when(i < nv_sm[0])
    def _():
        @pl.when(j == 0)
        def _():
            hbuf[...] = x_ref[...].astype(BF16)
            acc[...] = jnp.zeros_like(acc)

        _swiglu_chunk(hbuf, wa_ref[0, 0].astype(BF16), wb_ref[0, 0].astype(BF16), wo_ref[0, 0].astype(BF16), acc)

        @pl.when(j == pl.num_programs(1) - 1)
        def _():
            o_ref[...] = acc[...]


def _combine_kernel(x_ref, gate_ref, meta_ref, g_ref, b_ref, dest_hbm, yb_hbm, o_ref,
                    y0, y1, idx_sm, sem, isem):
    tm = x_ref.shape[1]
    i = pl.program_id(0)
    slot = i % 2

    def gather(tile, s):
        icp = pltpu.make_async_copy(dest_hbm.at[tile], idx_sm, isem)
        icp.start()
        icp.wait()
        d0, d1 = y0.at[s], y1.at[s]

        def issue(r, carry):
            pltpu.make_async_copy(yb_hbm.at[pl.ds(idx_sm[2 * r], 1)], d0.at[pl.ds(r, 1)], sem.at[s]).start()
            pltpu.make_async_copy(yb_hbm.at[pl.ds(idx_sm[2 * r + 1], 1)], d1.at[pl.ds(r, 1)], sem.at[s]).start()
            return carry

        lax.fori_loop(0, tm, issue, 0, unroll=8)

    @pl.when(i == 0)
    def _():
        gather(0, 0)

    @pl.when(i + 1 < pl.num_programs(0))
    def _():
        gather(i + 1, 1 - slot)

    for buf in (y0, y1):
        pltpu.make_async_copy(yb_hbm.at[pl.ds(0, tm)], buf.at[slot], sem.at[slot]).wait()
    meta = meta_ref[...]
    y = meta[:, 2:3] * y0[slot] + meta[:, 3:4] * y1[slot]
    o_ref[0] = _res_ln(x_ref[0], y, gate_ref[0], g_ref[...], b_ref[...])


def _moe_layer(x, sc, sh, gate, ln_g, ln_b, router, w_in, w_out, layer, tm=1024, tr=1024, th=512):
    bsz, s, d = x.shape
    n = bsz * s
    nt = n // tm
    spt = s // tm
    hid = w_out.shape[-2]
    nj = hid // th
    min_blocks = n * TOP_K // MOE_BLOCK
    nb = min_blocks + N_EXPERTS
    tile = (d,)

    row1 = pl.BlockSpec((1, tm, d), lambda i: (i // spt, i % spt, 0))
    mod1 = pl.BlockSpec((1, 1, d), lambda i: (i // spt, 0, 0))
    slab1 = pl.BlockSpec((tm, LANES), lambda i: (i, 0))
    w_r = jnp.concatenate([router, jnp.zeros((d, LANES - N_EXPERTS), F32)], 1)
    rpt = s // tr
    meta, counts = pl.pallas_call(
        _router_kernel,
        grid=(n // tr,),
        in_specs=[pl.BlockSpec((1, tr, d), lambda i: (i // rpt, i % rpt, 0)),
                  pl.BlockSpec((1, 1, d), lambda i: (i // rpt, 0, 0)),
                  pl.BlockSpec((1, 1, d), lambda i: (i // rpt, 0, 0)),
                  pl.BlockSpec((d, LANES), lambda i: (0, 0))],
        out_specs=[pl.BlockSpec((tr, LANES), lambda i: (i, 0)), pl.BlockSpec((1, LANES), lambda i: (0, 0))],
        out_shape=[jax.ShapeDtypeStruct((n, LANES), F32), jax.ShapeDtypeStruct((1, LANES), F32)],
        scratch_shapes=[pltpu.VMEM((1, LANES), F32)],
        compiler_params=_cparams(("arbitrary",)),
        name="moe_router",
    )(x, sc, sh, w_r)

    cnt = counts[0, :N_EXPERTS].astype(jnp.int32)
    nblk = (cnt + MOE_BLOCK - 1) // MOE_BLOCK
    ends = jnp.cumsum(nblk)
    first_row = (ends - nblk) * MOE_BLOCK
    nvalid = ends[-1]
    eid = meta[:, :TOP_K].astype(jnp.int32)
    rank = meta[:, 4:4 + TOP_K].astype(jnp.int32)
    base = jnp.sum(jnp.where(eid[:, :, None] == jnp.arange(N_EXPERTS), first_row, 0), -1)
    dest = (base + rank).reshape(nt, TOP_K * tm)
    tbl = jnp.concatenate([first_row + cnt, ends * MOE_BLOCK, nvalid.reshape(1)]).astype(jnp.int32)
    bi = jnp.minimum(jnp.arange(nb, dtype=jnp.int32), nvalid - 1)
    blk_e = jnp.sum(bi[:, None] >= ends[None, :], -1).astype(jnp.int32)

    row1p = pl.BlockSpec((1, tm, d), lambda i, t: (i // spt, i % spt, 0))
    mod1p = pl.BlockSpec((1, 1, d), lambda i, t: (i // spt, 0, 0))
    xb = pl.pallas_call(
        functools.partial(_dispatch_kernel, first_tail_block=min_blocks),
        grid_spec=pltpu.PrefetchScalarGridSpec(
            num_scalar_prefetch=1,
            grid=(nt,),
            in_specs=[row1p, mod1p, mod1p, pl.BlockSpec(memory_space=pl.ANY)],
            out_specs=pl.BlockSpec(memory_space=pl.ANY),
            scratch_shapes=[pltpu.VMEM((2, tm) + tile, F32), pltpu.VMEM((MOE_BLOCK,) + tile, F32),
                            pltpu.SMEM((TOP_K * tm,), jnp.int32),
                            pltpu.SemaphoreType.DMA((2,)), pltpu.SemaphoreType.DMA, pltpu.SemaphoreType.DMA]),
        out_shape=jax.ShapeDtypeStruct((nb * MOE_BLOCK,) + tile, F32),
        compiler_params=_cparams(("arbitrary",)),
        name="moe_dispatch",
    )(tbl, x, sc, sh, dest)

    last = nj - 1

    def jsel(i, j, nv):
        return jnp.where(i < nv[0], j, last)

    yb = pl.pallas_call(
        _expert_kernel,
        grid_spec=pltpu.PrefetchScalarGridSpec(
            num_scalar_prefetch=2,
            grid=(nb, nj),
            in_specs=[pl.BlockSpec((MOE_BLOCK,) + tile, lambda i, j, be, nv: (i, 0)),
                      pl.BlockSpec((1, 1, d, th), lambda i, j, be, nv: (layer, be[i], 0, jsel(i, j, nv))),
                      pl.BlockSpec((1, 1, d, th), lambda i, j, be, nv: (layer, be[i], 0, jsel(i, j, nv) + nj)),
                      pl.BlockSpec((1, 1, th, d), lambda i, j, be, nv: (layer, be[i], jsel(i, j, nv), 0))],
            out_specs=pl.BlockSpec((MOE_BLOCK,) + tile, lambda i, j, be, nv: (i, 0)),
            scratch_shapes=[pltpu.VMEM((MOE_BLOCK, d), BF16), pltpu.VMEM((MOE_BLOCK, d), F32)]),
        out_shape=jax.ShapeDtypeStruct((nb * MOE_BLOCK,) + tile, F32),
        compiler_params=_cparams(("arbitrary", "arbitrary")),
        name="moe_experts",
    )(blk_e, nvalid.reshape(1).astype(jnp.int32), xb, w_in, w_in, w_out)

    return pl.pallas_call(
        _combine_kernel,
        grid=(nt,),
        in_specs=[row1, mod1, slab1, pl.BlockSpec((1, d), lambda i: (0, 0)), pl.BlockSpec((1, d), lambda i: (0, 0)),
                  pl.BlockSpec(memory_space=pl.ANY), pl.BlockSpec(memory_space=pl.ANY)],
        out_specs=row1,
        out_shape=jax.ShapeDtypeStruct(x.shape, F32),
        scratch_shapes=[pltpu.VMEM((2, tm) + tile, F32), pltpu.VMEM((2, tm) + tile, F32),
                        pltpu.SMEM((TOP_K * tm,), jnp.int32),
                        pltpu.SemaphoreType.DMA((2,)), pltpu.SemaphoreType.DMA],
        compiler_params=_cparams(("arbitrary",)),
        name="moe_combine",
    )(x, gate, meta, ln_g, ln_b, dest, yb)


def kernel(x, c, positions, ada_w, ada_b, ln_g, ln_b, conv_w_in, conv_dw, conv_dw_b, conv_ln_g, conv_ln_b, conv_w_out, sgu_w_in, sgu_b_in, sgu_ln_g, sgu_ln_b, sgu_w_s, sgu_b_s, sgu_w_out, gdn_w_in, gdn_conv, gdn_a_log, gdn_dt_bias, gdn_norm_g, gdn_w_out, mla_w_in, mla_q_norm_g, mla_kv_norm_g, mla_w_uq, mla_w_ukv, mla_w_out, ffn_w_in, ffn_w_out, moe_router, moe_w_in, moe_w_out):
    bsz, s, d = x.shape
    mod = _ada_mod(c, ada_w, ada_b).reshape(DEPTH, bsz, 6, 1, d)
    for i in range(DEPTH):
        sh1, sc1, g1, sh2, sc2, g2 = (mod[i, :, t] for t in range(6))
        lg1, lb1 = ln_g[i, 0].reshape(1, d), ln_b[i, 0].reshape(1, d)
        lg2, lb2 = ln_g[i, 1].reshape(1, d), ln_b[i, 1].reshape(1, d)
        j = i // 4
        mixer = i % 4
        if mixer == 0:
            x = _conformer_layer(x, sc1, sh1, g1, lg1, lb1, conv_w_in[j], conv_dw[j], conv_dw_b[j],
                                 conv_ln_g[j], conv_ln_b[j], conv_w_out[j])
        elif mixer == 1:
            x = _sgu_layer(x, sc1, sh1, g1, lg1, lb1, sgu_w_in[j], sgu_b_in[j], sgu_ln_g[j], sgu_ln_b[j],
                           sgu_w_s[j], sgu_b_s[j], sgu_w_out[j])
        elif mixer == 2:
            x = _gdn_layer(x, sc1, sh1, g1, lg1, lb1, gdn_w_in[j], gdn_conv[j], gdn_a_log[j], gdn_dt_bias[j],
                           gdn_norm_g[j], gdn_w_out[j])
        else:
            x = _mla_layer(x, positions, sc1, sh1, g1, lg1, lb1, mla_w_in[j], mla_q_norm_g[j],
                           mla_kv_norm_g[j], mla_w_uq[j], mla_w_ukv[j], mla_w_out[j])
        if i % 2 == 0:
            x = _swiglu_layer(x, sc2, sh2, g2, lg2, lb2, ffn_w_in[i // 2], ffn_w_out[i // 2])
        else:
            x = _moe_layer(x, sc2, sh2, g2, lg2, lb2, moe_router[i // 2], moe_w_in, moe_w_out, i // 2)
    return x
```

```python
import functools
import math

import jax
import jax.numpy as jnp
from jax import lax
from jax.experimental import pallas as pl
from jax.experimental.pallas import tpu as pltpu

F32 = jnp.float32
BF16 = jnp.bfloat16

D_MODEL = 1024
DEPTH = 4
ALPHA = (2 * DEPTH) ** 0.25

CONV_WIDTH = 31
CONV_HALO = 32
CONV_ROWS = 256
SGU_CHUNK = 128
SGU_GROUPS = 8
SGU_HALF = 2 * D_MODEL
GDN_HEADS = 8
GDN_DK = 128
GDN_DV = 128
GDN_CONV = 4
GDN_TILE = 256
GDN_TAIL = 8
MLA_HEADS = 8
MLA_Q_RANK = 512
MLA_KV_RANK = 256
MLA_NOPE = 128
MLA_ROPE = 64
MLA_V = 128
ROPE_THETA = 10000.0
FFN_HIDDEN = 7 * D_MODEL // 2
N_EXPERTS = 8
TOP_K = 2
LANES = 128
SUBLANES = 8
MOE_BLOCK = 1024
FFN_ROW_SPLIT = 2
VMEM_LIMIT = 56 * 1024 * 1024


def _cparams(sem):
    return pltpu.CompilerParams(dimension_semantics=sem, vmem_limit_bytes=VMEM_LIMIT)


def _sigmoid(x):
    return 1.0 / (1.0 + jnp.exp(-x))


def _silu(x):
    return x * _sigmoid(x)


def _ln(z, g, b, eps=1e-5):
    mu = jnp.mean(z, -1, keepdims=True)
    zc = z - mu
    var = jnp.mean(zc * zc, -1, keepdims=True)
    return zc * lax.rsqrt(var + eps) * g + b


def _res_ln(x, y, gate, g, b):
    return _ln(ALPHA * x + (1.0 + gate) * y, g, b)


def _dot(a, b):
    return jnp.dot(a, b, preferred_element_type=F32)


def _dot_nt(a, b):
    return lax.dot_general(a, b, (((1,), (1,)), ((), ())), preferred_element_type=F32)


def _dot_tn(a, b):
    return lax.dot_general(a, b, (((0,), (0,)), ((), ())), preferred_element_type=F32)


def _split_bf16(a, parts):
    out = []
    for _ in range(parts):
        piece = a.astype(BF16)
        out.append(piece)
        a = a - piece.astype(F32)
    return out


def _dot3(a, b):
    (ah, al), (bh, bl) = _split_bf16(a, 2), _split_bf16(b, 2)
    return _dot(ah, bh) + (_dot(ah, bl) + _dot(al, bh))


def _ada_kernel(c_ref, w_ref, b_ref, o_ref):
    cond = _silu(c_ref[...])
    o_ref[0] = _dot3(cond, w_ref[0]) + b_ref[0]


def _ada_mod(c, ada_w, ada_b):
    bsz, d = c.shape
    n_out = ada_w.shape[-1]
    tn = 1024
    return pl.pallas_call(
        _ada_kernel,
        grid=(DEPTH, n_out // tn),
        in_specs=[
            pl.BlockSpec((bsz, d), lambda i, j: (0, 0)),
            pl.BlockSpec((1, d, tn), lambda i, j: (i, 0, j)),
            pl.BlockSpec((1, 1, tn), lambda i, j: (i, 0, j)),
        ],
        out_specs=pl.BlockSpec((1, bsz, tn), lambda i, j: (i, 0, j)),
        out_shape=jax.ShapeDtypeStruct((DEPTH, bsz, n_out), F32),
        compiler_params=_cparams(("parallel", "parallel")),
        name="ada_mod",
    )(c, ada_w, ada_b.reshape(DEPTH, 1, n_out))


def _row_spec(tm, width):
    return pl.BlockSpec((1, tm, width), lambda b, i: (b, i, 0))


def _mod_spec(width=D_MODEL):
    return pl.BlockSpec((1, 1, width), lambda b, i: (b, 0, 0))


def _full_spec(shape):
    nd = len(shape)
    return pl.BlockSpec(shape, lambda b, i: (0,) * nd)


def _outproj_kernel(a_ref, w_ref, x_ref, gate_ref, g_ref, b_ref, o_ref):
    y = _dot(a_ref[0], w_ref[...])
    o_ref[0] = _res_ln(x_ref[0], y, gate_ref[0], g_ref[...], b_ref[...])


def _outproj_ln(a, w, x, gate, ln_g, ln_b, tm=1024):
    bsz, s, k = a.shape
    d = x.shape[-1]
    return pl.pallas_call(
        _outproj_kernel,
        grid=(bsz, s // tm),
        in_specs=[_row_spec(tm, k), _full_spec((k, d)), _row_spec(tm, d), _mod_spec(d),
                  _full_spec((1, d)), _full_spec((1, d))],
        out_specs=_row_spec(tm, d),
        out_shape=jax.ShapeDtypeStruct(x.shape, F32),
        compiler_params=_cparams(("parallel", "parallel")),
        name="outproj_ln",
    )(a, w, x, gate, ln_g, ln_b)


def _conv_in_kernel(x_ref, sc_ref, sh_ref, w_ref, o_ref):
    h = (x_ref[0] * (1.0 + sc_ref[0]) + sh_ref[0]).astype(BF16)
    ag = _dot(h, w_ref[...])
    d = o_ref.shape[-1]
    o_ref[0] = ag[:, :d] * _sigmoid(ag[:, d:])


def _conv_mid_kernel(y_ref, halo_ref, dw_ref, dwb_ref, cg_ref, cb_ref, w_ref,
                     x_ref, gate_ref, g_ref, b_ref, o_ref, ybuf, cbuf, shbuf):
    tm = y_ref.shape[1]
    d = y_ref.shape[2]
    first = pl.program_id(1) == 0
    ybuf[0:CONV_HALO, :] = jnp.where(first, 0.0, halo_ref[0])
    ybuf[CONV_HALO:, :] = y_ref[0]
    rc = CONV_ROWS
    off = CONV_HALO - (CONV_WIDTH - 1)
    def col_block(ci, carry):
        cols = pl.ds(pl.multiple_of(ci * LANES, LANES), LANES)
        for r0 in range(0, tm, rc):
            acc = None
            for r in range(SUBLANES):
                ks = [k for k in range(CONV_WIDTH) if (off + k) % SUBLANES == r]
                span = max((off + k) // SUBLANES for k in ks) * SUBLANES + rc
                shbuf[r, 0:span, :] = ybuf[r0 + r:r0 + r + span, cols]
                for k in ks:
                    q8 = (off + k) // SUBLANES * SUBLANES
                    term = dw_ref[k:k + 1, cols] * shbuf[r, q8:q8 + rc, :]
                    acc = term if acc is None else acc + term
            cbuf[r0:r0 + rc, cols] = acc
        return carry

    lax.fori_loop(0, d // LANES, col_block, 0)
    yc = cbuf[...] + dwb_ref[...]
    yn = _silu(_ln(yc, cg_ref[...], cb_ref[...]))
    yo = _dot(yn.astype(BF16), w_ref[...])
    o_ref[0] = _res_ln(x_ref[0], yo, gate_ref[0], g_ref[...], b_ref[...])


def _conformer_layer(x, sc, sh, gate, ln_g, ln_b, w_in, dw, dw_b, cg, cb, w_out, tm=512, tin=1024):
    bsz, s, d = x.shape
    y = pl.pallas_call(
        _conv_in_kernel,
        grid=(bsz, s // tin),
        in_specs=[_row_spec(tin, d), _mod_spec(), _mod_spec(), _full_spec((d, 2 * d))],
        out_specs=_row_spec(tin, d),
        out_shape=jax.ShapeDtypeStruct((bsz, s, d), F32),
        compiler_params=_cparams(("parallel", "parallel")),
        name="conv_in",
    )(x, sc, sh, w_in.astype(BF16))
    hb = tm // CONV_HALO
    halo_spec = pl.BlockSpec((1, CONV_HALO, d), lambda b, i: (b, jnp.maximum(i * hb - 1, 0), 0))
    dw_pad = jnp.concatenate([dw, jnp.zeros((1, d), F32)], 0)
    return pl.pallas_call(
        _conv_mid_kernel,
        grid=(bsz, s // tm),
        in_specs=[_row_spec(tm, d), halo_spec, _full_spec((CONV_WIDTH + 1, d)), _full_spec((1, d)),
                  _full_spec((1, d)), _full_spec((1, d)), _full_spec((d, d)),
                  _row_spec(tm, d), _mod_spec(), _full_spec((1, d)), _full_spec((1, d))],
        out_specs=_row_spec(tm, d),
        out_shape=jax.ShapeDtypeStruct((bsz, s, d), F32),
        scratch_shapes=[pltpu.VMEM((tm + CONV_HALO, d), F32), pltpu.VMEM((tm, d), F32),
                        pltpu.VMEM((SUBLANES, CONV_ROWS + CONV_HALO, LANES), F32)],
        compiler_params=_cparams(("parallel", "parallel")),
        name="conv_mid",
    )(y, y, dw_pad, dw_b.reshape(1, d), cg.reshape(1, d), cb.reshape(1, d), w_out.astype(BF16),
      x, gate, ln_g, ln_b)


def _swiglu_chunk(hbuf, wa, wb, wo, acc):
    rows = hbuf.shape[0] // FFN_ROW_SPLIT
    groups = [slice(g * rows, (g + 1) * rows) for g in range(FFN_ROW_SPLIT)]
    ab = [(_dot(hbuf[rs, :], wa), _dot(hbuf[rs, :], wb)) for rs in groups]
    acts = [(_silu(a) * b).astype(BF16) for a, b in ab]
    for rs, act in zip(groups, acts):
        acc[rs, :] += _dot(act, wo)


def _swiglu_kernel(x_ref, sc_ref, sh_ref, wa_ref, wb_ref, wo_ref, gate_ref, g_ref, b_ref,
                   o_ref, hbuf, acc):
    j = pl.program_id(2)

    @pl.when(j == 0)
    def _():
        hbuf[...] = (x_ref[0] * (1.0 + sc_ref[0]) + sh_ref[0]).astype(BF16)
        acc[...] = jnp.zeros_like(acc)

    _swiglu_chunk(hbuf, wa_ref[...], wb_ref[...], wo_ref[...], acc)

    @pl.when(j == pl.num_programs(2) - 1)
    def _():
        o_ref[0] = _res_ln(x_ref[0], acc[...], gate_ref[0], g_ref[...], b_ref[...])


def _swiglu_layer(x, sc, sh, gate, ln_g, ln_b, w_in, w_out, tm=1024, th=1792):
    bsz, s, d = x.shape
    hid = w_out.shape[0]
    nj = hid // th
    row = pl.BlockSpec((1, tm, d), lambda b, i, j: (b, i, 0))
    mod = pl.BlockSpec((1, 1, d), lambda b, i, j: (b, 0, 0))
    vec = pl.BlockSpec((1, d), lambda b, i, j: (0, 0))
    w_in_b = w_in.astype(BF16)
    return pl.pallas_call(
        _swiglu_kernel,
        grid=(bsz, s // tm, nj),
        in_specs=[row, mod, mod,
                  pl.BlockSpec((d, th), lambda b, i, j: (0, j)),
                  pl.BlockSpec((d, th), lambda b, i, j: (0, j + nj)),
                  pl.BlockSpec((th, d), lambda b, i, j: (j, 0)),
                  mod, vec, vec],
        out_specs=row,
        out_shape=jax.ShapeDtypeStruct(x.shape, F32),
        scratch_shapes=[pltpu.VMEM((tm, d), BF16), pltpu.VMEM((tm, d), F32)],
        compiler_params=_cparams(("parallel", "parallel", "arbitrary")),
        name="swiglu",
    )(x, sc, sh, w_in_b, w_in_b, w_out.astype(BF16), gate, ln_g, ln_b)


def _sgu_kernel(x_ref, sc_ref, sh_ref, wi_ref, bi_ref, vg_ref, vb_ref, ws_ref, bs_ref, wo_ref,
                gate_ref, g_ref, b_ref, o_ref, gbuf):
    tm = x_ref.shape[1]
    h = (x_ref[0] * (1.0 + sc_ref[0]) + sh_ref[0]).astype(BF16)
    z = _dot(h, wi_ref[...]) + bi_ref[...]
    z = 0.5 * z * (1.0 + lax.erf(z * (2.0 ** -0.5)))
    u = z[:, :SGU_HALF]
    v = _ln(z[:, SGU_HALF:], vg_ref[...], vb_ref[...]).astype(BF16)
    gw = SGU_HALF // SGU_GROUPS
    ri = lax.broadcasted_iota(jnp.int32, (SGU_CHUNK, SGU_CHUNK), 0)
    ci = lax.broadcasted_iota(jnp.int32, (SGU_CHUNK, SGU_CHUNK), 1)
    causal = ci <= ri
    for g in range(SGU_GROUPS):
        wsg = jnp.where(causal, ws_ref[g], 0.0).astype(BF16)
        bias = bs_ref[:, g:g + 1]
        for c in range(tm // SGU_CHUNK):
            rows = slice(c * SGU_CHUNK, (c + 1) * SGU_CHUNK)
            cols = slice(g * gw, (g + 1) * gw)
            sv = _dot(wsg, v[rows, cols]) + bias
            gbuf[rows, cols] = (u[rows, cols] * sv).astype(BF16)
    yo = _dot(gbuf[...], wo_ref[...])
    o_ref[0] = _res_ln(x_ref[0], yo, gate_ref[0], g_ref[...], b_ref[...])


def _sgu_layer(x, sc, sh, gate, ln_g, ln_b, w_in, b_in, vg, vb, w_s, b_s, w_out, tm=512):
    bsz, s, d = x.shape
    return pl.pallas_call(
        _sgu_kernel,
        grid=(bsz, s // tm),
        in_specs=[_row_spec(tm, d), _mod_spec(), _mod_spec(),
                  _full_spec((d, 2 * SGU_HALF)), _full_spec((1, 2 * SGU_HALF)),
                  _full_spec((1, SGU_HALF)), _full_spec((1, SGU_HALF)),
                  _full_spec((SGU_GROUPS, SGU_CHUNK, SGU_CHUNK)), _full_spec((SGU_CHUNK, SGU_GROUPS)),
                  _full_spec((SGU_HALF, d)), _mod_spec(), _full_spec((1, d)), _full_spec((1, d))],
        out_specs=_row_spec(tm, d),
        out_shape=jax.ShapeDtypeStruct(x.shape, F32),
        scratch_shapes=[pltpu.VMEM((tm, SGU_HALF), BF16)],
        compiler_params=_cparams(("parallel", "parallel")),
        name="sgu",
    )(x, sc, sh, w_in.astype(BF16), b_in.reshape(1, -1), vg.reshape(1, -1), vb.reshape(1, -1),
      w_s, b_s.T, w_out.astype(BF16), gate, ln_g, ln_b)


def _gdn_in_kernel(x_ref, sc_ref, sh_ref, wq_ref, wz_ref, wba_ref, alog_ref, dtb_ref,
                   qkv_ref, z_ref, bg_ref, gct_ref):
    tm = x_ref.shape[1]
    c = GDN_TILE
    hf = x_ref[0] * (1.0 + sc_ref[0]) + sh_ref[0]
    h = hf.astype(BF16)
    qkv_ref[0] = _dot(h, wq_ref[...])
    z_ref[0] = _dot(h, wz_ref[...]).astype(BF16)
    ba = _dot3(hf, wba_ref[...])
    bg_ref[0, :, :LANES] = _sigmoid(ba)
    a_in = ba + dtb_ref[...]
    softplus = jnp.maximum(a_in, 0.0) + jnp.log1p(jnp.exp(-jnp.abs(a_in)))
    g_all = -jnp.exp(alog_ref[...]) * softplus
    ri = lax.broadcasted_iota(jnp.int32, (c, c), 0)
    ci = lax.broadcasted_iota(jnp.int32, (c, c), 1)
    tril = (ci <= ri).astype(BF16)
    for r0 in range(0, tm, c):
        g_hi, g_mid, g_lo = _split_bf16(g_all[r0:r0 + c], 3)
        gc = _dot(tril, g_hi) + (_dot(tril, g_mid) + _dot(tril, g_lo))
        bg_ref[0, r0:r0 + c, LANES:] = gc
        gct_ref[0, :, r0:r0 + c] = gc.T[GDN_HEADS:2 * GDN_HEADS]


def _gdn_chunk_kernel(q_ref, k_ref, v_ref, cwq_ref, cwk_ref, cwv_ref, bg_ref, gct_ref,
                      z_ref, ng_ref, o_ref, tail, cbuf, state, *, hp):
    c = q_ref.shape[1]

    @pl.when(pl.program_id(2) == 0)
    def _():
        tail[...] = jnp.zeros_like(tail)
        state[...] = jnp.zeros_like(state)

    off = GDN_TAIL - (GDN_CONV - 1)
    convd = []
    for idx, (ref, cw) in enumerate(((q_ref, cwq_ref), (k_ref, cwk_ref), (v_ref, cwv_ref))):
        cbuf[idx, 0:GDN_TAIL, :] = tail[idx]
        cbuf[idx, GDN_TAIL:, :] = ref[0]
        tail[idx] = ref[0, c - GDN_TAIL:c, :]
        acc = cw[0:1, :] * cbuf[idx, pl.ds(off, c), :]
        for t in range(1, GDN_CONV):
            acc = acc + cw[t:t + 1, :] * cbuf[idx, pl.ds(off + t, c), :]
        convd.append(_silu(acc))
    qc_all, kc_all, v_all = convd

    lane = lax.broadcasted_iota(jnp.int32, (1, LANES), 1)
    sub = lax.broadcasted_iota(jnp.int32, (GDN_HEADS, 1), 0)
    ri = lax.broadcasted_iota(jnp.int32, (c, c), 0)
    ci = lax.broadcasted_iota(jnp.int32, (c, c), 1)
    causal = ci <= ri
    strict = ci < ri
    xor = ri ^ ci
    eye = (ri == ci).astype(F32)
    bg = bg_ref[0]
    gct = gct_ref[0]
    z_all = z_ref[0]
    heads = range(hp)
    q, k, v, kb, beta, gc, dmat, a_mat = ([None] * hp for _ in range(8))
    for t in heads:
        hd = pl.program_id(1) * hp + t
        cols = slice(t * LANES, (t + 1) * LANES)
        qc, kc, v[t] = qc_all[:, cols], kc_all[:, cols], v_all[:, cols]
        q[t] = qc * lax.rsqrt(jnp.sum(qc * qc, -1, keepdims=True) + 1e-6) * (GDN_DK ** -0.5)
        k[t] = kc * lax.rsqrt(jnp.sum(kc * kc, -1, keepdims=True) + 1e-6)
        sel = lane == hd
        beta[t] = jnp.sum(jnp.where(sel, bg[:, :LANES], 0.0), -1, keepdims=True)
        gc[t] = jnp.sum(jnp.where(lane == GDN_HEADS + hd, bg[:, LANES:], 0.0), -1, keepdims=True)
        gc_row = jnp.sum(jnp.where(sub == hd, gct, 0.0), 0, keepdims=True)
        dmat[t] = jnp.where(causal, jnp.exp(jnp.where(causal, gc[t] - gc_row, 0.0)), 0.0)
        kb[t] = k[t].astype(BF16)
    for t in heads:
        a_mat[t] = jnp.where(strict, beta[t] * _dot_nt(kb[t], kb[t]) * dmat[t], 0.0)
    inv = [eye - jnp.where(xor == 1, a_mat[t], 0.0) for t in heads]
    lvl = 1
    while (1 << lvl) < c:
        invb = [inv[t].astype(BF16) for t in heads]
        joins = (xor >> lvl) == 1
        dm = [_dot(invb[t], jnp.where(joins, a_mat[t], 0.0).astype(BF16)) for t in heads]
        inv = [inv[t] - _dot(dm[t].astype(BF16), invb[t]) for t in heads]
        lvl += 1

    egc = [jnp.exp(gc[t]) for t in heads]
    sol = [_dot(inv[t].astype(BF16),
                jnp.concatenate([v[t] * beta[t], k[t] * (beta[t] * egc[t])], -1).astype(BF16)) for t in heads]
    qk = [jnp.where(causal, _dot_nt(q[t].astype(BF16), kb[t]) * dmat[t], 0.0).astype(BF16) for t in heads]
    s_prev = [state[t] for t in heads]
    sb = [s_prev[t].astype(BF16) for t in heads]
    vnb = [(sol[t][:, :GDN_DV] - _dot(sol[t][:, GDN_DV:].astype(BF16), sb[t])).astype(BF16) for t in heads]
    o = [_dot((q[t] * egc[t]).astype(BF16), sb[t]) + _dot(qk[t], vnb[t]) for t in heads]
    for t in heads:
        g_last = gc[t][c - 1:c, :]
        k_dec = (k[t] * jnp.exp(g_last - gc[t])).astype(BF16)
        state[t] = s_prev[t] * jnp.exp(g_last) + _dot_tn(k_dec, vnb[t])
    for t in heads:
        cols = slice(t * LANES, (t + 1) * LANES)
        on = o[t] * lax.rsqrt(jnp.mean(o[t] * o[t], -1, keepdims=True) + 1e-6) * ng_ref[...]
        o_ref[0, :, cols] = (on * _silu(z_all[:, cols].astype(F32))).astype(BF16)


def _gdn_layer(x, sc, sh, gate, ln_g, ln_b, w_in, conv_w, a_log, dt_bias, norm_g, w_out, tm=512, hp=4):
    bsz, s, d = x.shape
    nh = GDN_HEADS
    nqkv = 3 * nh * GDN_DK
    w_qkv = w_in[:, :nqkv].astype(BF16)
    w_z = w_in[:, nqkv:nqkv + nh * GDN_DV].astype(BF16)
    w_b = w_in[:, nqkv + nh * GDN_DV:nqkv + nh * GDN_DV + nh]
    w_a = w_in[:, nqkv + nh * GDN_DV + nh:]
    w_ba = jnp.concatenate([w_b, w_a, jnp.zeros((d, LANES - 2 * nh), F32)], 1)
    head_pad, lane_pad = jnp.zeros((nh,), F32), jnp.zeros((LANES - 2 * nh,), F32)
    alog = jnp.concatenate([head_pad, a_log, lane_pad]).reshape(1, LANES)
    dtb = jnp.concatenate([head_pad, dt_bias, lane_pad]).reshape(1, LANES)
    qkv, z, bg, gct = pl.pallas_call(
        _gdn_in_kernel,
        grid=(bsz, s // tm),
        in_specs=[_row_spec(tm, d), _mod_spec(), _mod_spec(), _full_spec((d, nqkv)),
                  _full_spec((d, nh * GDN_DV)), _full_spec((d, LANES)),
                  _full_spec((1, LANES)), _full_spec((1, LANES))],
        out_specs=[_row_spec(tm, nqkv), _row_spec(tm, nh * GDN_DV), _row_spec(tm, 2 * LANES),
                   pl.BlockSpec((1, nh, tm), lambda b, i: (b, 0, i))],
        out_shape=[jax.ShapeDtypeStruct((bsz, s, nqkv), F32),
                   jax.ShapeDtypeStruct((bsz, s, nh * GDN_DV), BF16),
                   jax.ShapeDtypeStruct((bsz, s, 2 * LANES), F32),
                   jax.ShapeDtypeStruct((bsz, nh, s), F32)],
        compiler_params=_cparams(("parallel", "parallel")),
        name="gdn_in",
    )(x, sc, sh, w_qkv, w_z, w_ba, alog, dtb)

    c = GDN_TILE
    wide = hp * LANES
    ng = nh // hp

    def col(base):
        return pl.BlockSpec((1, c, wide), lambda b, h, i: (b, i, base + h))

    def cw(base):
        return pl.BlockSpec((GDN_CONV, wide), lambda b, h, i: (0, base + h))

    og = pl.pallas_call(
        functools.partial(_gdn_chunk_kernel, hp=hp),
        grid=(bsz, ng, s // c),
        in_specs=[col(0), col(ng), col(2 * ng), cw(0), cw(ng), cw(2 * ng),
                  pl.BlockSpec((1, c, 2 * LANES), lambda b, h, i: (b, i, 0)),
                  pl.BlockSpec((1, nh, c), lambda b, h, i: (b, 0, i)),
                  col(0), pl.BlockSpec((1, LANES), lambda b, h, i: (0, 0))],
        out_specs=col(0),
        out_shape=jax.ShapeDtypeStruct((bsz, s, nh * GDN_DV), BF16),
        scratch_shapes=[pltpu.VMEM((3, GDN_TAIL, wide), F32), pltpu.VMEM((3, c + GDN_TAIL, wide), F32),
                        pltpu.VMEM((hp, GDN_DK, GDN_DV), F32)],
        compiler_params=_cparams(("parallel", "parallel", "arbitrary")),
        name="gdn_chunk",
    )(qkv, qkv, qkv, conv_w, conv_w, conv_w, bg, gct, z, norm_g.reshape(1, LANES))
    return _outproj_ln(og, w_out.astype(BF16), x, gate, ln_g, ln_b)


def _mla_in_kernel(x_ref, sc_ref, sh_ref, pos_ref, wi_ref, qg_ref, kg_ref, wqn_ref, wqr_ref, wqt_ref,
                   wkv_ref, qn_ref, qr_ref, kn_ref, kr_ref, v_ref):
    h = (x_ref[0] * (1.0 + sc_ref[0]) + sh_ref[0]).astype(BF16)
    p = _dot(h, wi_ref[...])
    cq = p[:, :MLA_Q_RANK]
    ckv = p[:, MLA_Q_RANK:MLA_Q_RANK + MLA_KV_RANK]
    kr = p[:, MLA_Q_RANK + MLA_KV_RANK:MLA_Q_RANK + MLA_KV_RANK + LANES]
    krt = p[:, MLA_Q_RANK + MLA_KV_RANK + LANES:]
    cqn = (cq * lax.rsqrt(jnp.mean(cq * cq, -1, keepdims=True) + 1e-6) * qg_ref[...]).astype(BF16)
    ckn = (ckv * lax.rsqrt(jnp.mean(ckv * ckv, -1, keepdims=True) + 1e-6) * kg_ref[...]).astype(BF16)
    lane = lax.broadcasted_iota(jnp.int32, (1, LANES), 1)
    half = MLA_ROPE // 2
    fidx = (lane % half).astype(F32)
    inv_freq = jnp.exp(fidx * (-math.log(ROPE_THETA) / half))
    ang = pos_ref[0].astype(F32) * inv_freq
    live = lane < MLA_ROPE
    cos = jnp.where(live, jnp.cos(ang), 0.0)
    sin = jnp.where(live, jnp.sin(ang), 0.0)
    scale = (MLA_NOPE + MLA_ROPE) ** -0.5 * math.log2(math.e)
    qn_ref[0] = (_dot(cqn, wqn_ref[...]) * scale).astype(BF16)
    qr = _dot(cqn, wqr_ref[...])
    qrt = _dot(cqn, wqt_ref[...])
    for hd in range(MLA_HEADS):
        cs = slice(hd * LANES, (hd + 1) * LANES)
        qr_ref[0, :, cs] = ((qr[:, cs] * cos + qrt[:, cs] * sin) * scale).astype(BF16)
    kr_ref[0] = (kr * cos + krt * sin).astype(BF16)
    kv = _dot(ckn, wkv_ref[...])
    nk = MLA_HEADS * MLA_NOPE
    kn_ref[0] = kv[:, :nk].astype(BF16)
    ones = jnp.ones((kv.shape[0], LANES), BF16)
    for hd in range(MLA_HEADS):
        v_ref[0, :, 2 * hd * LANES:(2 * hd + 1) * LANES] = kv[:, nk + hd * MLA_V:nk + (hd + 1) * MLA_V].astype(BF16)
        v_ref[0, :, (2 * hd + 1) * LANES:(2 * hd + 2) * LANES] = ones


def _mla_attn_kernel(qn_ref, qr_ref, kn_ref, kr_ref, v_ref, o_ref, qbuf, m_sc, acc_sc, *, hp):
    tq = qn_ref.shape[1]
    tk = tq
    i = pl.program_id(2)
    for h in range(hp):
        cols = slice(h * LANES, (h + 1) * LANES)
        qbuf[h, :, :LANES] = qn_ref[0, :, cols]
        qbuf[h, :, LANES:] = qr_ref[0, :, cols]
    m_sc[...] = jnp.full_like(m_sc, -jnp.inf)
    acc_sc[...] = jnp.zeros_like(acc_sc)

    def step(j, masked):
        r0 = pl.multiple_of(j * tk, tk)
        krt = kr_ref[0, pl.ds(r0, tk), :]
        if masked:
            ri = lax.broadcasted_iota(jnp.int32, (tq, tk), 0)
            ci = lax.broadcasted_iota(jnp.int32, (tq, tk), 1)
            keep = ci <= ri
        scs, ps, alphas = [], [], []
        for h in range(hp):
            cols = slice(h * LANES, (h + 1) * LANES)
            kt = jnp.concatenate([kn_ref[0, pl.ds(r0, tk), cols], krt], -1)
            scs.append(_dot_nt(qbuf[h], kt))
        for h in range(hp):
            sc = jnp.where(keep, scs[h], -jnp.inf) if masked else scs[h]
            m_prev = m_sc[h]
            m_new = jnp.maximum(m_prev, jnp.max(sc, -1, keepdims=True))
            alphas.append(jnp.exp2(m_prev - m_new))
            m_sc[h] = m_new
            ps.append(jnp.concatenate(
                [jnp.exp2(sc[:, c0:c0 + LANES] - m_new) for c0 in range(0, tk, LANES)], -1).astype(BF16))
        for h in range(hp):
            vcols = slice(2 * h * LANES, (2 * h + 2) * LANES)
            alpha2 = jnp.concatenate([alphas[h], alphas[h]], -1)
            acc_sc[h] = alpha2 * acc_sc[h] + _dot(ps[h], v_ref[0, pl.ds(r0, tk), vcols])

    def body(j, carry):
        step(j, False)
        return carry

    lax.fori_loop(0, i, body, 0)
    step(i, True)
    for h in range(hp):
        o_ref[0, :, h * LANES:(h + 1) * LANES] = (acc_sc[h, :, :LANES] / acc_sc[h, :, LANES:]).astype(BF16)


def _mla_layer(x, positions, sc, sh, gate, ln_g, ln_b, w_in, q_norm_g, kv_norm_g, w_uq, w_ukv, w_out,
               tm=1024, tq=512, hp=4):
    bsz, s, d = x.shape
    nh = MLA_HEADS
    half = MLA_ROPE // 2
    rot = jnp.concatenate([jnp.arange(half, MLA_ROPE), jnp.arange(half)])
    sign = jnp.concatenate([-jnp.ones((half,), F32), jnp.ones((half,), F32)])
    zpad = jnp.zeros((d, LANES - MLA_ROPE), F32)
    w_kr = w_in[:, MLA_Q_RANK + MLA_KV_RANK:]
    w_in_ext = jnp.concatenate(
        [w_in[:, :MLA_Q_RANK + MLA_KV_RANK], w_kr, zpad, w_kr[:, rot] * sign, zpad], 1).astype(BF16)
    wq = w_uq.reshape(MLA_Q_RANK, nh, MLA_NOPE + MLA_ROPE)
    w_qn = wq[:, :, :MLA_NOPE].reshape(MLA_Q_RANK, nh * MLA_NOPE).astype(BF16)
    wq_r = wq[:, :, MLA_NOPE:]
    hpad = jnp.zeros((MLA_Q_RANK, nh, LANES - MLA_ROPE), F32)
    w_qr = jnp.concatenate([wq_r, hpad], -1).reshape(MLA_Q_RANK, nh * LANES).astype(BF16)
    w_qt = jnp.concatenate([wq_r[:, :, rot] * sign, hpad], -1).reshape(MLA_Q_RANK, nh * LANES).astype(BF16)
    wkv = w_ukv.reshape(MLA_KV_RANK, nh, MLA_NOPE + MLA_V)
    w_kv = jnp.concatenate([wkv[:, :, :MLA_NOPE].reshape(MLA_KV_RANK, nh * MLA_NOPE),
                            wkv[:, :, MLA_NOPE:].reshape(MLA_KV_RANK, nh * MLA_V)], 1).astype(BF16)
    n_in = w_in_ext.shape[1]
    wide = nh * LANES
    qn, qr, kn, kr, v = pl.pallas_call(
        _mla_in_kernel,
        grid=(bsz, s // tm),
        in_specs=[_row_spec(tm, d), _mod_spec(), _mod_spec(), _row_spec(tm, 1),
                  _full_spec((d, n_in)), _full_spec((1, MLA_Q_RANK)), _full_spec((1, MLA_KV_RANK)),
                  _full_spec((MLA_Q_RANK, wide)), _full_spec((MLA_Q_RANK, wide)),
                  _full_spec((MLA_Q_RANK, wide)), _full_spec((MLA_KV_RANK, 2 * wide))],
        out_specs=[_row_spec(tm, wide), _row_spec(tm, wide), _row_spec(tm, wide), _row_spec(tm, LANES),
                   _row_spec(tm, 2 * wide)],
        out_shape=[jax.ShapeDtypeStruct((bsz, s, wide), BF16), jax.ShapeDtypeStruct((bsz, s, wide), BF16),
                   jax.ShapeDtypeStruct((bsz, s, wide), BF16), jax.ShapeDtypeStruct((bsz, s, LANES), BF16),
                   jax.ShapeDtypeStruct((bsz, s, 2 * wide), BF16)],
        compiler_params=_cparams(("parallel", "parallel")),
        name="mla_in",
    )(x, sc, sh, positions.reshape(bsz, s, 1), w_in_ext, q_norm_g.reshape(1, -1), kv_norm_g.reshape(1, -1),
      w_qn, w_qr, w_qt, w_kv)

    gw = hp * LANES
    qspec = pl.BlockSpec((1, tq, gw), lambda b, h, i: (b, i, h))
    kspec = pl.BlockSpec((1, s, gw), lambda b, h, i: (b, 0, h))
    o = pl.pallas_call(
        functools.partial(_mla_attn_kernel, hp=hp),
        grid=(bsz, nh // hp, s // tq),
        in_specs=[qspec, qspec, kspec, pl.BlockSpec((1, s, LANES), lambda b, h, i: (b, 0, 0)),
                  pl.BlockSpec((1, s, 2 * gw), lambda b, h, i: (b, 0, h))],
        out_specs=qspec,
        out_shape=jax.ShapeDtypeStruct((bsz, s, wide), BF16),
        scratch_shapes=[pltpu.VMEM((hp, tq, 2 * LANES), BF16), pltpu.VMEM((hp, tq, LANES), F32),
                        pltpu.VMEM((hp, tq, 2 * MLA_V), F32)],
        compiler_params=_cparams(("parallel", "parallel", "arbitrary")),
        name="mla_attn",
    )(qn, qr, kn, kr, v)
    return _outproj_ln(o, w_out.astype(BF16), x, gate, ln_g, ln_b)


def _router_kernel(x_ref, sc_ref, sh_ref, wr_ref, meta_ref, cnt_ref, run):
    tm = x_ref.shape[1]

    @pl.when(pl.program_id(0) == 0)
    def _():
        run[...] = jnp.zeros_like(run)

    h = x_ref[0] * (1.0 + sc_ref[0]) + sh_ref[0]
    lane = lax.broadcasted_iota(jnp.int32, (tm, LANES), 1)
    lane_f = lane.astype(F32)
    logits = jnp.where(lane < N_EXPERTS, _dot3(h, wr_ref[...]), -jnp.inf)
    m1 = jnp.max(logits, -1, keepdims=True)
    i1 = jnp.min(jnp.where(logits == m1, lane_f, float(LANES)), -1, keepdims=True)
    oh1 = lane_f == i1
    rest = jnp.where(oh1, -jnp.inf, logits)
    m2 = jnp.max(rest, -1, keepdims=True)
    i2 = jnp.min(jnp.where(rest == m2, lane_f, float(LANES)), -1, keepdims=True)
    oh2 = lane_f == i2
    e21 = jnp.exp(m2 - m1)
    g1 = 1.0 / (1.0 + e21)
    g2 = e21 / (1.0 + e21)
    cnt = oh1.astype(F32) + oh2.astype(F32)
    ri = lax.broadcasted_iota(jnp.int32, (tm, tm), 0)
    ci = lax.broadcasted_iota(jnp.int32, (tm, tm), 1)
    before = _dot((ci < ri).astype(BF16), cnt.astype(BF16)) + run[...]
    r1 = jnp.sum(jnp.where(oh1, before, 0.0), -1, keepdims=True)
    r2 = jnp.sum(jnp.where(oh2, before, 0.0), -1, keepdims=True)
    meta = jnp.zeros((tm, LANES), F32)
    for k, col in enumerate((i1, i2, g1, g2, r1, r2)):
        meta = jnp.where(lane == k, col, meta)
    meta_ref[...] = meta
    run[...] = run[...] + jnp.sum(cnt, 0, keepdims=True)
    cnt_ref[...] = run[...]


def _dispatch_kernel(tbl_sm, x_ref, sc_ref, sh_ref, dest_hbm, xb_hbm, hbuf, zbuf, idx_sm, rsem, sem, isem, *,
                     first_tail_block):
    tm = x_ref.shape[1]
    i = pl.program_id(0)
    last = pl.num_programs(0) - 1
    slot = i % 2
    hslot = hbuf.at[slot]

    def wait_rows(s):
        for _ in range(TOP_K):
            pltpu.make_async_copy(hbuf.at[s], xb_hbm.at[pl.ds(0, tm)], rsem.at[s]).wait()

    icp = pltpu.make_async_copy(dest_hbm.at[i], idx_sm, isem)
    icp.start()

    @pl.when(i >= 2)
    def _():
        wait_rows(slot)

    hslot[...] = x_ref[0] * (1.0 + sc_ref[0]) + sh_ref[0]
    icp.wait()

    def issue(r, carry):
        pltpu.make_async_copy(hslot.at[pl.ds(r, 1)], xb_hbm.at[pl.ds(idx_sm[2 * r], 1)], rsem.at[slot]).start()
        pltpu.make_async_copy(hslot.at[pl.ds(r, 1)], xb_hbm.at[pl.ds(idx_sm[2 * r + 1], 1)], rsem.at[slot]).start()
        return carry

    lax.fori_loop(0, tm, issue, 0, unroll=8)

    @pl.when(i == last)
    def _():
        @pl.when(i >= 1)
        def _():
            wait_rows(1 - slot)

        wait_rows(slot)
        zbuf[...] = jnp.zeros_like(zbuf)
        n_blocks = xb_hbm.shape[0] // MOE_BLOCK

        def pad_copies(e):
            start = tbl_sm[e]
            end = tbl_sm[N_EXPERTS + e]
            n1 = (-start) & (SUBLANES - 1)
            a0 = start + n1
            l8 = end - a0
            out = []
            for r in range(SUBLANES - 1):
                out.append((r < n1, pltpu.make_async_copy(zbuf.at[pl.ds(r, 1)], xb_hbm.at[pl.ds(start + r, 1)], sem)))
            sz = MOE_BLOCK // 2
            while sz >= SUBLANES:
                off = pl.multiple_of(a0 + (l8 & ~(2 * sz - 1)), SUBLANES)
                out.append(((l8 & sz) != 0,
                            pltpu.make_async_copy(zbuf.at[pl.ds(0, sz)], xb_hbm.at[pl.ds(off, sz)], sem)))
                sz //= 2
            return out

        def tail_copies():
            nvalid = tbl_sm[2 * N_EXPERTS]
            return [(bi >= nvalid,
                     pltpu.make_async_copy(zbuf, xb_hbm.at[pl.ds(bi * MOE_BLOCK, MOE_BLOCK)], sem))
                    for bi in range(first_tail_block, n_blocks)]

        def start_all(e, carry):
            for pred, cp in pad_copies(e):
                pl.when(pred)(cp.start)
            return carry

        def wait_all(e, carry):
            for pred, cp in pad_copies(e):
                pl.when(pred)(cp.wait)
            return carry

        lax.fori_loop(0, N_EXPERTS, start_all, 0)
        for pred, cp in tail_copies():
            pl.when(pred)(cp.start)
        lax.fori_loop(0, N_EXPERTS, wait_all, 0)
        for pred, cp in tail_copies():
            pl.when(pred)(cp.wait)


def _expert_kernel(be_sm, nv_sm, x_ref, wa_ref, wb_ref, wo_ref, o_ref, hbuf, acc):
    i = pl.program_id(0)
    j = pl.program_id(1)

    @pl.when((i >= nv_sm[0]) & (j == pl.num_programs(1) - 1))
    def _():
        o_ref[...] = jnp.zeros_like(o_ref)

    @pl.when(i < nv_sm[0])
    def _():
        @pl.when(j == 0)
        def _():
            hbuf[...] = x_ref[...].astype(BF16)
            acc[...] = jnp.zeros_like(acc)

        _swiglu_chunk(hbuf, wa_ref[0, 0].astype(BF16), wb_ref[0, 0].astype(BF16), wo_ref[0, 0].astype(BF16), acc)

        @pl.when(j == pl.num_programs(1) - 1)
        def _():
            o_ref[...] = acc[...]


def _combine_kernel(x_ref, gate_ref, meta_ref, g_ref, b_ref, dest_hbm, yb_hbm, o_ref,
                    y0, y1, idx_sm, sem, isem):
    tm = x_ref.shape[1]
    i = pl.program_id(0)
    slot = i % 2

    def gather(tile, s):
        icp = pltpu.make_async_copy(dest_hbm.at[tile], idx_sm, isem)
        icp.start()
        icp.wait()
        d0, d1 = y0.at[s], y1.at[s]

        def issue(r, carry):
            pltpu.make_async_copy(yb_hbm.at[pl.ds(idx_sm[2 * r], 1)], d0.at[pl.ds(r, 1)], sem.at[s]).start()
            pltpu.make_async_copy(yb_hbm.at[pl.ds(idx_sm[2 * r + 1], 1)], d1.at[pl.ds(r, 1)], sem.at[s]).start()
            return carry

        lax.fori_loop(0, tm, issue, 0, unroll=8)

    @pl.when(i == 0)
    def _():
        gather(0, 0)

    @pl.when(i + 1 < pl.num_programs(0))
    def _():
        gather(i + 1, 1 - slot)

    for buf in (y0, y1):
        pltpu.make_async_copy(yb_hbm.at[pl.ds(0, tm)], buf.at[slot], sem.at[slot]).wait()
    meta = meta_ref[...]
    y = meta[:, 2:3] * y0[slot] + meta[:, 3:4] * y1[slot]
    o_ref[0] = _res_ln(x_ref[0], y, gate_ref[0], g_ref[...], b_ref[...])


def _moe_layer(x, sc, sh, gate, ln_g, ln_b, router, w_in, w_out, layer, tm=1024, tr=1024, th=512):
    bsz, s, d = x.shape
    n = bsz * s
    nt = n // tm
    spt = s // tm
    hid = w_out.shape[-2]
    nj = hid // th
    min_blocks = n * TOP_K // MOE_BLOCK
    nb = min_blocks + N_EXPERTS
    tile = (d,)

    row1 = pl.BlockSpec((1, tm, d), lambda i: (i // spt, i % spt, 0))
    mod1 = pl.BlockSpec((1, 1, d), lambda i: (i // spt, 0, 0))
    slab1 = pl.BlockSpec((tm, LANES), lambda i: (i, 0))
    w_r = jnp.concatenate([router, jnp.zeros((d, LANES - N_EXPERTS), F32)], 1)
    rpt = s // tr
    meta, counts = pl.pallas_call(
        _router_kernel,
        grid=(n // tr,),
        in_specs=[pl.BlockSpec((1, tr, d), lambda i: (i // rpt, i % rpt, 0)),
                  pl.BlockSpec((1, 1, d), lambda i: (i // rpt, 0, 0)),
                  pl.BlockSpec((1, 1, d), lambda i: (i // rpt, 0, 0)),
                  pl.BlockSpec((d, LANES), lambda i: (0, 0))],
        out_specs=[pl.BlockSpec((tr, LANES), lambda i: (i, 0)), pl.BlockSpec((1, LANES), lambda i: (0, 0))],
        out_shape=[jax.ShapeDtypeStruct((n, LANES), F32), jax.ShapeDtypeStruct((1, LANES), F32)],
        scratch_shapes=[pltpu.VMEM((1, LANES), F32)],
        compiler_params=_cparams(("arbitrary",)),
        name="moe_router",
    )(x, sc, sh, w_r)

    cnt = counts[0, :N_EXPERTS].astype(jnp.int32)
    nblk = (cnt + MOE_BLOCK - 1) // MOE_BLOCK
    ends = jnp.cumsum(nblk)
    first_row = (ends - nblk) * MOE_BLOCK
    nvalid = ends[-1]
    eid = meta[:, :TOP_K].astype(jnp.int32)
    rank = meta[:, 4:4 + TOP_K].astype(jnp.int32)
    base = jnp.sum(jnp.where(eid[:, :, None] == jnp.arange(N_EXPERTS), first_row, 0), -1)
    dest = (base + rank).reshape(nt, TOP_K * tm)
    tbl = jnp.concatenate([first_row + cnt, ends * MOE_BLOCK, nvalid.reshape(1)]).astype(jnp.int32)
    bi = jnp.minimum(jnp.arange(nb, dtype=jnp.int32), nvalid - 1)
    blk_e = jnp.sum(bi[:, None] >= ends[None, :], -1).astype(jnp.int32)

    row1p = pl.BlockSpec((1, tm, d), lambda i, t: (i // spt, i % spt, 0))
    mod1p = pl.BlockSpec((1, 1, d), lambda i, t: (i // spt, 0, 0))
    xb = pl.pallas_call(
        functools.partial(_dispatch_kernel, first_tail_block=min_blocks),
        grid_spec=pltpu.PrefetchScalarGridSpec(
            num_scalar_prefetch=1,
            grid=(nt,),
            in_specs=[row1p, mod1p, mod1p, pl.BlockSpec(memory_space=pl.ANY)],
            out_specs=pl.BlockSpec(memory_space=pl.ANY),
            scratch_shapes=[pltpu.VMEM((2, tm) + tile, F32), pltpu.VMEM((MOE_BLOCK,) + tile, F32),
                            pltpu.SMEM((TOP_K * tm,), jnp.int32),
                            pltpu.SemaphoreType.DMA((2,)), pltpu.SemaphoreType.DMA, pltpu.SemaphoreType.DMA]),
        out_shape=jax.ShapeDtypeStruct((nb * MOE_BLOCK,) + tile, F32),
        compiler_params=_cparams(("arbitrary",)),
        name="moe_dispatch",
    )(tbl, x, sc, sh, dest)

    last = nj - 1

    def jsel(i, j, nv):
        return jnp.where(i < nv[0], j, last)

    yb = pl.pallas_call(
        _expert_kernel,
        grid_spec=pltpu.PrefetchScalarGridSpec(
            num_scalar_prefetch=2,
            grid=(nb, nj),
            in_specs=[pl.BlockSpec((MOE_BLOCK,) + tile, lambda i, j, be, nv: (i, 0)),
                      pl.BlockSpec((1, 1, d, th), lambda i, j, be, nv: (layer, be[i], 0, jsel(i, j, nv))),
                      pl.BlockSpec((1, 1, d, th), lambda i, j, be, nv: (layer, be[i], 0, jsel(i, j, nv) + nj)),
                      pl.BlockSpec((1, 1, th, d), lambda i, j, be, nv: (layer, be[i], jsel(i, j, nv), 0))],
            out_specs=pl.BlockSpec((MOE_BLOCK,) + tile, lambda i, j, be, nv: (i, 0)),
            scratch_shapes=[pltpu.VMEM((MOE_BLOCK, d), BF16), pltpu.VMEM((MOE_BLOCK, d), F32)]),
        out_shape=jax.ShapeDtypeStruct((nb * MOE_BLOCK,) + tile, F32),
        compiler_params=_cparams(("arbitrary", "arbitrary")),
        name="moe_experts",
    )(blk_e, nvalid.reshape(1).astype(jnp.int32), xb, w_in, w_in, w_out)

    return pl.pallas_call(
        _combine_kernel,
        grid=(nt,),
        in_specs=[row1, mod1, slab1, pl.BlockSpec((1, d), lambda i: (0, 0)), pl.BlockSpec((1, d), lambda i: (0, 0)),
                  pl.BlockSpec(memory_space=pl.ANY), pl.BlockSpec(memory_space=pl.ANY)],
        out_specs=row1,
        out_shape=jax.ShapeDtypeStruct(x.shape, F32),
        scratch_shapes=[pltpu.VMEM((2, tm) + tile, F32), pltpu.VMEM((2, tm) + tile, F32),
                        pltpu.SMEM((TOP_K * tm,), jnp.int32),
                        pltpu.SemaphoreType.DMA((2,)), pltpu.SemaphoreType.DMA],
        compiler_params=_cparams(("arbitrary",)),
        name="moe_combine",
    )(x, gate, meta, ln_g, ln_b, dest, yb)


def kernel(x, c, positions, ada_w, ada_b, ln_g, ln_b, conv_w_in, conv_dw, conv_dw_b, conv_ln_g, conv_ln_b, conv_w_out, sgu_w_in, sgu_b_in, sgu_ln_g, sgu_ln_b, sgu_w_s, sgu_b_s, sgu_w_out, gdn_w_in, gdn_conv, gdn_a_log, gdn_dt_bias, gdn_norm_g, gdn_w_out, mla_w_in, mla_q_norm_g, mla_kv_norm_g, mla_w_uq, mla_w_ukv, mla_w_out, ffn_w_in, ffn_w_out, moe_router, moe_w_in, moe_w_out):
    bsz, s, d = x.shape
    mod = _ada_mod(c, ada_w, ada_b).reshape(DEPTH, bsz, 6, 1, d)
    for i in range(DEPTH):
        sh1, sc1, g1, sh2, sc2, g2 = (mod[i, :, t] for t in range(6))
        lg1, lb1 = ln_g[i, 0].reshape(1, d), ln_b[i, 0].reshape(1, d)
        lg2, lb2 = ln_g[i, 1].reshape(1, d), ln_b[i, 1].reshape(1, d)
        j = i // 4
        mixer = i % 4
        if mixer == 0:
            x = _conformer_layer(x, sc1, sh1, g1, lg1, lb1, conv_w_in[j], conv_dw[j], conv_dw_b[j],
                                 conv_ln_g[j], conv_ln_b[j], conv_w_out[j])
        elif mixer == 1:
            x = _sgu_layer(x, sc1, sh1, g1, lg1, lb1, sgu_w_in[j], sgu_b_in[j], sgu_ln_g[j], sgu_ln_b[j],
                           sgu_w_s[j], sgu_b_s[j], sgu_w_out[j])
        elif mixer == 2:
            x = _gdn_layer(x, sc1, sh1, g1, lg1, lb1, gdn_w_in[j], gdn_conv[j], gdn_a_log[j], gdn_dt_bias[j],
                           gdn_norm_g[j], gdn_w_out[j])
        else:
            x = _mla_layer(x, positions, sc1, sh1, g1, lg1, lb1, mla_w_in[j], mla_q_norm_g[j],
                           mla_kv_norm_g[j], mla_w_uq[j], mla_w_ukv[j], mla_w_out[j])
        if i % 2 == 0:
            x = _swiglu_layer(x, sc2, sh2, g2, lg2, lb2, ffn_w_in[i // 2], ffn_w_out[i // 2])
        else:
            x = _moe_layer(x, sc2, sh2, g2, lg2, lb2, moe_router[i // 2], moe_w_in, moe_w_out, i // 2)
    return x
```
